```python
import jax, jax.numpy as jnp
from jax import lax
import numpy as np

D_MODEL = 1024
BATCH = 8
SEQ = 2048
DEPTH = 1
DEC_BATCH = 128
DEC_SEQ = 1
PAST_LEN = 16384
PAGE_SIZE = 128

H_R = 4
DK_R = 128
DV_R = 256
H_G = 8
DK_G = 128
DV_G = 128
D_FF = 4 * D_MODEL
D_PLE = 256
CHUNK = 64
ROPE_BASE = 10000.0
EPS = 1e-6
QK_R = H_R * DK_R
V_R = H_R * DV_R
F_G = H_G * DK_G
V_G = H_G * DV_G
IN_SPLITS = (QK_R, QK_R, V_R, V_R, F_G, F_G, V_G, V_G, D_MODEL, D_MODEL)
D_IN = 2 * QK_R + 2 * V_R + 2 * F_G + 2 * V_G + 2 * D_MODEL

kernel_name = "retnet_hgrn2_gated_parallel_decoder_step"

F32 = jnp.float32


def rms_norm(x, g):
    xf = x.astype(F32)
    y = xf * lax.rsqrt(jnp.mean(xf * xf, axis=-1, keepdims=True) + EPS)
    return (y * g.astype(F32)).astype(x.dtype)


def head_norm_gate(o, gain, gate):
    B, T, H, Dv = o.shape
    o = o * lax.rsqrt(jnp.mean(o * o, axis=-1, keepdims=True) + EPS)
    return o.reshape(B, T, H * Dv) * gain.astype(F32) * jax.nn.silu(gate.astype(F32))


def rotary(x, pos):
    half = x.shape[-1] // 2
    inv = ROPE_BASE ** (-jnp.arange(half, dtype=F32) / half)
    ang = pos[:, None] * inv[None, :]
    cos = jnp.cos(ang)[None, :, None, :]
    sin = jnp.sin(ang)[None, :, None, :]
    x1, x2 = x[..., :half], x[..., half:]
    return jnp.concatenate([x1 * cos - x2 * sin, x1 * sin + x2 * cos], axis=-1)


def chunk_len(T):
    return CHUNK if T % CHUNK == 0 else T


def to_chunks(x, C):
    B, T, H, D = x.shape
    return x.reshape(B, T // C, C, H, D).transpose(1, 0, 3, 2, 4)


def from_chunks(o):
    n, B, H, C, D = o.shape
    return o.transpose(1, 0, 3, 2, 4).reshape(B, n * C, H, D)


def retention_chunked(q, k, v, log_gamma, S0):
    T = q.shape[1]
    C = chunk_len(T)
    idx = jnp.arange(C, dtype=F32)
    rel = idx[:, None] - idx[None, :]
    lg = log_gamma[:, None, None]
    decay_mask = jnp.where(rel >= 0, jnp.exp(lg * jnp.maximum(rel, 0.0)), 0.0)
    q_decay = jnp.exp(log_gamma[:, None] * (idx + 1.0))[None, :, :, None]
    k_decay = jnp.exp(log_gamma[:, None] * (C - 1.0 - idx))[None, :, :, None]
    chunk_decay = jnp.exp(log_gamma * C)[None, :, None, None]

    def step(S, inp):
        qb, kb, vb = inp
        scores = jnp.einsum('bhtk,bhsk->bhts', qb, kb) * decay_mask[None]
        o = jnp.einsum('bhts,bhsv->bhtv', scores, vb) + jnp.einsum('bhtk,bhkv->bhtv', qb * q_decay, S)
        S = chunk_decay * S + jnp.einsum('bhsk,bhsv->bhkv', kb * k_decay, vb)
        return S, o

    S, o = lax.scan(step, S0, (to_chunks(q, C), to_chunks(k, C), to_chunks(v, C)))
    return from_chunks(o), S


def gla_chunked(q, k, v, log_f, S0):
    T = q.shape[1]
    C = chunk_len(T)
    ar = jnp.arange(C)
    causal = (ar[:, None] >= ar[None, :])[None, None, :, :, None]

    def step(S, inp):
        qb, kb, vb, lfb = inp
        b = jnp.cumsum(lfb, axis=2)
        diff = b[:, :, :, None, :] - b[:, :, None, :, :]
        dec = jnp.where(causal, jnp.exp(jnp.minimum(diff, 0.0)), 0.0)
        A = jnp.einsum('bhtk,bhsk,bhtsk->bhts', qb, kb, dec)
        o = jnp.einsum('bhts,bhsv->bhtv', A, vb) + jnp.einsum('bhtk,bhkv->bhtv', qb * jnp.exp(b), S)
        b_last = b[:, :, -1:, :]
        S = jnp.exp(b_last[:, :, 0, :])[..., None] * S + jnp.einsum('bhsk,bhsv->bhkv', kb * jnp.exp(b_last - b), vb)
        return S, o

    S, o = lax.scan(step, S0, (to_chunks(q, C), to_chunks(k, C), to_chunks(v, C), to_chunks(log_f, C)))
    return from_chunks(o), S


def layer_step(r, p_l, s_ret0, s_hg0, pos, lb, norm_mix_g, w_in, ret_norm_g, hg_norm_g,
               w_up_ret, w_up_hg, w_o, norm_mlp_g, w_ff1, w_ff2, norm_ple_g, w_ple_gate, w_ple_proj):
    B, T, _ = r.shape
    h = rms_norm(r, norm_mix_g)
    z = h @ w_in
    split_at = [int(i) for i in np.cumsum(IN_SPLITS)[:-1]]
    rq, rk, rv, rg, gq, gf, gi, gg, a_ret, a_hg = jnp.split(z, split_at, axis=-1)

    log_gamma = jnp.log(1.0 - 2.0 ** (-5.0 - jnp.arange(H_R, dtype=F32)))
    q_r = rotary(rq.astype(F32).reshape(B, T, H_R, DK_R), pos)
    k_r = rotary(rk.astype(F32).reshape(B, T, H_R, DK_R), pos) * (DK_R ** -0.5)
    v_r = rv.astype(F32).reshape(B, T, H_R, DV_R)
    o_r, s_ret = retention_chunked(q_r, k_r, v_r, log_gamma, s_ret0.astype(F32))
    u_r = head_norm_gate(o_r, ret_norm_g, rg).astype(r.dtype) @ w_up_ret

    gf32 = gf.astype(F32)
    log_f = jnp.logaddexp(jnp.log(lb), jnp.log1p(-lb) + jax.nn.log_sigmoid(gf32))
    k_g = (1.0 - jnp.exp(log_f)).reshape(B, T, H_G, DK_G)
    q_g = jax.nn.silu(gq.astype(F32)).reshape(B, T, H_G, DK_G)
    v_g = gi.astype(F32).reshape(B, T, H_G, DV_G)
    o_g, s_hg = gla_chunked(q_g, k_g, v_g, log_f.reshape(B, T, H_G, DK_G), s_hg0.astype(F32))
    u_g = head_norm_gate(o_g, hg_norm_g, gg).astype(r.dtype) @ w_up_hg

    m = jax.nn.sigmoid(a_ret) * u_r + jax.nn.sigmoid(a_hg) * u_g
    r = r + m @ w_o

    hm = rms_norm(r, norm_mlp_g)
    r = r + jnp.square(jax.nn.relu(hm @ w_ff1)) @ w_ff2

    hp = rms_norm(r, norm_ple_g)
    r = r + jax.nn.sigmoid(hp @ w_ple_gate) * (p_l @ w_ple_proj)
    return r, s_ret, s_hg


def setup_inputs(seed: int = 0) -> dict:
    key = jax.random.key(seed)
    ks = jax.random.split(key, 24)
    n = lambda k, s, sc: jax.random.normal(k, s, F32) * sc
    gain = lambda k, s: 1.0 + 0.02 * jax.random.normal(k, s, F32)
    return {
        "x_prompt": n(ks[0], (BATCH, SEQ, D_MODEL), 1.0),
        "x_sample": n(ks[1], (DEC_BATCH, DEC_SEQ, D_MODEL), 1.0),
        "state_ret": n(ks[2], (DEPTH, DEC_BATCH, H_R, DK_R, DV_R), 0.5),
        "state_hgrn": n(ks[3], (DEPTH, DEC_BATCH, H_G, DK_G, DV_G), 0.5),
        "p_prompt": n(ks[4], (DEPTH, BATCH, SEQ, D_PLE), 1.0),
        "p_sample": n(ks[5], (DEPTH, DEC_BATCH, DEC_SEQ, D_PLE), 1.0),
        "norm_mix_g": gain(ks[6], (DEPTH, D_MODEL)),
        "w_in": n(ks[7], (DEPTH, D_MODEL, D_IN), D_MODEL ** -0.5),
        "ret_norm_g": gain(ks[8], (DEPTH, V_R)),
        "hg_norm_g": gain(ks[9], (DEPTH, V_G)),
        "hg_lb": n(ks[10], (DEPTH + 1, F_G), 0.1),
        "w_up_ret": n(ks[11], (DEPTH, V_R, D_MODEL), V_R ** -0.5),
        "w_up_hg": n(ks[12], (DEPTH, V_G, D_MODEL), V_G ** -0.5),
        "w_o": n(ks[13], (DEPTH, D_MODEL, D_MODEL), D_MODEL ** -0.5),
        "norm_mlp_g": gain(ks[14], (DEPTH, D_MODEL)),
        "w_ff1": n(ks[15], (DEPTH, D_MODEL, D_FF), D_MODEL ** -0.5),
        "w_ff2": n(ks[16], (DEPTH, D_FF, D_MODEL), D_FF ** -0.5),
        "norm_ple_g": gain(ks[17], (DEPTH, D_MODEL)),
        "w_ple_gate": n(ks[18], (DEPTH, D_MODEL, D_MODEL), D_MODEL ** -0.5),
        "w_ple_proj": n(ks[19], (DEPTH, D_PLE, D_MODEL), D_PLE ** -0.5),
        "norm_final_g": gain(ks[20], (D_MODEL,)),
    }


def reference(x_prompt, x_sample, state_ret, state_hgrn, p_prompt, p_sample, norm_mix_g, w_in,
              ret_norm_g, hg_norm_g, hg_lb, w_up_ret, w_up_hg, w_o, norm_mlp_g, w_ff1, w_ff2,
              norm_ple_g, w_ple_gate, w_ple_proj, norm_final_g):
    B, T = x_prompt.shape[0], x_prompt.shape[1]
    Td = x_sample.shape[1]
    pos_p = jnp.arange(T, dtype=F32)
    pos_s = PAST_LEN + jnp.arange(Td, dtype=F32)
    lb_all = jnp.cumsum(jax.nn.softmax(hg_lb.astype(F32), axis=0), axis=0)
    rp, rs = x_prompt, x_sample
    ret_p, hg_p, ret_s, hg_s = [], [], [], []
    for l in range(DEPTH):
        wl = (norm_mix_g[l], w_in[l], ret_norm_g[l], hg_norm_g[l], w_up_ret[l], w_up_hg[l], w_o[l],
              norm_mlp_g[l], w_ff1[l], w_ff2[l], norm_ple_g[l], w_ple_gate[l], w_ple_proj[l])
        zr = jnp.zeros((B, H_R, DK_R, DV_R), F32)
        zg = jnp.zeros((B, H_G, DK_G, DV_G), F32)
        rp, sr_p, sg_p = layer_step(rp, p_prompt[l], zr, zg, pos_p, lb_all[l], *wl)
        rs, sr_s, sg_s = layer_step(rs, p_sample[l], state_ret[l], state_hgrn[l], pos_s, lb_all[l], *wl)
        ret_p.append(sr_p.astype(state_ret.dtype))
        hg_p.append(sg_p.astype(state_hgrn.dtype))
        ret_s.append(sr_s.astype(state_ret.dtype))
        hg_s.append(sg_s.astype(state_hgrn.dtype))
    y_prompt = rms_norm(rp, norm_final_g)
    y_sample = rms_norm(rs, norm_final_g)
    return (y_prompt, y_sample, jnp.stack(ret_p), jnp.stack(hg_p), jnp.stack(ret_s), jnp.stack(hg_s))
```

```python
import functools

import jax
import jax.numpy as jnp
import numpy as np
from jax import lax
from jax.experimental import pallas as pl
from jax.experimental.pallas import tpu as pltpu

F32 = jnp.float32
BF16 = jnp.bfloat16

PAST_LEN = 16384
ROPE_BASE = 10000.0
EPS = 1e-6

H_R, DK_R, DV_R = 4, 128, 256
H_G, DK_G, DV_G = 8, 128, 128
QK_R = H_R * DK_R
V_R = H_R * DV_R
F_G = H_G * DK_G
V_G = H_G * DV_G

V7X_VMEM_BYTES = 64 * 1024 * 1024
VMEM_LIMIT = V7X_VMEM_BYTES - 8 * 1024 * 1024

RET_CHUNK = 128
RET_TILE = 512
HG_CHUNK = 64
HG_SUB = 16
HG_TILE = 512

NT_DIMS = (((1,), (1,)), ((), ()))
TN_DIMS = (((0,), (0,)), ((), ()))


def _dot(a, b):
    return jnp.dot(a.astype(BF16), b.astype(BF16), preferred_element_type=F32)


def _dot_nt(a, b):
    return lax.dot_general(a.astype(BF16), b.astype(BF16), NT_DIMS, preferred_element_type=F32)


def _dot_tn(a, b):
    return lax.dot_general(a.astype(BF16), b.astype(BF16), TN_DIMS, preferred_element_type=F32)


def _rms(x, g):
    return x * lax.rsqrt(jnp.mean(x * x, axis=-1, keepdims=True) + EPS) * g


def _silu(x):
    return x * jax.nn.sigmoid(x)


def _params(sem):
    return pltpu.CompilerParams(dimension_semantics=sem, vmem_limit_bytes=VMEM_LIMIT)


def _in_proj_kernel(x_ref, g_ref, w_ref, z_ref, h_ref):
    @pl.when(pl.program_id(1) == 0)
    def _():
        h_ref[...] = _rms(x_ref[...], g_ref[...]).astype(BF16)

    z_ref[...] = jnp.dot(h_ref[...], w_ref[...], preferred_element_type=F32)


def _in_proj(x, g, w, tm, tn):
    n, d = x.shape
    d_in = w.shape[1]
    return pl.pallas_call(
        _in_proj_kernel,
        grid=(n // tm, d_in // tn),
        in_specs=[
            pl.BlockSpec((tm, d), lambda i, j: (i, 0)),
            pl.BlockSpec((1, d), lambda i, j: (0, 0)),
            pl.BlockSpec((d, tn), lambda i, j: (0, j)),
        ],
        out_specs=pl.BlockSpec((tm, tn), lambda i, j: (i, j)),
        out_shape=jax.ShapeDtypeStruct((n, d_in), F32),
        scratch_shapes=[pltpu.VMEM((tm, d), BF16)],
        compiler_params=_params(("parallel", "arbitrary")),
        name="in_proj",
    )(x, g, w)


def _rotate(x, cos, sin_signed):
    return x * cos + pltpu.roll(x, DK_R // 2, 1) * sin_signed


def _head_norm_gate(o, gain, gate):
    return o * lax.rsqrt(jnp.mean(o * o, axis=-1, keepdims=True) + EPS) * gain * _silu(gate)


def _ret_kernel(q_ref, k_ref, v_ref, g_ref, cos_ref, sin_ref, dm_ref, qd_ref, kd_ref, cd_ref,
                gain_ref, o_ref, s_ref, *, chunk, n_chunks):
    @pl.when(pl.program_id(2) == 0)
    def _():
        s_ref[...] = jnp.zeros_like(s_ref)

    dm = dm_ref[...]
    qd = qd_ref[...]
    kd = kd_ref[...]
    cd = cd_ref[...]
    gain = gain_ref[...]
    for c in range(n_chunks):
        rows = slice(c * chunk, (c + 1) * chunk)
        cos = cos_ref[rows, :]
        sin = sin_ref[rows, :]
        q = _rotate(q_ref[rows, :], cos, sin)
        k = _rotate(k_ref[rows, :], cos, sin) * (DK_R ** -0.5)
        v = v_ref[rows, :].astype(BF16)
        s = s_ref[...]
        scores = _dot_nt(q, k) * dm
        o = _dot(scores, v) + _dot(q * qd, s)
        s_ref[...] = cd * s + _dot_tn(k * kd, v)
        o_ref[rows, :] = _head_norm_gate(o, gain, g_ref[rows, :])


def _retention(z3, cos, sin, gain):
    b, t, _ = z3.shape
    c = RET_CHUNK
    tt = RET_TILE
    lg = np.log(1.0 - 2.0 ** (-5.0 - np.arange(H_R, dtype=np.float64)))
    idx = np.arange(c, dtype=np.float64)
    rel = idx[:, None] - idx[None, :]
    dm = np.where(rel >= 0, np.exp(lg[:, None, None] * np.maximum(rel, 0.0)), 0.0)
    qd = np.broadcast_to(np.exp(lg[:, None] * (idx + 1.0))[:, :, None], (H_R, c, DK_R))
    kd = np.broadcast_to(np.exp(lg[:, None] * (c - 1.0 - idx))[:, :, None], (H_R, c, DK_R))
    cd = np.broadcast_to(np.exp(lg * c)[:, None, None], (H_R, 1, DV_R))
    dm, qd, kd, cd = (jnp.asarray(a, F32) for a in (dm, qd, kd, cd))
    kq = QK_R // DK_R
    head_tab = lambda rows, cols: pl.BlockSpec((None, rows, cols), lambda bi, h, ti: (h, 0, 0))
    return pl.pallas_call(
        functools.partial(_ret_kernel, chunk=c, n_chunks=tt // c),
        grid=(b, H_R, t // tt),
        in_specs=[
            pl.BlockSpec((None, tt, DK_R), lambda bi, h, ti: (bi, ti, h)),
            pl.BlockSpec((None, tt, DK_R), lambda bi, h, ti: (bi, ti, kq + h)),
            pl.BlockSpec((None, tt, DV_R), lambda bi, h, ti: (bi, ti, 2 * QK_R // DV_R + h)),
            pl.BlockSpec((None, tt, DV_R), lambda bi, h, ti: (bi, ti, (2 * QK_R + V_R) // DV_R + h)),
            pl.BlockSpec((tt, DK_R), lambda bi, h, ti: (ti, 0)),
            pl.BlockSpec((tt, DK_R), lambda bi, h, ti: (ti, 0)),
            head_tab(c, c), head_tab(c, DK_R), head_tab(c, DK_R), head_tab(1, DV_R),
            pl.BlockSpec((1, DV_R), lambda bi, h, ti: (0, h)),
        ],
        out_specs=[
            pl.BlockSpec((None, tt, DV_R), lambda bi, h, ti: (bi, ti, h)),
            pl.BlockSpec((None, None, DK_R, DV_R), lambda bi, h, ti: (bi, h, 0, 0)),
        ],
        out_shape=[
            jax.ShapeDtypeStruct((b, t, V_R), F32),
            jax.ShapeDtypeStruct((b, H_R, DK_R, DV_R), F32),
        ],
        compiler_params=_params(("parallel", "parallel", "arbitrary")),
        name="retention",
    )(z3, z3, z3, z3, cos, sin, dm, qd, kd, cd, gain)


def _lower_bound(lb_ref):
    a = lb_ref[...]
    a0, a1 = a[0:1, :], a[1:2, :]
    m = jnp.maximum(a0, a1)
    e0 = jnp.exp(a0 - m)
    e1 = jnp.exp(a1 - m)
    return e0 / (e0 + e1)


def _forget_gate(gf, lb):
    sg = jax.nn.sigmoid(gf)
    f = lb + (1.0 - lb) * sg
    return jnp.log(f), (1.0 - lb) * (1.0 - sg), f


def _cumsum_rows(x):
    n = x.shape[0]
    row = lax.broadcasted_iota(jnp.int32, x.shape, 0)
    shift = 1
    while shift < n:
        x = x + jnp.where(row >= shift, pltpu.roll(x, shift, 0), 0.0)
        shift *= 2
    return x


def _hgrn_kernel(q_ref, f_ref, i_ref, g_ref, lb_ref, gain_ref, o_ref, s_ref,
                 st_ref, b_sc, k_sc, v_sc, p_sc, a_sc, *, chunk, sub, n_chunks):
    n_sub = chunk // sub
    half = sub // 2

    @pl.when(pl.program_id(2) == 0)
    def _():
        st_ref[...] = jnp.zeros_like(st_ref)

    lb = _lower_bound(lb_ref)
    gain = gain_ref[...]
    ones = jnp.ones((DK_G, 128), BF16)
    row8 = lax.broadcasted_iota(jnp.int32, (half, DK_G), 0)
    col = lax.broadcasted_iota(jnp.int32, (sub, chunk), 1)

    def body(c, carry):
        r0 = pl.multiple_of(c * chunk, chunk)
        rows = pl.ds(r0, chunk)
        lf, k, _ = _forget_gate(f_ref[rows, :], lb)
        q = _silu(q_ref[rows, :])
        v = i_ref[rows, :]
        b = _cumsum_rows(lf)
        b_sc[...] = b
        k_sc[...] = k
        v_sc[...] = v
        vb = v.astype(BF16)

        for j in range(n_sub):
            base = j * sub
            q_lo, q_hi = q[base:base + half], q[base + half:base + sub]
            b_lo, b_hi = b[base:base + half], b[base + half:base + sub]
            for s in range(sub):
                bs = b_sc[base + s:base + s + 1, :]
                ks = k_sc[base + s:base + s + 1, :]
                prow = (j * sub + s) * sub
                p_hi = jnp.exp(b_hi - bs) * (q_hi * ks)
                if s < half:
                    p_lo = jnp.where(row8 >= s, jnp.exp(b_lo - bs) * (q_lo * ks), 0.0)
                else:
                    p_lo = jnp.zeros_like(p_hi)
                    p_hi = jnp.where(row8 >= s - half, p_hi, 0.0)
                p_sc[prow:prow + sub, :] = jnp.concatenate([p_lo, p_hi], axis=0).astype(BF16)
        a_sc[...] = jnp.dot(p_sc[...], ones, preferred_element_type=F32)
        o_diag = []
        for j in range(n_sub):
            base = j * sub
            acc_lo = jnp.zeros((half, DV_G), F32)
            acc_hi = jnp.zeros((half, DV_G), F32)
            for s in range(sub):
                vs = v_sc[base + s:base + s + 1, :]
                prow = (j * sub + s) * sub
                if s < half:
                    acc_lo = acc_lo + a_sc[prow:prow + half, :] * vs
                acc_hi = acc_hi + a_sc[prow + half:prow + sub, :] * vs
            o_diag += [acc_lo, acc_hi]
        o = jnp.concatenate(o_diag, axis=0)

        a_rows = [jnp.zeros((sub, chunk), F32)]
        for i in range(1, n_sub):
            ref_b = b[i * sub - 1:i * sub, :]
            qt = q[i * sub:(i + 1) * sub] * jnp.exp(b[i * sub:(i + 1) * sub] - ref_b)
            kt = k * jnp.exp(jnp.minimum(ref_b - b, 0.0))
            a_rows.append(jnp.where(col < i * sub, _dot_nt(qt, kt), 0.0))
        o = o + jnp.dot(jnp.concatenate(a_rows, axis=0).astype(BF16), vb, preferred_element_type=F32)

        b_last = b[chunk - 1:chunk, :]
        st = st_ref[...]
        o = o + _dot_nt(q * jnp.exp(b), st)
        st_ref[...] = st * jnp.exp(b_last) + _dot_tn(vb, k * jnp.exp(b_last - b))
        o_ref[rows, :] = _head_norm_gate(o, gain, g_ref[rows, :])
        return carry

    lax.fori_loop(0, n_chunks, body, 0)

    @pl.when(pl.program_id(2) == pl.num_programs(2) - 1)
    def _():
        s_ref[...] = st_ref[...].T


def _hgrn(z3, hg_lb, gain):
    b, t, _ = z3.shape
    c, sub, tt = HG_CHUNK, HG_SUB, HG_TILE
    off = (2 * QK_R + 2 * V_R) // DK_G
    hb = F_G // DK_G
    col_block = lambda k: pl.BlockSpec((None, tt, DK_G), lambda bi, h, ti: (bi, ti, off + k * hb + h))
    return pl.pallas_call(
        functools.partial(_hgrn_kernel, chunk=c, sub=sub, n_chunks=tt // c),
        grid=(b, H_G, t // tt),
        in_specs=[
            col_block(0), col_block(1), col_block(2), col_block(3),
            pl.BlockSpec((2, DK_G), lambda bi, h, ti: (0, h)),
            pl.BlockSpec((1, DV_G), lambda bi, h, ti: (0, h)),
        ],
        out_specs=[
            pl.BlockSpec((None, tt, DV_G), lambda bi, h, ti: (bi, ti, h)),
            pl.BlockSpec((None, None, DK_G, DV_G), lambda bi, h, ti: (bi, h, 0, 0)),
        ],
        out_shape=[
            jax.ShapeDtypeStruct((b, t, V_G), F32),
            jax.ShapeDtypeStruct((b, H_G, DK_G, DV_G), F32),
        ],
        scratch_shapes=[
            pltpu.VMEM((DV_G, DK_G), F32),
            pltpu.VMEM((c, DK_G), F32),
            pltpu.VMEM((c, DK_G), F32),
            pltpu.VMEM((c, DV_G), F32),
            pltpu.VMEM((c * sub, DK_G), BF16),
            pltpu.VMEM((c * sub, 128), F32),
        ],
        compiler_params=_params(("parallel", "parallel", "arbitrary")),
        name="hgrn",
    )(z3, z3, z3, z3, hg_lb, gain)


def _step_kernel(z_ref, cos_ref, sin_ref, lb_ref, rgain_ref, ggain_ref, sr0_ref, sg0_ref,
                 or_ref, og_ref, sr_ref, sg_ref, *, bb):
    first = lambda a: jnp.where(lax.broadcasted_iota(jnp.int32, a.shape, 0) == 0, a, 0.0)
    cos = cos_ref[...]
    sin = sin_ref[...]
    lb_all = _lower_bound(lb_ref)
    for i in range(bb):
        z8 = lambda lo, n: jnp.broadcast_to(z_ref[i, :, lo:lo + n], (8, n))
        for h in range(H_R):
            q = _rotate(z8(h * DK_R, DK_R), cos, sin)
            k = _rotate(z8(QK_R + h * DK_R, DK_R), cos, sin) * (DK_R ** -0.5)
            v = z8(2 * QK_R + h * DV_R, DV_R)
            gate = z_ref[i, :, 2 * QK_R + V_R + h * DV_R:2 * QK_R + V_R + (h + 1) * DV_R]
            gamma = 1.0 - 2.0 ** (-5.0 - h)
            s_new = gamma * sr0_ref[i, h] + _dot_tn(first(k), v)
            sr_ref[i, h] = s_new
            o = _dot(q, s_new)[0:1, :]
            or_ref[i, :, h * DV_R:(h + 1) * DV_R] = _head_norm_gate(
                o, rgain_ref[:, h * DV_R:(h + 1) * DV_R], gate)
        g0 = 2 * QK_R + 2 * V_R
        for h in range(H_G):
            lanes = slice(h * DK_G, (h + 1) * DK_G)
            q = _silu(z8(g0 + h * DK_G, DK_G))
            _, k, f = _forget_gate(z8(g0 + F_G + h * DK_G, DK_G), lb_all[:, lanes])
            v = z8(g0 + 2 * F_G + h * DV_G, DV_G)
            gate = z_ref[i, :, g0 + 2 * F_G + V_G + h * DV_G:g0 + 2 * F_G + V_G + (h + 1) * DV_G]
            st_new = sg0_ref[i, h].T * f[0:1, :] + _dot_tn(first(v), k)
            sg_ref[i, h] = st_new.T
            o = _dot_nt(q, st_new)[0:1, :]
            og_ref[i, :, lanes] = _head_norm_gate(o, ggain_ref[:, lanes], gate)


def _step(z, cos, sin, hg_lb, rgain, ggain, sr0, sg0, bb):
    n, d_in = z.shape
    z3 = z.reshape(n, 1, d_in)
    row = lambda w: pl.BlockSpec((bb, 1, w), lambda i: (i, 0, 0))
    const = lambda a: pl.BlockSpec(a.shape, lambda i: (0,) * a.ndim)
    sr_spec = pl.BlockSpec((bb, H_R, DK_R, DV_R), lambda i: (i, 0, 0, 0))
    sg_spec = pl.BlockSpec((bb, H_G, DK_G, DV_G), lambda i: (i, 0, 0, 0))
    return pl.pallas_call(
        functools.partial(_step_kernel, bb=bb),
        grid=(n // bb,),
        in_specs=[row(d_in), const(cos), const(sin), const(hg_lb), const(rgain), const(ggain),
                  sr_spec, sg_spec],
        out_specs=[row(V_R), row(V_G), sr_spec, sg_spec],
        out_shape=[
            jax.ShapeDtypeStruct((n, 1, V_R), F32),
            jax.ShapeDtypeStruct((n, 1, V_G), F32),
            jax.ShapeDtypeStruct(sr0.shape, F32),
            jax.ShapeDtypeStruct(sg0.shape, F32),
        ],
        compiler_params=_params(("parallel",)),
        name="step",
    )(z3, cos, sin, hg_lb, rgain, ggain, sr0, sg0)


def _post_kernel(x_ref, gr_ref, gg_ref, ar_ref, ag_ref, p_ref, wur_ref, wug_ref, wo_ref,
                 nm_ref, w1_ref, w2_ref, np_ref, wg_ref, wp_ref, nf_ref, y_ref):
    u_r = jnp.dot(gr_ref[...].astype(BF16), wur_ref[...], preferred_element_type=F32)
    u_g = jnp.dot(gg_ref[...].astype(BF16), wug_ref[...], preferred_element_type=F32)
    m = jax.nn.sigmoid(ar_ref[...]) * u_r + jax.nn.sigmoid(ag_ref[...]) * u_g
    r = x_ref[...] + jnp.dot(m.astype(BF16), wo_ref[...], preferred_element_type=F32)
    hm = _rms(r, nm_ref[...]).astype(BF16)
    a = jnp.maximum(jnp.dot(hm, w1_ref[...], preferred_element_type=F32), 0.0)
    r = r + jnp.dot((a * a).astype(BF16), w2_ref[...], preferred_element_type=F32)
    hp = _rms(r, np_ref[...]).astype(BF16)
    gate = jax.nn.sigmoid(jnp.dot(hp, wg_ref[...], preferred_element_type=F32))
    r = r + gate * jnp.dot(p_ref[...].astype(BF16), wp_ref[...], preferred_element_type=F32)
    y_ref[...] = _rms(r, nf_ref[...])


def _post(x, g_r, g_g, z, p, wur, wug, wo, nm, w1, w2, npl, wg, wp, nf, tm):
    n, d = x.shape
    a_blk = (2 * QK_R + 2 * V_R + 2 * F_G + 2 * V_G) // d
    rows = lambda w, j=0: pl.BlockSpec((tm, w), lambda i: (i, j))
    const = lambda a: pl.BlockSpec(a.shape, lambda i: (0, 0), pipeline_mode=pl.Buffered(1))
    return pl.pallas_call(
        _post_kernel,
        grid=(n // tm,),
        in_specs=[rows(d), rows(d), rows(d), rows(d, a_blk), rows(d, a_blk + 1), rows(p.shape[1]),
                  const(wur), const(wug), const(wo), const(nm), const(w1), const(w2),
                  const(npl), const(wg), const(wp), const(nf)],
        out_specs=rows(d),
        out_shape=jax.ShapeDtypeStruct((n, d), F32),
        compiler_params=_params(("parallel",)),
        name="post",
    )(x, g_r, g_g, z, z, p, wur, wug, wo, nm, w1, w2, npl, wg, wp, nf)


def _rope_tables(pos):
    half = DK_R // 2
    inv = ROPE_BASE ** (-jnp.arange(half, dtype=F32) / half)
    ang = pos[:, None] * inv[None, :]
    cos, sin = jnp.cos(ang), jnp.sin(ang)
    return jnp.concatenate([cos, cos], axis=-1), jnp.concatenate([-sin, sin], axis=-1)


def kernel(x_prompt, x_sample, state_ret, state_hgrn, p_prompt, p_sample, norm_mix_g, w_in,
           ret_norm_g, hg_norm_g, hg_lb, w_up_ret, w_up_hg, w_o, norm_mlp_g, w_ff1, w_ff2,
           norm_ple_g, w_ple_gate, w_ple_proj, norm_final_g):
    b, t, d = x_prompt.shape
    nb, td, _ = x_sample.shape
    assert w_in.shape[0] == 1 and td == 1 and hg_lb.shape[0] == 2
    bf = lambda w: w[0].astype(BF16)
    row = lambda g: g.reshape(1, -1)
    w_in_b = bf(w_in)
    post_w = (bf(w_up_ret), bf(w_up_hg), bf(w_o), row(norm_mlp_g[0]), bf(w_ff1), bf(w_ff2),
              row(norm_ple_g[0]), bf(w_ple_gate), bf(w_ple_proj), row(norm_final_g))
    g_mix = row(norm_mix_g[0])
    rgain = row(ret_norm_g[0])
    ggain = row(hg_norm_g[0])

    xp = x_prompt.reshape(b * t, d)
    zp = _in_proj(xp, g_mix, w_in_b, tm=1024, tn=1024)
    zp3 = zp.reshape(b, t, -1)
    cos_p, sin_p = _rope_tables(jnp.arange(t, dtype=F32))
    gr_p, ret_p = _retention(zp3, cos_p, sin_p, rgain)
    gg_p, hg_p = _hgrn(zp3, hg_lb, ggain)
    y_p = _post(xp, gr_p.reshape(b * t, -1), gg_p.reshape(b * t, -1), zp,
                p_prompt[0].reshape(b * t, -1), *post_w, tm=256)

    xs = x_sample.reshape(nb, d)
    zs = _in_proj(xs, g_mix, w_in_b, tm=nb, tn=1024)
    cos_s, sin_s = _rope_tables(PAST_LEN + jnp.arange(td, dtype=F32))
    gr_s, gg_s, ret_s, hg_s = _step(zs, cos_s, sin_s, hg_lb, rgain, ggain,
                                    state_ret[0], state_hgrn[0], bb=2)
    y_s = _post(xs, gr_s.reshape(nb, -1), gg_s.reshape(nb, -1), zs,
                p_sample[0].reshape(nb, -1), *post_w, tm=nb)

    return (y_p.reshape(b, t, d), y_s.reshape(nb, td, d), ret_p[None], hg_p[None],
            ret_s[None], hg_s[None])
```

```python
import functools

import jax
import jax.numpy as jnp
import numpy as np
from jax import lax
from jax.experimental import pallas as pl
from jax.experimental.pallas import tpu as pltpu

F32 = jnp.float32
BF16 = jnp.bfloat16

PAST_LEN = 16384
ROPE_BASE = 10000.0
EPS = 1e-6

H_R, DK_R, DV_R = 4, 128, 256
H_G, DK_G, DV_G = 8, 128, 128
QK_R = H_R * DK_R
V_R = H_R * DV_R
F_G = H_G * DK_G
V_G = H_G * DV_G

V7X_VMEM_BYTES = 64 * 1024 * 1024
VMEM_LIMIT = V7X_VMEM_BYTES - 8 * 1024 * 1024

RET_CHUNK = 128
RET_TILE = 512
HG_CHUNK = 64
HG_SUB = 8
HG_TILE = 1024

NT_DIMS = (((1,), (1,)), ((), ()))
TN_DIMS = (((0,), (0,)), ((), ()))


def _dot(a, b):
    return jnp.dot(a.astype(BF16), b.astype(BF16), preferred_element_type=F32)


def _dot_nt(a, b):
    return lax.dot_general(a.astype(BF16), b.astype(BF16), NT_DIMS, preferred_element_type=F32)


def _dot_tn(a, b):
    return lax.dot_general(a.astype(BF16), b.astype(BF16), TN_DIMS, preferred_element_type=F32)


def _rms(x, g):
    return x * lax.rsqrt(jnp.mean(x * x, axis=-1, keepdims=True) + EPS) * g


def _sigmoid(x):
    return 0.5 * jnp.tanh(0.5 * x) + 0.5


def _silu(x):
    return x * _sigmoid(x)


def _params(sem):
    return pltpu.CompilerParams(dimension_semantics=sem, vmem_limit_bytes=VMEM_LIMIT)


def _in_proj_kernel(x_ref, g_ref, w_ref, z_ref, h_ref):
    @pl.when(pl.program_id(1) == 0)
    def _():
        h_ref[...] = _rms(x_ref[...], g_ref[...]).astype(BF16)

    z_ref[...] = jnp.dot(h_ref[...], w_ref[...], preferred_element_type=F32)


def _in_proj(x, g, w, tm, tn):
    n, d = x.shape
    d_in = w.shape[1]
    return pl.pallas_call(
        _in_proj_kernel,
        grid=(n // tm, d_in // tn),
        in_specs=[
            pl.BlockSpec((tm, d), lambda i, j: (i, 0)),
            pl.BlockSpec((1, d), lambda i, j: (0, 0)),
            pl.BlockSpec((d, tn), lambda i, j: (0, j)),
        ],
        out_specs=pl.BlockSpec((tm, tn), lambda i, j: (i, j)),
        out_shape=jax.ShapeDtypeStruct((n, d_in), F32),
        scratch_shapes=[pltpu.VMEM((tm, d), BF16)],
        compiler_params=_params(("parallel", "arbitrary")),
        name="in_proj",
    )(x, g, w)


def _rotate(x, cos, sin_signed):
    return x * cos + pltpu.roll(x, DK_R // 2, 1) * sin_signed


def _head_norm_gate(o, gain, gate):
    return o * lax.rsqrt(jnp.mean(o * o, axis=-1, keepdims=True) + EPS) * gain * _silu(gate)


def _ret_kernel(q_ref, k_ref, v_ref, g_ref, cos_ref, sin_ref, dm_ref, qd_ref, kd_ref, cd_ref,
                gain_ref, o_ref, s_ref, *, chunk, n_chunks):
    @pl.when(pl.program_id(2) == 0)
    def _():
        s_ref[...] = jnp.zeros_like(s_ref)

    dm = dm_ref[...]
    qd = qd_ref[...]
    kd = kd_ref[...]
    cd = cd_ref[...]
    gain = gain_ref[...]
    for c in range(n_chunks):
        rows = slice(c * chunk, (c + 1) * chunk)
        cos = cos_ref[rows, :]
        sin = sin_ref[rows, :]
        q = _rotate(q_ref[rows, :], cos, sin)
        k = _rotate(k_ref[rows, :], cos, sin) * (DK_R ** -0.5)
        v = v_ref[rows, :].astype(BF16)
        s = s_ref[...]
        scores = _dot_nt(q, k) * dm
        o = _dot(scores, v) + _dot(q * qd, s)
        s_ref[...] = cd * s + _dot_tn(k * kd, v)
        o_ref[rows, :] = _head_norm_gate(o, gain, g_ref[rows, :])


def _retention(z3, cos, sin, gain):
    b, t, _ = z3.shape
    c = RET_CHUNK
    tt = RET_TILE
    lg = np.log(1.0 - 2.0 ** (-5.0 - np.arange(H_R, dtype=np.float64)))
    idx = np.arange(c, dtype=np.float64)
    rel = idx[:, None] - idx[None, :]
    dm = np.where(rel >= 0, np.exp(lg[:, None, None] * np.maximum(rel, 0.0)), 0.0)
    qd = np.broadcast_to(np.exp(lg[:, None] * (idx + 1.0))[:, :, None], (H_R, c, DK_R))
    kd = np.broadcast_to(np.exp(lg[:, None] * (c - 1.0 - idx))[:, :, None], (H_R, c, DK_R))
    cd = np.broadcast_to(np.exp(lg * c)[:, None, None], (H_R, 1, DV_R))
    dm, qd, kd, cd = (jnp.asarray(a, F32) for a in (dm, qd, kd, cd))
    kq = QK_R // DK_R
    head_tab = lambda rows, cols: pl.BlockSpec((None, rows, cols), lambda bi, h, ti: (h, 0, 0))
    return pl.pallas_call(
        functools.partial(_ret_kernel, chunk=c, n_chunks=tt // c),
        grid=(b, H_R, t // tt),
        in_specs=[
            pl.BlockSpec((None, tt, DK_R), lambda bi, h, ti: (bi, ti, h)),
            pl.BlockSpec((None, tt, DK_R), lambda bi, h, ti: (bi, ti, kq + h)),
            pl.BlockSpec((None, tt, DV_R), lambda bi, h, ti: (bi, ti, 2 * QK_R // DV_R + h)),
            pl.BlockSpec((None, tt, DV_R), lambda bi, h, ti: (bi, ti, (2 * QK_R + V_R) // DV_R + h)),
            pl.BlockSpec((tt, DK_R), lambda bi, h, ti: (ti, 0)),
            pl.BlockSpec((tt, DK_R), lambda bi, h, ti: (ti, 0)),
            head_tab(c, c), head_tab(c, DK_R), head_tab(c, DK_R), head_tab(1, DV_R),
            pl.BlockSpec((1, DV_R), lambda bi, h, ti: (0, h)),
        ],
        out_specs=[
            pl.BlockSpec((None, tt, DV_R), lambda bi, h, ti: (bi, ti, h)),
            pl.BlockSpec((None, None, DK_R, DV_R), lambda bi, h, ti: (bi, h, 0, 0)),
        ],
        out_shape=[
            jax.ShapeDtypeStruct((b, t, V_R), F32),
            jax.ShapeDtypeStruct((b, H_R, DK_R, DV_R), F32),
        ],
        compiler_params=_params(("parallel", "parallel", "arbitrary")),
        name="retention",
    )(z3, z3, z3, z3, cos, sin, dm, qd, kd, cd, gain)


def _lower_bound(lb_ref):
    a = lb_ref[...]
    a0, a1 = a[0:1, :], a[1:2, :]
    m = jnp.maximum(a0, a1)
    e0 = jnp.exp(a0 - m)
    e1 = jnp.exp(a1 - m)
    return e0 / (e0 + e1)


def _forget_gate(gf, lb):
    sg = jax.nn.sigmoid(gf)
    f = lb + (1.0 - lb) * sg
    return jnp.log(f), (1.0 - lb) * (1.0 - sg), f


def _cumsum_rows(tri, x):
    hi = x.astype(BF16)
    rest = x - hi.astype(F32)
    mid = rest.astype(BF16)
    lo = (rest - mid.astype(F32)).astype(BF16)
    y = jnp.dot(tri, jnp.concatenate([hi, mid, lo], axis=1), preferred_element_type=F32)
    n = x.shape[1]
    return y[:, :n] + y[:, n:2 * n] + y[:, 2 * n:]


def _hgrn_kernel(q_ref, f_ref, i_ref, g_ref, lb_ref, gain_ref, o_ref, s_ref,
                 st_ref, q_sc, b_sc, c_sc, *, chunk, sub, n_chunks):
    assert sub == 8 and chunk % (2 * sub) == 0
    chunks = [slice(c * chunk, (c + 1) * chunk) for c in range(n_chunks)]

    @pl.when(pl.program_id(2) == 0)
    def _():
        st_ref[...] = jnp.zeros_like(st_ref)

    lb = _lower_bound(lb_ref)
    key_scale = 1.0 - lb
    gain = gain_ref[...]
    r64 = lax.broadcasted_iota(jnp.int32, (chunk, chunk), 0)
    c64 = lax.broadcasted_iota(jnp.int32, (chunk, chunk), 1)
    tri = jnp.where(r64 >= c64, 1.0, 0.0).astype(BF16)
    causal_lane = jnp.where(r64[:sub] >= c64[:sub] % sub, c64[:sub], -1)
    zrows = lambda n: jnp.zeros((n, DK_G), F32)

    sg = _sigmoid(f_ref[...])
    log_f = jnp.log2(lb + key_scale * sg)
    log_k = jnp.log2(key_scale * (1.0 - sg))
    for rows in chunks:
        b_sc[rows, :] = _cumsum_rows(tri, log_f[rows])
    c_sc[...] = b_sc[...] - log_k
    q_sc[...] = _silu(q_ref[...])

    def diagonal(r0):
        blocks = []
        for j in range(chunk // sub):
            o = r0 + j * sub
            q, b = q_sc[o:o + sub, :], b_sc[o:o + sub, :]
            acc = jnp.zeros((sub, chunk), F32)
            for s in range(sub):
                p = jnp.exp2(b - c_sc[o + s:o + s + 1, :]) * q
                acc = jnp.where(causal_lane == j * sub + s, jnp.sum(p, axis=-1, keepdims=True), acc)
            blocks.append(acc)
        return jnp.concatenate(blocks, axis=0)

    def below(r0, w):
        qs, ks = [], []
        for mid in range(r0 + w, r0 + chunk, 2 * w):
            ref_b = b_sc[mid - 1:mid, :]
            qs += [zrows(w), q_sc[mid:mid + w, :] * jnp.exp2(b_sc[mid:mid + w, :] - ref_b)]
            ks += [jnp.exp2(ref_b - c_sc[mid - w:mid, :]), zrows(w)]
        scores = _dot_nt(jnp.concatenate(qs, axis=0), jnp.concatenate(ks, axis=0))
        return jnp.where(c64 // w == r64 // w - 1, scores, 0.0)

    st = st_ref[...]
    for rows in chunks:
        a = diagonal(rows.start)
        w = chunk // 2
        while w >= sub:
            a = a + below(rows.start, w)
            w //= 2
        v = i_ref[rows, :].astype(BF16)
        b_last = b_sc[rows.stop - 1:rows.stop, :]
        o = _dot(a, v) + _dot_nt(q_sc[rows, :] * jnp.exp2(b_sc[rows, :]), st)
        st = st * jnp.exp2(b_last) + _dot_tn(v, jnp.exp2(b_last - c_sc[rows, :]))
        o_ref[rows, :] = _head_norm_gate(o, gain, g_ref[rows, :])
    st_ref[...] = st

    @pl.when(pl.program_id(2) == pl.num_programs(2) - 1)
    def _():
        s_ref[...] = st.T


def _hgrn(z3, hg_lb, gain):
    b, t, _ = z3.shape
    c, sub, tt = HG_CHUNK, HG_SUB, HG_TILE
    off = (2 * QK_R + 2 * V_R) // DK_G
    hb = F_G // DK_G
    col_block = lambda k: pl.BlockSpec((None, tt, DK_G), lambda bi, h, ti: (bi, ti, off + k * hb + h))
    return pl.pallas_call(
        functools.partial(_hgrn_kernel, chunk=c, sub=sub, n_chunks=tt // c),
        grid=(b, H_G, t // tt),
        in_specs=[
            col_block(0), col_block(1), col_block(2), col_block(3),
            pl.BlockSpec((2, DK_G), lambda bi, h, ti: (0, h)),
            pl.BlockSpec((1, DV_G), lambda bi, h, ti: (0, h)),
        ],
        out_specs=[
            pl.BlockSpec((None, tt, DV_G), lambda bi, h, ti: (bi, ti, h)),
            pl.BlockSpec((None, None, DK_G, DV_G), lambda bi, h, ti: (bi, h, 0, 0)),
        ],
        out_shape=[
            jax.ShapeDtypeStruct((b, t, V_G), F32),
            jax.ShapeDtypeStruct((b, H_G, DK_G, DV_G), F32),
        ],
        scratch_shapes=[
            pltpu.VMEM((DV_G, DK_G), F32),
            pltpu.VMEM((tt, DK_G), F32),
            pltpu.VMEM((tt, DK_G), F32),
            pltpu.VMEM((tt, DK_G), F32),
        ],
        compiler_params=_params(("parallel", "parallel", "arbitrary")),
        name="hgrn",
    )(z3, z3, z3, z3, hg_lb, gain)


def _step_kernel(z_ref, cos_ref, sin_ref, lb_ref, rgain_ref, ggain_ref, sr0_ref, sg0_ref,
                 or_ref, og_ref, sr_ref, sg_ref, *, bb):
    first = lambda a: jnp.where(lax.broadcasted_iota(jnp.int32, a.shape, 0) == 0, a, 0.0)
    cos = cos_ref[...]
    sin = sin_ref[...]
    lb_all = _lower_bound(lb_ref)
    for i in range(bb):
        z8 = lambda lo, n: jnp.broadcast_to(z_ref[i, :, lo:lo + n], (8, n))
        for h in range(H_R):
            q = _rotate(z8(h * DK_R, DK_R), cos, sin)
            k = _rotate(z8(QK_R + h * DK_R, DK_R), cos, sin) * (DK_R ** -0.5)
            v = z8(2 * QK_R + h * DV_R, DV_R)
            gate = z_ref[i, :, 2 * QK_R + V_R + h * DV_R:2 * QK_R + V_R + (h + 1) * DV_R]
            gamma = 1.0 - 2.0 ** (-5.0 - h)
            s_new = gamma * sr0_ref[i, h] + _dot_tn(first(k), v)
            sr_ref[i, h] = s_new
            o = _dot(q, s_new)[0:1, :]
            or_ref[i, :, h * DV_R:(h + 1) * DV_R] = _head_norm_gate(
                o, rgain_ref[:, h * DV_R:(h + 1) * DV_R], gate)
        g0 = 2 * QK_R + 2 * V_R
        for h in range(H_G):
            lanes = slice(h * DK_G, (h + 1) * DK_G)
            q = _silu(z8(g0 + h * DK_G, DK_G))
            _, k, f = _forget_gate(z8(g0 + F_G + h * DK_G, DK_G), lb_all[:, lanes])
            v = z8(g0 + 2 * F_G + h * DV_G, DV_G)
            gate = z_ref[i, :, g0 + 2 * F_G + V_G + h * DV_G:g0 + 2 * F_G + V_G + (h + 1) * DV_G]
            st_new = sg0_ref[i, h].T * f[0:1, :] + _dot_tn(first(v), k)
            sg_ref[i, h] = st_new.T
            o = _dot_nt(q, st_new)[0:1, :]
            og_ref[i, :, lanes] = _head_norm_gate(o, ggain_ref[:, lanes], gate)


def _step(z, cos, sin, hg_lb, rgain, ggain, sr0, sg0, bb):
    n, d_in = z.shape
    z3 = z.reshape(n, 1, d_in)
    row = lambda w: pl.BlockSpec((bb, 1, w), lambda i: (i, 0, 0))
    const = lambda a: pl.BlockSpec(a.shape, lambda i: (0,) * a.ndim)
    sr_spec = pl.BlockSpec((bb, H_R, DK_R, DV_R), lambda i: (i, 0, 0, 0))
    sg_spec = pl.BlockSpec((bb, H_G, DK_G, DV_G), lambda i: (i, 0, 0, 0))
    return pl.pallas_call(
        functools.partial(_step_kernel, bb=bb),
        grid=(n // bb,),
        in_specs=[row(d_in), const(cos), const(sin), const(hg_lb), const(rgain), const(ggain),
                  sr_spec, sg_spec],
        out_specs=[row(V_R), row(V_G), sr_spec, sg_spec],
        out_shape=[
            jax.ShapeDtypeStruct((n, 1, V_R), F32),
            jax.ShapeDtypeStruct((n, 1, V_G), F32),
            jax.ShapeDtypeStruct(sr0.shape, F32),
            jax.ShapeDtypeStruct(sg0.shape, F32),
        ],
        compiler_params=_params(("parallel",)),
        name="step",
    )(z3, cos, sin, hg_lb, rgain, ggain, sr0, sg0)


def _post_kernel(x_ref, gr_ref, gg_ref, ar_ref, ag_ref, p_ref, wur_ref, wug_ref, wo_ref,
                 nm_ref, w1_ref, w2_ref, np_ref, wg_ref, wp_ref, nf_ref, y_ref):
    u_r = jnp.dot(gr_ref[...].astype(BF16), wur_ref[...], preferred_element_type=F32)
    u_g = jnp.dot(gg_ref[...].astype(BF16), wug_ref[...], preferred_element_type=F32)
    m = jax.nn.sigmoid(ar_ref[...]) * u_r + jax.nn.sigmoid(ag_ref[...]) * u_g
    r = x_ref[...] + jnp.dot(m.astype(BF16), wo_ref[...], preferred_element_type=F32)
    hm = _rms(r, nm_ref[...]).astype(BF16)
    a = jnp.maximum(jnp.dot(hm, w1_ref[...], preferred_element_type=F32), 0.0)
    r = r + jnp.dot((a * a).astype(BF16), w2_ref[...], preferred_element_type=F32)
    hp = _rms(r, np_ref[...]).astype(BF16)
    gate = jax.nn.sigmoid(jnp.dot(hp, wg_ref[...], preferred_element_type=F32))
    r = r + gate * jnp.dot(p_ref[...].astype(BF16), wp_ref[...], preferred_element_type=F32)
    y_ref[...] = _rms(r, nf_ref[...])


def _post(x, g_r, g_g, z, p, wur, wug, wo, nm, w1, w2, npl, wg, wp, nf, tm):
    n, d = x.shape
    a_blk = (2 * QK_R + 2 * V_R + 2 * F_G + 2 * V_G) // d
    rows = lambda w, j=0: pl.BlockSpec((tm, w), lambda i: (i, j))
    const = lambda a: pl.BlockSpec(a.shape, lambda i: (0, 0), pipeline_mode=pl.Buffered(1))
    return pl.pallas_call(
        _post_kernel,
        grid=(n // tm,),
        in_specs=[rows(d), rows(d), rows(d), rows(d, a_blk), rows(d, a_blk + 1), rows(p.shape[1]),
                  const(wur), const(wug), const(wo), const(nm), const(w1), const(w2),
                  const(npl), const(wg), const(wp), const(nf)],
        out_specs=rows(d),
        out_shape=jax.ShapeDtypeStruct((n, d), F32),
        compiler_params=_params(("parallel",)),
        name="post",
    )(x, g_r, g_g, z, z, p, wur, wug, wo, nm, w1, w2, npl, wg, wp, nf)


def _rope_tables(pos):
    half = DK_R // 2
    inv = ROPE_BASE ** (-jnp.arange(half, dtype=F32) / half)
    ang = pos[:, None] * inv[None, :]
    cos, sin = jnp.cos(ang), jnp.sin(ang)
    return jnp.concatenate([cos, cos], axis=-1), jnp.concatenate([-sin, sin], axis=-1)


def kernel(x_prompt, x_sample, state_ret, state_hgrn, p_prompt, p_sample, norm_mix_g, w_in,
           ret_norm_g, hg_norm_g, hg_lb, w_up_ret, w_up_hg, w_o, norm_mlp_g, w_ff1, w_ff2,
           norm_ple_g, w_ple_gate, w_ple_proj, norm_final_g):
    b, t, d = x_prompt.shape
    nb, td, _ = x_sample.shape
    assert w_in.shape[0] == 1 and td == 1 and hg_lb.shape[0] == 2
    bf = lambda w: w[0].astype(BF16)
    row = lambda g: g.reshape(1, -1)
    w_in_b = bf(w_in)
    post_w = (bf(w_up_ret), bf(w_up_hg), bf(w_o), row(norm_mlp_g[0]), bf(w_ff1), bf(w_ff2),
              row(norm_ple_g[0]), bf(w_ple_gate), bf(w_ple_proj), row(norm_final_g))
    g_mix = row(norm_mix_g[0])
    rgain = row(ret_norm_g[0])
    ggain = row(hg_norm_g[0])

    xp = x_prompt.reshape(b * t, d)
    zp = _in_proj(xp, g_mix, w_in_b, tm=1024, tn=1024)
    zp3 = zp.reshape(b, t, -1)
    cos_p, sin_p = _rope_tables(jnp.arange(t, dtype=F32))
    gr_p, ret_p = _retention(zp3, cos_p, sin_p, rgain)
    gg_p, hg_p = _hgrn(zp3, hg_lb, ggain)
    y_p = _post(xp, gr_p.reshape(b * t, -1), gg_p.reshape(b * t, -1), zp,
                p_prompt[0].reshape(b * t, -1), *post_w, tm=256)

    xs = x_sample.reshape(nb, d)
    zs = _in_proj(xs, g_mix, w_in_b, tm=nb, tn=1024)
    cos_s, sin_s = _rope_tables(PAST_LEN + jnp.arange(td, dtype=F32))
    gr_s, gg_s, ret_s, hg_s = _step(zs, cos_s, sin_s, hg_lb, rgain, ggain,
                                    state_ret[0], state_hgrn[0], bb=2)
    y_s = _post(xs, gr_s.reshape(nb, -1), gg_s.reshape(nb, -1), zs,
                p_sample[0].reshape(nb, -1), *post_w, tm=nb)

    return (y_p.reshape(b, t, d), y_s.reshape(nb, td, d), ret_p[None], hg_p[None],
            ret_s[None], hg_s[None])
```

```python
import functools

import jax
import jax.numpy as jnp
import numpy as np
from jax import lax
from jax.experimental import pallas as pl
from jax.experimental.pallas import tpu as pltpu

F32 = jnp.float32
BF16 = jnp.bfloat16

PAST_LEN = 16384
ROPE_BASE = 10000.0
EPS = 1e-6

H_R, DK_R, DV_R = 4, 128, 256
H_G, DK_G, DV_G = 8, 128, 128
QK_R = H_R * DK_R
V_R = H_R * DV_R
F_G = H_G * DK_G
V_G = H_G * DV_G

V7X_VMEM_BYTES = 64 * 1024 * 1024
VMEM_LIMIT = V7X_VMEM_BYTES - 8 * 1024 * 1024

RET_CHUNK = 128
RET_TILE = 512
HG_CHUNK = 64
HG_SUB = 8
HG_GROUP = 4
HG_LAG = 2

NT_DIMS = (((1,), (1,)), ((), ()))
TN_DIMS = (((0,), (0,)), ((), ()))


def _dot(a, b):
    return jnp.dot(a.astype(BF16), b.astype(BF16), preferred_element_type=F32)


def _dot_nt(a, b):
    return lax.dot_general(a.astype(BF16), b.astype(BF16), NT_DIMS, preferred_element_type=F32)


def _dot_tn(a, b):
    return lax.dot_general(a.astype(BF16), b.astype(BF16), TN_DIMS, preferred_element_type=F32)


def _rms(x, g):
    return x * lax.rsqrt(jnp.mean(x * x, axis=-1, keepdims=True) + EPS) * g


def _sigmoid(x):
    return 0.5 * jnp.tanh(0.5 * x) + 0.5


def _silu(x):
    half = 0.5 * x
    return half + half * jnp.tanh(half)


def _params(sem):
    return pltpu.CompilerParams(dimension_semantics=sem, vmem_limit_bytes=VMEM_LIMIT)


def _norm_kernel(x_ref, g_ref, h_ref):
    h_ref[...] = _rms(x_ref[...], g_ref[...]).astype(BF16)


def _norm(x, g, tm):
    n, d = x.shape
    return pl.pallas_call(
        _norm_kernel,
        grid=(n // tm,),
        in_specs=[pl.BlockSpec((tm, d), lambda i: (i, 0)), pl.BlockSpec((1, d), lambda i: (0, 0))],
        out_specs=pl.BlockSpec((tm, d), lambda i: (i, 0)),
        out_shape=jax.ShapeDtypeStruct((n, d), BF16),
        compiler_params=_params(("parallel",)),
        name="norm",
    )(x, g)


def _proj_kernel(h_ref, w_ref, z_ref):
    z_ref[...] = jnp.dot(h_ref[...], w_ref[...], preferred_element_type=F32)


def _proj(h, w, tm, tn):
    n, d = h.shape
    d_out = w.shape[1]
    return pl.pallas_call(
        _proj_kernel,
        grid=(n // tm, d_out // tn),
        in_specs=[
            pl.BlockSpec((tm, d), lambda i, j: (i, 0)),
            pl.BlockSpec((d, tn), lambda i, j: (0, j)),
        ],
        out_specs=pl.BlockSpec((tm, tn), lambda i, j: (i, j)),
        out_shape=jax.ShapeDtypeStruct((n, d_out), F32),
        compiler_params=_params(("parallel", "arbitrary")),
        name="proj",
    )(h, w)


def _rotate(x, cos, sin_signed):
    return x * cos + pltpu.roll(x, DK_R // 2, 1) * sin_signed


def _head_norm_gate(o, gain, gate):
    return o * lax.rsqrt(jnp.mean(o * o, axis=-1, keepdims=True) + EPS) * gain * _silu(gate)


def _ret_kernel(q_ref, k_ref, v_ref, g_ref, cos_ref, sin_ref, dm_ref, qd_ref, kd_ref, cd_ref,
                gain_ref, o_ref, s_ref, *, chunk, n_chunks):
    @pl.when(pl.program_id(2) == 0)
    def _():
        s_ref[...] = jnp.zeros_like(s_ref)

    dm = dm_ref[...]
    qd = qd_ref[...]
    kd = kd_ref[...]
    cd = cd_ref[...]
    gain = gain_ref[...]
    for c in range(n_chunks):
        rows = slice(c * chunk, (c + 1) * chunk)
        cos = cos_ref[rows, :]
        sin = sin_ref[rows, :]
        q = _rotate(q_ref[rows, :], cos, sin)
        k = _rotate(k_ref[rows, :], cos, sin) * (DK_R ** -0.5)
        v = v_ref[rows, :].astype(BF16)
        s = s_ref[...]
        scores = _dot_nt(q, k) * dm
        o = _dot(scores, v) + _dot(q * qd, s)
        s_ref[...] = cd * s + _dot_tn(k * kd, v)
        o_ref[rows, :] = _head_norm_gate(o, gain, g_ref[rows, :])


def _retention(z3, cos, sin, gain):
    b, t, _ = z3.shape
    c = RET_CHUNK
    tt = RET_TILE
    lg = np.log(1.0 - 2.0 ** (-5.0 - np.arange(H_R, dtype=np.float64)))
    idx = np.arange(c, dtype=np.float64)
    rel = idx[:, None] - idx[None, :]
    dm = np.where(rel >= 0, np.exp(lg[:, None, None] * np.maximum(rel, 0.0)), 0.0)
    qd = np.broadcast_to(np.exp(lg[:, None] * (idx + 1.0))[:, :, None], (H_R, c, DK_R))
    kd = np.broadcast_to(np.exp(lg[:, None] * (c - 1.0 - idx))[:, :, None], (H_R, c, DK_R))
    cd = np.broadcast_to(np.exp(lg * c)[:, None, None], (H_R, 1, DV_R))
    dm, qd, kd, cd = (jnp.asarray(a, F32) for a in (dm, qd, kd, cd))
    kq = QK_R // DK_R
    head_tab = lambda rows, cols: pl.BlockSpec((None, rows, cols), lambda bi, h, ti: (h, 0, 0))
    return pl.pallas_call(
        functools.partial(_ret_kernel, chunk=c, n_chunks=tt // c),
        grid=(b, H_R, t // tt),
        in_specs=[
            pl.BlockSpec((None, tt, DK_R), lambda bi, h, ti: (bi, ti, h)),
            pl.BlockSpec((None, tt, DK_R), lambda bi, h, ti: (bi, ti, kq + h)),
            pl.BlockSpec((None, tt, DV_R), lambda bi, h, ti: (bi, ti, 2 * QK_R // DV_R + h)),
            pl.BlockSpec((None, tt, DV_R), lambda bi, h, ti: (bi, ti, (2 * QK_R + V_R) // DV_R + h)),
            pl.BlockSpec((tt, DK_R), lambda bi, h, ti: (ti, 0)),
            pl.BlockSpec((tt, DK_R), lambda bi, h, ti: (ti, 0)),
            head_tab(c, c), head_tab(c, DK_R), head_tab(c, DK_R), head_tab(1, DV_R),
            pl.BlockSpec((1, DV_R), lambda bi, h, ti: (0, h)),
        ],
        out_specs=[
            pl.BlockSpec((None, tt, DV_R), lambda bi, h, ti: (bi, ti, h)),
            pl.BlockSpec((None, None, DK_R, DV_R), lambda bi, h, ti: (bi, h, 0, 0)),
        ],
        out_shape=[
            jax.ShapeDtypeStruct((b, t, V_R), F32),
            jax.ShapeDtypeStruct((b, H_R, DK_R, DV_R), F32),
        ],
        compiler_params=_params(("parallel", "parallel", "arbitrary")),
        name="retention",
    )(z3, z3, z3, z3, cos, sin, dm, qd, kd, cd, gain)


def _lower_bound(lb_ref):
    a = lb_ref[...]
    a0, a1 = a[0:1, :], a[1:2, :]
    m = jnp.maximum(a0, a1)
    e0 = jnp.exp(a0 - m)
    e1 = jnp.exp(a1 - m)
    return e0 / (e0 + e1)


def _forget_gate(gf, lb):
    sg = jax.nn.sigmoid(gf)
    f = lb + (1.0 - lb) * sg
    return jnp.log(f), (1.0 - lb) * (1.0 - sg), f


def _cumsum_rows(tri, x):
    hi = x.astype(BF16)
    rest = x - hi.astype(F32)
    mid = rest.astype(BF16)
    lo = (rest - mid.astype(F32)).astype(BF16)
    y = jnp.dot(tri, jnp.concatenate([hi, mid, lo], axis=1), preferred_element_type=F32)
    n = x.shape[1]
    return y[:, :n] + y[:, n:2 * n] + y[:, 2 * n:]


def _hgrn_kernel(h_ref, w_ref, lb_ref, gain_ref, o_ref, s_ref, z_sc, q_sc, b_sc, c_sc,
                 *, chunk, sub, group, lag):
    assert sub == 8 and chunk % (2 * sub) == 0
    g_rows = group * chunk
    n_chunks = h_ref.shape[0] // chunk
    assert n_chunks % group == 0 and lag >= 1
    cols = lambda i: slice(i * DK_G, (i + 1) * DK_G)

    lb = _lower_bound(lb_ref)
    half_scale = 0.5 * (1.0 - lb)
    f_mid = lb + half_scale
    gain = gain_ref[...]
    r64 = lax.broadcasted_iota(jnp.int32, (chunk, chunk), 0)
    c64 = lax.broadcasted_iota(jnp.int32, (chunk, chunk), 1)
    tri = jnp.where(r64 >= c64, 1.0, 0.0).astype(BF16)
    causal_lane = jnp.where(r64[:sub] >= c64[:sub] % sub, c64[:sub], -1)
    zrows = lambda n: jnp.zeros((n, DK_G), F32)

    def project(r0):
        rows = slice(r0, r0 + g_rows)
        z_sc[rows, :] = jnp.dot(h_ref[rows, :], w_ref[...], preferred_element_type=F32)

    def gates(r0):
        rows = slice(r0, r0 + chunk)
        th_scaled = half_scale * jnp.tanh(0.5 * z_sc[rows, cols(1)])
        b_sc[rows, :] = _cumsum_rows(tri, jnp.log2(f_mid + th_scaled))
        c_sc[rows, :] = b_sc[rows, :] - jnp.log2(half_scale - th_scaled)
        q_sc[rows, :] = _silu(z_sc[rows, cols(0)])

    def diagonal(r0):
        blocks = []
        for j in range(chunk // sub):
            o = r0 + j * sub
            q, b = q_sc[o:o + sub, :], b_sc[o:o + sub, :]
            acc = jnp.zeros((sub, chunk), F32)
            for s in range(sub):
                p = jnp.exp2(b - c_sc[o + s:o + s + 1, :]) * q
                acc = jnp.where(causal_lane == j * sub + s, jnp.sum(p, axis=-1, keepdims=True), acc)
            blocks.append(acc)
        return jnp.concatenate(blocks, axis=0)

    def below(r0, w):
        qs, ks = [], []
        for mid in range(r0 + w, r0 + chunk, 2 * w):
            ref_b = b_sc[mid - 1:mid, :]
            qs += [zrows(w), q_sc[mid:mid + w, :] * jnp.exp2(b_sc[mid:mid + w, :] - ref_b)]
            ks += [jnp.exp2(ref_b - c_sc[mid - w:mid, :]), zrows(w)]
        scores = _dot_nt(jnp.concatenate(qs, axis=0), jnp.concatenate(ks, axis=0))
        return jnp.where(c64 // w == r64 // w - 1, scores, 0.0)

    def scores(r0):
        a = diagonal(r0)
        w = chunk // 2
        while w >= sub:
            a = a + below(r0, w)
            w //= 2
        return a

    def outputs(r0, a, st):
        rows = slice(r0, r0 + chunk)
        v = z_sc[rows, cols(2)].astype(BF16)
        b_last = b_sc[rows.stop - 1:rows.stop, :]
        o = _dot(a, v) + _dot_nt(q_sc[rows, :] * jnp.exp2(b_sc[rows, :]), st)
        o_ref[rows, :] = _head_norm_gate(o, gain, z_sc[rows, cols(3)])
        return st * jnp.exp2(b_last) + _dot_tn(v, jnp.exp2(b_last - c_sc[rows, :]))

    st = jnp.zeros((DV_G, DK_G), F32)
    pending = {}
    for step in range(-lag, n_chunks + 2 * lag):
        ahead = step + lag
        if 0 <= ahead < n_chunks and ahead % group == 0:
            project(ahead * chunk)
        if 0 <= step < n_chunks:
            gates(step * chunk)
        if 0 <= step - lag < n_chunks:
            pending[step - lag] = scores((step - lag) * chunk)
        if 0 <= step - 2 * lag < n_chunks:
            st = outputs((step - 2 * lag) * chunk, pending.pop(step - 2 * lag), st)
    s_ref[...] = st.T


def _hgrn(h3, w_hg, hg_lb, gain):
    b, t, d = h3.shape
    c, sub = HG_CHUNK, HG_SUB
    return pl.pallas_call(
        functools.partial(_hgrn_kernel, chunk=c, sub=sub, group=HG_GROUP, lag=HG_LAG),
        grid=(b, H_G),
        in_specs=[
            pl.BlockSpec((None, t, d), lambda bi, h: (bi, 0, 0)),
            pl.BlockSpec((d, 4 * DK_G), lambda bi, h: (0, h)),
            pl.BlockSpec((2, DK_G), lambda bi, h: (0, h)),
            pl.BlockSpec((1, DV_G), lambda bi, h: (0, h)),
        ],
        out_specs=[
            pl.BlockSpec((None, t, DV_G), lambda bi, h: (bi, 0, h)),
            pl.BlockSpec((None, None, DK_G, DV_G), lambda bi, h: (bi, h, 0, 0)),
        ],
        out_shape=[
            jax.ShapeDtypeStruct((b, t, V_G), F32),
            jax.ShapeDtypeStruct((b, H_G, DK_G, DV_G), F32),
        ],
        scratch_shapes=[
            pltpu.VMEM((t, 4 * DK_G), F32),
            pltpu.VMEM((t, DK_G), F32),
            pltpu.VMEM((t, DK_G), F32),
            pltpu.VMEM((t, DK_G), F32),
        ],
        compiler_params=_params(("parallel", "arbitrary")),
        name="hgrn",
    )(h3, w_hg, hg_lb, gain)


def _step_kernel(z_ref, cos_ref, sin_ref, lb_ref, rgain_ref, ggain_ref, sr0_ref, sg0_ref,
                 or_ref, og_ref, sr_ref, sg_ref, *, bb):
    first = lambda a: jnp.where(lax.broadcasted_iota(jnp.int32, a.shape, 0) == 0, a, 0.0)
    cos = cos_ref[...]
    sin = sin_ref[...]
    lb_all = _lower_bound(lb_ref)
    for i in range(bb):
        z8 = lambda lo, n: jnp.broadcast_to(z_ref[i, :, lo:lo + n], (8, n))
        for h in range(H_R):
            q = _rotate(z8(h * DK_R, DK_R), cos, sin)
            k = _rotate(z8(QK_R + h * DK_R, DK_R), cos, sin) * (DK_R ** -0.5)
            v = z8(2 * QK_R + h * DV_R, DV_R)
            gate = z_ref[i, :, 2 * QK_R + V_R + h * DV_R:2 * QK_R + V_R + (h + 1) * DV_R]
            gamma = 1.0 - 2.0 ** (-5.0 - h)
            s_new = gamma * sr0_ref[i, h] + _dot_tn(first(k), v)
            sr_ref[i, h] = s_new
            o = _dot(q, s_new)[0:1, :]
            or_ref[i, :, h * DV_R:(h + 1) * DV_R] = _head_norm_gate(
                o, rgain_ref[:, h * DV_R:(h + 1) * DV_R], gate)
        g0 = 2 * QK_R + 2 * V_R
        for h in range(H_G):
            lanes = slice(h * DK_G, (h + 1) * DK_G)
            q = _silu(z8(g0 + h * DK_G, DK_G))
            _, k, f = _forget_gate(z8(g0 + F_G + h * DK_G, DK_G), lb_all[:, lanes])
            v = z8(g0 + 2 * F_G + h * DV_G, DV_G)
            gate = z_ref[i, :, g0 + 2 * F_G + V_G + h * DV_G:g0 + 2 * F_G + V_G + (h + 1) * DV_G]
            st_new = sg0_ref[i, h].T * f[0:1, :] + _dot_tn(first(v), k)
            sg_ref[i, h] = st_new.T
            o = _dot_nt(q, st_new)[0:1, :]
            og_ref[i, :, lanes] = _head_norm_gate(o, ggain_ref[:, lanes], gate)


def _step(z, cos, sin, hg_lb, rgain, ggain, sr0, sg0, bb):
    n, d_in = z.shape
    z3 = z.reshape(n, 1, d_in)
    row = lambda w: pl.BlockSpec((bb, 1, w), lambda i: (i, 0, 0))
    const = lambda a: pl.BlockSpec(a.shape, lambda i: (0,) * a.ndim)
    sr_spec = pl.BlockSpec((bb, H_R, DK_R, DV_R), lambda i: (i, 0, 0, 0))
    sg_spec = pl.BlockSpec((bb, H_G, DK_G, DV_G), lambda i: (i, 0, 0, 0))
    return pl.pallas_call(
        functools.partial(_step_kernel, bb=bb),
        grid=(n // bb,),
        in_specs=[row(d_in), const(cos), const(sin), const(hg_lb), const(rgain), const(ggain),
                  sr_spec, sg_spec],
        out_specs=[row(V_R), row(V_G), sr_spec, sg_spec],
        out_shape=[
            jax.ShapeDtypeStruct((n, 1, V_R), F32),
            jax.ShapeDtypeStruct((n, 1, V_G), F32),
            jax.ShapeDtypeStruct(sr0.shape, F32),
            jax.ShapeDtypeStruct(sg0.shape, F32),
        ],
        compiler_params=_params(("parallel",)),
        name="step",
    )(z3, cos, sin, hg_lb, rgain, ggain, sr0, sg0)


def _post_kernel(x_ref, h_ref, gr_ref, gg_ref, p_ref, wa_ref, wur_ref, wug_ref, wo_ref,
                 nm_ref, w1_ref, w2_ref, np_ref, wg_ref, wp_ref, nf_ref, y_ref):
    d = x_ref.shape[1]
    merge = _sigmoid(jnp.dot(h_ref[...], wa_ref[...], preferred_element_type=F32))
    u_r = jnp.dot(gr_ref[...].astype(BF16), wur_ref[...], preferred_element_type=F32)
    u_g = jnp.dot(gg_ref[...].astype(BF16), wug_ref[...], preferred_element_type=F32)
    m = merge[:, :d] * u_r + merge[:, d:] * u_g
    r = x_ref[...] + jnp.dot(m.astype(BF16), wo_ref[...], preferred_element_type=F32)
    hm = _rms(r, nm_ref[...]).astype(BF16)
    a = jnp.maximum(jnp.dot(hm, w1_ref[...], preferred_element_type=F32), 0.0)
    r = r + jnp.dot((a * a).astype(BF16), w2_ref[...], preferred_element_type=F32)
    hp = _rms(r, np_ref[...]).astype(BF16)
    gate = _sigmoid(jnp.dot(hp, wg_ref[...], preferred_element_type=F32))
    r = r + gate * jnp.dot(p_ref[...].astype(BF16), wp_ref[...], preferred_element_type=F32)
    y_ref[...] = _rms(r, nf_ref[...])


def _post(x, h, g_r, g_g, p, wa, wur, wug, wo, nm, w1, w2, npl, wg, wp, nf, tm):
    n, d = x.shape
    rows = lambda w: pl.BlockSpec((tm, w), lambda i: (i, 0))
    const = lambda a: pl.BlockSpec(a.shape, lambda i: (0, 0), pipeline_mode=pl.Buffered(1))
    return pl.pallas_call(
        _post_kernel,
        grid=(n // tm,),
        in_specs=[rows(d), rows(d), rows(d), rows(d), rows(p.shape[1]),
                  const(wa), const(wur), const(wug), const(wo), const(nm), const(w1), const(w2),
                  const(npl), const(wg), const(wp), const(nf)],
        out_specs=rows(d),
        out_shape=jax.ShapeDtypeStruct((n, d), F32),
        compiler_params=_params(("parallel",)),
        name="post",
    )(x, h, g_r, g_g, p, wa, wur, wug, wo, nm, w1, w2, npl, wg, wp, nf)


def _rope_tables(pos):
    half = DK_R // 2
    inv = ROPE_BASE ** (-jnp.arange(half, dtype=F32) / half)
    ang = pos[:, None] * inv[None, :]
    cos, sin = jnp.cos(ang), jnp.sin(ang)
    return jnp.concatenate([cos, cos], axis=-1), jnp.concatenate([-sin, sin], axis=-1)


def kernel(x_prompt, x_sample, state_ret, state_hgrn, p_prompt, p_sample, norm_mix_g, w_in,
           ret_norm_g, hg_norm_g, hg_lb, w_up_ret, w_up_hg, w_o, norm_mlp_g, w_ff1, w_ff2,
           norm_ple_g, w_ple_gate, w_ple_proj, norm_final_g):
    b, t, d = x_prompt.shape
    nb, td, _ = x_sample.shape
    assert w_in.shape[0] == 1 and td == 1 and hg_lb.shape[0] == 2
    bf = lambda w: w[0].astype(BF16)
    row = lambda g: g.reshape(1, -1)
    w_in_b = bf(w_in)
    n_ret = 2 * QK_R + 2 * V_R
    n_hg = 2 * F_G + 2 * V_G
    w_ret = w_in_b[:, :n_ret]
    w_hg = w_in_b[:, n_ret:n_ret + n_hg].reshape(d, 4, H_G, DK_G).transpose(0, 2, 1, 3).reshape(d, n_hg)
    post_w = (w_in_b[:, n_ret + n_hg:], bf(w_up_ret), bf(w_up_hg), bf(w_o), row(norm_mlp_g[0]),
              bf(w_ff1), bf(w_ff2), row(norm_ple_g[0]), bf(w_ple_gate), bf(w_ple_proj), row(norm_final_g))
    g_mix = row(norm_mix_g[0])
    rgain = row(ret_norm_g[0])
    ggain = row(hg_norm_g[0])

    xp = x_prompt.reshape(b * t, d)
    hp = _norm(xp, g_mix, tm=1024)
    hp3 = hp.reshape(b, t, d)
    zr3 = _proj(hp, w_ret, tm=1024, tn=1024).reshape(b, t, n_ret)
    cos_p, sin_p = _rope_tables(jnp.arange(t, dtype=F32))
    gr_p, ret_p = _retention(zr3, cos_p, sin_p, rgain)
    gg_p, hg_p = _hgrn(hp3, w_hg, hg_lb, ggain)
    y_p = _post(xp, hp, gr_p.reshape(b * t, -1), gg_p.reshape(b * t, -1),
                p_prompt[0].reshape(b * t, -1), *post_w, tm=256)

    xs = x_sample.reshape(nb, d)
    hs = _norm(xs, g_mix, tm=nb)
    zs = _proj(hs, w_in_b, tm=nb, tn=1024)
    cos_s, sin_s = _rope_tables(PAST_LEN + jnp.arange(td, dtype=F32))
    gr_s, gg_s, ret_s, hg_s = _step(zs, cos_s, sin_s, hg_lb, rgain, ggain,
                                    state_ret[0], state_hgrn[0], bb=2)
    y_s = _post(xs, hs, gr_s.reshape(nb, -1), gg_s.reshape(nb, -1),
                p_sample[0].reshape(nb, -1), *post_w, tm=nb)

    return (y_p.reshape(b, t, d), y_s.reshape(nb, td, d), ret_p[None], hg_p[None],
            ret_s[None], hg_s[None])
```

```python
import functools

import jax
import jax.numpy as jnp
import numpy as np
from jax import lax
from jax.experimental import pallas as pl
from jax.experimental.pallas import tpu as pltpu

F32 = jnp.float32
BF16 = jnp.bfloat16

PAST_LEN = 16384
ROPE_BASE = 10000.0
EPS = 1e-6

H_R, DK_R, DV_R = 4, 128, 256
H_G, DK_G, DV_G = 8, 128, 128
QK_R = H_R * DK_R
V_R = H_R * DV_R
F_G = H_G * DK_G
V_G = H_G * DV_G

V7X_VMEM_BYTES = 64 * 1024 * 1024
VMEM_LIMIT = V7X_VMEM_BYTES - 8 * 1024 * 1024

RET_CHUNK = 128
RET_GROUP = 2
RET_LAG = 1
HG_CHUNK = 64
HG_SUB = 8
HG_GROUP = 4
HG_LAG = 2

NT_DIMS = (((1,), (1,)), ((), ()))
TN_DIMS = (((0,), (0,)), ((), ()))


def _dot(a, b):
    return jnp.dot(a.astype(BF16), b.astype(BF16), preferred_element_type=F32)


def _dot_nt(a, b):
    return lax.dot_general(a.astype(BF16), b.astype(BF16), NT_DIMS, preferred_element_type=F32)


def _dot_tn(a, b):
    return lax.dot_general(a.astype(BF16), b.astype(BF16), TN_DIMS, preferred_element_type=F32)


def _rms(x, g):
    return x * lax.rsqrt(jnp.mean(x * x, axis=-1, keepdims=True) + EPS) * g


def _sigmoid(x):
    return 0.5 * jnp.tanh(0.5 * x) + 0.5


def _silu(x):
    half = 0.5 * x
    return half + half * jnp.tanh(half)


def _params(sem):
    return pltpu.CompilerParams(dimension_semantics=sem, vmem_limit_bytes=VMEM_LIMIT)


def _norm_kernel(x_ref, g_ref, h_ref):
    h_ref[...] = _rms(x_ref[...], g_ref[...]).astype(BF16)


def _norm(x, g, tm):
    n, d = x.shape
    return pl.pallas_call(
        _norm_kernel,
        grid=(n // tm,),
        in_specs=[pl.BlockSpec((tm, d), lambda i: (i, 0)), pl.BlockSpec((1, d), lambda i: (0, 0))],
        out_specs=pl.BlockSpec((tm, d), lambda i: (i, 0)),
        out_shape=jax.ShapeDtypeStruct((n, d), BF16),
        compiler_params=_params(("parallel",)),
        name="norm",
    )(x, g)


def _proj_kernel(h_ref, w_ref, z_ref):
    z_ref[...] = jnp.dot(h_ref[...], w_ref[...], preferred_element_type=F32)


def _proj(h, w, tm, tn):
    n, d = h.shape
    d_out = w.shape[1]
    return pl.pallas_call(
        _proj_kernel,
        grid=(n // tm, d_out // tn),
        in_specs=[
            pl.BlockSpec((tm, d), lambda i, j: (i, 0)),
            pl.BlockSpec((d, tn), lambda i, j: (0, j)),
        ],
        out_specs=pl.BlockSpec((tm, tn), lambda i, j: (i, j)),
        out_shape=jax.ShapeDtypeStruct((n, d_out), F32),
        compiler_params=_params(("parallel", "arbitrary")),
        name="proj",
    )(h, w)


def _rotate(x, cos, sin_signed):
    return x * cos + pltpu.roll(x, DK_R // 2, 1) * sin_signed


def _head_norm_gate(o, gain, gate):
    return o * lax.rsqrt(jnp.mean(o * o, axis=-1, keepdims=True) + EPS) * gain * _silu(gate)


def _ret_kernel(h_ref, w_ref, cos_ref, sin_ref, dm_ref, qd_ref, kd_ref, cd_ref, gain_ref,
                o_ref, s_ref, z_sc, *, chunk, group, lag):
    n_chunks = h_ref.shape[0] // chunk
    assert n_chunks % group == 0 and lag >= 1
    g_rows = group * chunk
    q_cols = slice(0, DK_R)
    k_cols = slice(DK_R, 2 * DK_R)
    v_cols = slice(2 * DK_R, 2 * DK_R + DV_R)
    g_cols = slice(2 * DK_R + DV_R, 2 * DK_R + 2 * DV_R)
    dm = dm_ref[...]
    qd = qd_ref[...]
    kd = kd_ref[...]
    cd = cd_ref[...]
    gain = gain_ref[...]

    def project(r0):
        rows = slice(r0, r0 + g_rows)
        z_sc[rows, :] = jnp.dot(h_ref[rows, :], w_ref[...], preferred_element_type=F32)

    def scores(r0):
        rows = slice(r0, r0 + chunk)
        cos = cos_ref[rows, :]
        sin = sin_ref[rows, :]
        q = _rotate(z_sc[rows, q_cols], cos, sin)
        k = _rotate(z_sc[rows, k_cols], cos, sin) * (DK_R ** -0.5)
        a = (_dot_nt(q, k) * dm).astype(BF16)
        return a, (q * qd).astype(BF16), (k * kd).astype(BF16)

    def outputs(r0, parts, s):
        a, q_in, k_st = parts
        rows = slice(r0, r0 + chunk)
        v = z_sc[rows, v_cols].astype(BF16)
        o = (jnp.dot(a, v, preferred_element_type=F32)
             + jnp.dot(q_in, s.astype(BF16), preferred_element_type=F32))
        o_ref[rows, :] = _head_norm_gate(o, gain, z_sc[rows, g_cols])
        return cd * s + lax.dot_general(k_st, v, TN_DIMS, preferred_element_type=F32)

    s = jnp.zeros((DK_R, DV_R), F32)
    pending = {}
    for step in range(-lag, n_chunks + lag):
        ahead = step + lag
        if 0 <= ahead < n_chunks and ahead % group == 0:
            project(ahead * chunk)
        if 0 <= step < n_chunks:
            pending[step] = scores(step * chunk)
        if 0 <= step - lag < n_chunks:
            s = outputs((step - lag) * chunk, pending.pop(step - lag), s)
    s_ref[...] = s


def _retention(h3, w_r, cos, sin, gain):
    b, t, d = h3.shape
    c = RET_CHUNK
    n_head = 2 * DK_R + 2 * DV_R
    lg = np.log(1.0 - 2.0 ** (-5.0 - np.arange(H_R, dtype=np.float64)))
    idx = np.arange(c, dtype=np.float64)
    rel = idx[:, None] - idx[None, :]
    dm = np.where(rel >= 0, np.exp(lg[:, None, None] * np.maximum(rel, 0.0)), 0.0)
    qd = np.broadcast_to(np.exp(lg[:, None] * (idx + 1.0))[:, :, None], (H_R, c, DK_R))
    kd = np.broadcast_to(np.exp(lg[:, None] * (c - 1.0 - idx))[:, :, None], (H_R, c, DK_R))
    cd = np.broadcast_to(np.exp(lg * c)[:, None, None], (H_R, 1, DV_R))
    dm, qd, kd, cd = (jnp.asarray(a, F32) for a in (dm, qd, kd, cd))
    head_tab = lambda rows, cols: pl.BlockSpec((None, rows, cols), lambda bi, h: (h, 0, 0))
    return pl.pallas_call(
        functools.partial(_ret_kernel, chunk=c, group=RET_GROUP, lag=RET_LAG),
        grid=(b, H_R),
        in_specs=[
            pl.BlockSpec((None, t, d), lambda bi, h: (bi, 0, 0)),
            pl.BlockSpec((d, n_head), lambda bi, h: (0, h)),
            pl.BlockSpec((t, DK_R), lambda bi, h: (0, 0)),
            pl.BlockSpec((t, DK_R), lambda bi, h: (0, 0)),
            head_tab(c, c), head_tab(c, DK_R), head_tab(c, DK_R), head_tab(1, DV_R),
            pl.BlockSpec((1, DV_R), lambda bi, h: (0, h)),
        ],
        out_specs=[
            pl.BlockSpec((None, t, DV_R), lambda bi, h: (bi, 0, h)),
            pl.BlockSpec((None, None, DK_R, DV_R), lambda bi, h: (bi, h, 0, 0)),
        ],
        out_shape=[
            jax.ShapeDtypeStruct((b, t, V_R), F32),
            jax.ShapeDtypeStruct((b, H_R, DK_R, DV_R), F32),
        ],
        scratch_shapes=[pltpu.VMEM((t, n_head), F32)],
        compiler_params=_params(("parallel", "arbitrary")),
        name="retention",
    )(h3, w_r, cos, sin, dm, qd, kd, cd, gain)


def _lower_bound(lb_ref):
    a = lb_ref[...]
    a0, a1 = a[0:1, :], a[1:2, :]
    m = jnp.maximum(a0, a1)
    e0 = jnp.exp(a0 - m)
    e1 = jnp.exp(a1 - m)
    return e0 / (e0 + e1)


def _forget_gate(gf, lb):
    sg = jax.nn.sigmoid(gf)
    f = lb + (1.0 - lb) * sg
    return jnp.log(f), (1.0 - lb) * (1.0 - sg), f


def _cumsum_rows(tri, x):
    hi = x.astype(BF16)
    rest = x - hi.astype(F32)
    mid = rest.astype(BF16)
    lo = (rest - mid.astype(F32)).astype(BF16)
    y = jnp.dot(tri, jnp.concatenate([hi, mid, lo], axis=1), preferred_element_type=F32)
    n = x.shape[1]
    return y[:, :n] + y[:, n:2 * n] + y[:, 2 * n:]


def _hgrn_kernel(h_ref, w_ref, lb_ref, gain_ref, o_ref, s_ref, z_sc, q_sc, b_sc, c_sc,
                 *, chunk, sub, group, lag):
    assert sub == 8 and chunk % (2 * sub) == 0
    g_rows = group * chunk
    n_chunks = h_ref.shape[0] // chunk
    assert n_chunks % group == 0 and lag >= 1
    cols = lambda i: slice(i * DK_G, (i + 1) * DK_G)

    lb = _lower_bound(lb_ref)
    half_scale = 0.5 * (1.0 - lb)
    f_mid = lb + half_scale
    gain = gain_ref[...]
    r64 = lax.broadcasted_iota(jnp.int32, (chunk, chunk), 0)
    c64 = lax.broadcasted_iota(jnp.int32, (chunk, chunk), 1)
    tri = jnp.where(r64 >= c64, 1.0, 0.0).astype(BF16)
    causal_lane = jnp.where(r64[:sub] >= c64[:sub] % sub, c64[:sub], -1)
    zrows = lambda n: jnp.zeros((n, DK_G), F32)

    def project(r0):
        rows = slice(r0, r0 + g_rows)
        z_sc[rows, :] = jnp.dot(h_ref[rows, :], w_ref[...], preferred_element_type=F32)

    def gates(r0):
        rows = slice(r0, r0 + chunk)
        th_scaled = half_scale * jnp.tanh(0.5 * z_sc[rows, cols(1)])
        b_sc[rows, :] = _cumsum_rows(tri, jnp.log2(f_mid + th_scaled))
        c_sc[rows, :] = b_sc[rows, :] - jnp.log2(half_scale - th_scaled)
        q_sc[rows, :] = _silu(z_sc[rows, cols(0)])

    def diagonal(r0):
        blocks = []
        for j in range(chunk // sub):
            o = r0 + j * sub
            q, b = q_sc[o:o + sub, :], b_sc[o:o + sub, :]
            acc = jnp.zeros((sub, chunk), F32)
            for s in range(sub):
                p = jnp.exp2(b - c_sc[o + s:o + s + 1, :]) * q
                acc = jnp.where(causal_lane == j * sub + s, jnp.sum(p, axis=-1, keepdims=True), acc)
            blocks.append(acc)
        return jnp.concatenate(blocks, axis=0)

    def below(r0, w):
        qs, ks = [], []
        for mid in range(r0 + w, r0 + chunk, 2 * w):
            ref_b = b_sc[mid - 1:mid, :]
            qs += [zrows(w), q_sc[mid:mid + w, :] * jnp.exp2(b_sc[mid:mid + w, :] - ref_b)]
            ks += [jnp.exp2(ref_b - c_sc[mid - w:mid, :]), zrows(w)]
        scores = _dot_nt(jnp.concatenate(qs, axis=0), jnp.concatenate(ks, axis=0))
        return jnp.where(c64 // w == r64 // w - 1, scores, 0.0)

    def scores(r0):
        a = diagonal(r0)
        w = chunk // 2
        while w >= sub:
            a = a + below(r0, w)
            w //= 2
        return a

    def outputs(r0, a, st):
        rows = slice(r0, r0 + chunk)
        v = z_sc[rows, cols(2)].astype(BF16)
        b_last = b_sc[rows.stop - 1:rows.stop, :]
        o = _dot(a, v) + _dot_nt(q_sc[rows, :] * jnp.exp2(b_sc[rows, :]), st)
        o_ref[rows, :] = _head_norm_gate(o, gain, z_sc[rows, cols(3)])
        return st * jnp.exp2(b_last) + _dot_tn(v, jnp.exp2(b_last - c_sc[rows, :]))

    st = jnp.zeros((DV_G, DK_G), F32)
    pending = {}
    for step in range(-lag, n_chunks + 2 * lag):
        ahead = step + lag
        if 0 <= ahead < n_chunks and ahead % group == 0:
            project(ahead * chunk)
        if 0 <= step < n_chunks:
            gates(step * chunk)
        if 0 <= step - lag < n_chunks:
            pending[step - lag] = scores((step - lag) * chunk)
        if 0 <= step - 2 * lag < n_chunks:
            st = outputs((step - 2 * lag) * chunk, pending.pop(step - 2 * lag), st)
    s_ref[...] = st.T


def _hgrn(h3, w_hg, hg_lb, gain):
    b, t, d = h3.shape
    c, sub = HG_CHUNK, HG_SUB
    return pl.pallas_call(
        functools.partial(_hgrn_kernel, chunk=c, sub=sub, group=HG_GROUP, lag=HG_LAG),
        grid=(b, H_G),
        in_specs=[
            pl.BlockSpec((None, t, d), lambda bi, h: (bi, 0, 0)),
            pl.BlockSpec((d, 4 * DK_G), lambda bi, h: (0, h)),
            pl.BlockSpec((2, DK_G), lambda bi, h: (0, h)),
            pl.BlockSpec((1, DV_G), lambda bi, h: (0, h)),
        ],
        out_specs=[
            pl.BlockSpec((None, t, DV_G), lambda bi, h: (bi, 0, h)),
            pl.BlockSpec((None, None, DK_G, DV_G), lambda bi, h: (bi, h, 0, 0)),
        ],
        out_shape=[
            jax.ShapeDtypeStruct((b, t, V_G), F32),
            jax.ShapeDtypeStruct((b, H_G, DK_G, DV_G), F32),
        ],
        scratch_shapes=[
            pltpu.VMEM((t, 4 * DK_G), F32),
            pltpu.VMEM((t, DK_G), F32),
            pltpu.VMEM((t, DK_G), F32),
            pltpu.VMEM((t, DK_G), F32),
        ],
        compiler_params=_params(("parallel", "arbitrary")),
        name="hgrn",
    )(h3, w_hg, hg_lb, gain)


def _step_kernel(z_ref, cos_ref, sin_ref, lb_ref, rgain_ref, ggain_ref, sr0_ref, sg0_ref,
                 or_ref, og_ref, sr_ref, sg_ref, *, bb):
    first = lambda a: jnp.where(lax.broadcasted_iota(jnp.int32, a.shape, 0) == 0, a, 0.0)
    cos = cos_ref[...]
    sin = sin_ref[...]
    lb_all = _lower_bound(lb_ref)
    for i in range(bb):
        z8 = lambda lo, n: jnp.broadcast_to(z_ref[i, :, lo:lo + n], (8, n))
        for h in range(H_R):
            q = _rotate(z8(h * DK_R, DK_R), cos, sin)
            k = _rotate(z8(QK_R + h * DK_R, DK_R), cos, sin) * (DK_R ** -0.5)
            v = z8(2 * QK_R + h * DV_R, DV_R)
            gate = z_ref[i, :, 2 * QK_R + V_R + h * DV_R:2 * QK_R + V_R + (h + 1) * DV_R]
            gamma = 1.0 - 2.0 ** (-5.0 - h)
            s_new = gamma * sr0_ref[i, h] + _dot_tn(first(k), v)
            sr_ref[i, h] = s_new
            o = _dot(q, s_new)[0:1, :]
            or_ref[i, :, h * DV_R:(h + 1) * DV_R] = _head_norm_gate(
                o, rgain_ref[:, h * DV_R:(h + 1) * DV_R], gate)
        g0 = 2 * QK_R + 2 * V_R
        for h in range(H_G):
            lanes = slice(h * DK_G, (h + 1) * DK_G)
            q = _silu(z8(g0 + h * DK_G, DK_G))
            _, k, f = _forget_gate(z8(g0 + F_G + h * DK_G, DK_G), lb_all[:, lanes])
            v = z8(g0 + 2 * F_G + h * DV_G, DV_G)
            gate = z_ref[i, :, g0 + 2 * F_G + V_G + h * DV_G:g0 + 2 * F_G + V_G + (h + 1) * DV_G]
            st_new = sg0_ref[i, h].T * f[0:1, :] + _dot_tn(first(v), k)
            sg_ref[i, h] = st_new.T
            o = _dot_nt(q, st_new)[0:1, :]
            og_ref[i, :, lanes] = _head_norm_gate(o, ggain_ref[:, lanes], gate)


def _step(z, cos, sin, hg_lb, rgain, ggain, sr0, sg0, bb):
    n, d_in = z.shape
    z3 = z.reshape(n, 1, d_in)
    row = lambda w: pl.BlockSpec((bb, 1, w), lambda i: (i, 0, 0))
    const = lambda a: pl.BlockSpec(a.shape, lambda i: (0,) * a.ndim)
    sr_spec = pl.BlockSpec((bb, H_R, DK_R, DV_R), lambda i: (i, 0, 0, 0))
    sg_spec = pl.BlockSpec((bb, H_G, DK_G, DV_G), lambda i: (i, 0, 0, 0))
    return pl.pallas_call(
        functools.partial(_step_kernel, bb=bb),
        grid=(n // bb,),
        in_specs=[row(d_in), const(cos), const(sin), const(hg_lb), const(rgain), const(ggain),
                  sr_spec, sg_spec],
        out_specs=[row(V_R), row(V_G), sr_spec, sg_spec],
        out_shape=[
            jax.ShapeDtypeStruct((n, 1, V_R), F32),
            jax.ShapeDtypeStruct((n, 1, V_G), F32),
            jax.ShapeDtypeStruct(sr0.shape, F32),
            jax.ShapeDtypeStruct(sg0.shape, F32),
        ],
        compiler_params=_params(("parallel",)),
        name="step",
    )(z3, cos, sin, hg_lb, rgain, ggain, sr0, sg0)


def _post_kernel(x_ref, h_ref, gr_ref, gg_ref, p_ref, wa_ref, wur_ref, wug_ref, wo_ref,
                 nm_ref, w1_ref, w2_ref, np_ref, wg_ref, wp_ref, nf_ref, y_ref):
    d = x_ref.shape[1]
    merge = _sigmoid(jnp.dot(h_ref[...], wa_ref[...], preferred_element_type=F32))
    u_r = jnp.dot(gr_ref[...].astype(BF16), wur_ref[...], preferred_element_type=F32)
    u_g = jnp.dot(gg_ref[...].astype(BF16), wug_ref[...], preferred_element_type=F32)
    m = merge[:, :d] * u_r + merge[:, d:] * u_g
    r = x_ref[...] + jnp.dot(m.astype(BF16), wo_ref[...], preferred_element_type=F32)
    hm = _rms(r, nm_ref[...]).astype(BF16)
    a = jnp.maximum(jnp.dot(hm, w1_ref[...], preferred_element_type=F32), 0.0)
    r = r + jnp.dot((a * a).astype(BF16), w2_ref[...], preferred_element_type=F32)
    hp = _rms(r, np_ref[...]).astype(BF16)
    gate = _sigmoid(jnp.dot(hp, wg_ref[...], preferred_element_type=F32))
    r = r + gate * jnp.dot(p_ref[...].astype(BF16), wp_ref[...], preferred_element_type=F32)
    y_ref[...] = _rms(r, nf_ref[...])


def _post(x, h, g_r, g_g, p, wa, wur, wug, wo, nm, w1, w2, npl, wg, wp, nf, tm):
    n, d = x.shape
    rows = lambda w: pl.BlockSpec((tm, w), lambda i: (i, 0))
    const = lambda a: pl.BlockSpec(a.shape, lambda i: (0, 0), pipeline_mode=pl.Buffered(1))
    return pl.pallas_call(
        _post_kernel,
        grid=(n // tm,),
        in_specs=[rows(d), rows(d), rows(d), rows(d), rows(p.shape[1]),
                  const(wa), const(wur), const(wug), const(wo), const(nm), const(w1), const(w2),
                  const(npl), const(wg), const(wp), const(nf)],
        out_specs=rows(d),
        out_shape=jax.ShapeDtypeStruct((n, d), F32),
        compiler_params=_params(("parallel",)),
        name="post",
    )(x, h, g_r, g_g, p, wa, wur, wug, wo, nm, w1, w2, npl, wg, wp, nf)


def _rope_tables(pos):
    half = DK_R // 2
    inv = ROPE_BASE ** (-jnp.arange(half, dtype=F32) / half)
    ang = pos[:, None] * inv[None, :]
    cos, sin = jnp.cos(ang), jnp.sin(ang)
    return jnp.concatenate([cos, cos], axis=-1), jnp.concatenate([-sin, sin], axis=-1)


def kernel(x_prompt, x_sample, state_ret, state_hgrn, p_prompt, p_sample, norm_mix_g, w_in,
           ret_norm_g, hg_norm_g, hg_lb, w_up_ret, w_up_hg, w_o, norm_mlp_g, w_ff1, w_ff2,
           norm_ple_g, w_ple_gate, w_ple_proj, norm_final_g):
    b, t, d = x_prompt.shape
    nb, td, _ = x_sample.shape
    assert w_in.shape[0] == 1 and td == 1 and hg_lb.shape[0] == 2
    bf = lambda w: w[0].astype(BF16)
    row = lambda g: g.reshape(1, -1)
    w_in_b = bf(w_in)
    n_ret = 2 * QK_R + 2 * V_R
    n_hg = 2 * F_G + 2 * V_G
    heads = lambda lo, width: w_in_b[:, lo:lo + H_R * width].reshape(d, H_R, width)
    w_ret = jnp.concatenate([heads(0, DK_R), heads(QK_R, DK_R), heads(2 * QK_R, DV_R),
                             heads(2 * QK_R + V_R, DV_R)], axis=2).reshape(d, n_ret)
    w_hg = w_in_b[:, n_ret:n_ret + n_hg].reshape(d, 4, H_G, DK_G).transpose(0, 2, 1, 3).reshape(d, n_hg)
    post_w = (w_in_b[:, n_ret + n_hg:], bf(w_up_ret), bf(w_up_hg), bf(w_o), row(norm_mlp_g[0]),
              bf(w_ff1), bf(w_ff2), row(norm_ple_g[0]), bf(w_ple_gate), bf(w_ple_proj), row(norm_final_g))
    g_mix = row(norm_mix_g[0])
    rgain = row(ret_norm_g[0])
    ggain = row(hg_norm_g[0])

    xp = x_prompt.reshape(b * t, d)
    hp = _norm(xp, g_mix, tm=1024)
    hp3 = hp.reshape(b, t, d)
    cos_p, sin_p = _rope_tables(jnp.arange(t, dtype=F32))
    gr_p, ret_p = _retention(hp3, w_ret, cos_p, sin_p, rgain)
    gg_p, hg_p = _hgrn(hp3, w_hg, hg_lb, ggain)
    y_p = _post(xp, hp, gr_p.reshape(b * t, -1), gg_p.reshape(b * t, -1),
                p_prompt[0].reshape(b * t, -1), *post_w, tm=256)

    xs = x_sample.reshape(nb, d)
    hs = _norm(xs, g_mix, tm=nb)
    zs = _proj(hs, w_in_b, tm=nb, tn=1024)
    cos_s, sin_s = _rope_tables(PAST_LEN + jnp.arange(td, dtype=F32))
    gr_s, gg_s, ret_s, hg_s = _step(zs, cos_s, sin_s, hg_lb, rgain, ggain,
                                    state_ret[0], state_hgrn[0], bb=2)
    y_s = _post(xs, hs, gr_s.reshape(nb, -1), gg_s.reshape(nb, -1),
                p_sample[0].reshape(nb, -1), *post_w, tm=nb)

    return (y_p.reshape(b, t, d), y_s.reshape(nb, td, d), ret_p[None], hg_p[None],
            ret_s[None], hg_s[None])
```

```python
import functools

import jax
import jax.numpy as jnp
import numpy as np
from jax import lax
from jax.experimental import pallas as pl
from jax.experimental.pallas import tpu as pltpu

F32 = jnp.float32
BF16 = jnp.bfloat16

PAST_LEN = 16384
ROPE_BASE = 10000.0
EPS = 1e-6

H_R, DK_R, DV_R = 4, 128, 256
H_G, DK_G, DV_G = 8, 128, 128
QK_R = H_R * DK_R
V_R = H_R * DV_R
F_G = H_G * DK_G
V_G = H_G * DV_G

V7X_VMEM_BYTES = 64 * 1024 * 1024
VMEM_LIMIT = V7X_VMEM_BYTES - 8 * 1024 * 1024

RET_CHUNK = 128
RET_GROUP = 2
RET_LAG = 1
HG_CHUNK = 64
HG_SUB = 8
HG_GROUP = 4
HG_LAG = 2

NT_DIMS = (((1,), (1,)), ((), ()))
TN_DIMS = (((0,), (0,)), ((), ()))


def _dot(a, b):
    return jnp.dot(a.astype(BF16), b.astype(BF16), preferred_element_type=F32)


def _dot_nt(a, b):
    return lax.dot_general(a.astype(BF16), b.astype(BF16), NT_DIMS, preferred_element_type=F32)


def _dot_tn(a, b):
    return lax.dot_general(a.astype(BF16), b.astype(BF16), TN_DIMS, preferred_element_type=F32)


def _rms(x, g):
    return x * lax.rsqrt(jnp.mean(x * x, axis=-1, keepdims=True) + EPS) * g


def _sigmoid(x):
    return 0.5 * jnp.tanh(0.5 * x) + 0.5


def _silu(x):
    half = 0.5 * x
    return half + half * jnp.tanh(half)


def _params(sem):
    return pltpu.CompilerParams(dimension_semantics=sem, vmem_limit_bytes=VMEM_LIMIT)


def _norm_kernel(x_ref, g_ref, h_ref):
    h_ref[...] = _rms(x_ref[...], g_ref[...]).astype(BF16)


def _norm(x, g, tm):
    n, d = x.shape
    return pl.pallas_call(
        _norm_kernel,
        grid=(n // tm,),
        in_specs=[pl.BlockSpec((tm, d), lambda i: (i, 0)), pl.BlockSpec((1, d), lambda i: (0, 0))],
        out_specs=pl.BlockSpec((tm, d), lambda i: (i, 0)),
        out_shape=jax.ShapeDtypeStruct((n, d), BF16),
        compiler_params=_params(("parallel",)),
        name="norm",
    )(x, g)


def _proj_kernel(h_ref, w_ref, z_ref):
    z_ref[...] = jnp.dot(h_ref[...], w_ref[...].astype(BF16), preferred_element_type=F32)


def _proj(h, w_in, tm, tn):
    n, d = h.shape
    d_out = w_in.shape[2]
    return pl.pallas_call(
        _proj_kernel,
        grid=(n // tm, d_out // tn),
        in_specs=[
            pl.BlockSpec((tm, d), lambda i, j: (i, 0)),
            pl.BlockSpec((None, d, tn), lambda i, j: (0, 0, j)),
        ],
        out_specs=pl.BlockSpec((tm, tn), lambda i, j: (i, j)),
        out_shape=jax.ShapeDtypeStruct((n, d_out), F32),
        compiler_params=_params(("parallel", "arbitrary")),
        name="proj",
    )(h, w_in)


def _rotate(x, cos, sin_signed):
    return x * cos + pltpu.roll(x, DK_R // 2, 1) * sin_signed


def _head_norm_gate(o, gain, gate):
    return o * lax.rsqrt(jnp.mean(o * o, axis=-1, keepdims=True) + EPS) * gain * _silu(gate)


def _ret_kernel(h_ref, wq_ref, wk_ref, wv_ref, wg_ref, cos_ref, sin_ref, dm_ref, qd_ref, kd_ref,
                cd_ref, gain_ref, o_ref, s_ref, w_sc, z_sc, *, chunk, group, lag):
    n_chunks = h_ref.shape[0] // chunk
    assert n_chunks % group == 0 and lag >= 1
    g_rows = group * chunk
    q_cols = slice(0, DK_R)
    k_cols = slice(DK_R, 2 * DK_R)
    v_cols = slice(2 * DK_R, 2 * DK_R + DV_R)
    g_cols = slice(2 * DK_R + DV_R, 2 * DK_R + 2 * DV_R)
    for cols, w_ref in ((q_cols, wq_ref), (k_cols, wk_ref), (v_cols, wv_ref), (g_cols, wg_ref)):
        w_sc[:, cols] = w_ref[...].astype(BF16)
    dm = dm_ref[...]
    qd = qd_ref[...]
    kd = kd_ref[...]
    cd = cd_ref[...]
    gain = gain_ref[...]

    def project(r0):
        rows = slice(r0, r0 + g_rows)
        z_sc[rows, :] = jnp.dot(h_ref[rows, :], w_sc[...], preferred_element_type=F32)

    def scores(r0):
        rows = slice(r0, r0 + chunk)
        cos = cos_ref[rows, :]
        sin = sin_ref[rows, :]
        q = _rotate(z_sc[rows, q_cols], cos, sin)
        k = _rotate(z_sc[rows, k_cols], cos, sin) * (DK_R ** -0.5)
        a = (_dot_nt(q, k) * dm).astype(BF16)
        return a, (q * qd).astype(BF16), (k * kd).astype(BF16)

    def outputs(r0, parts, s):
        a, q_in, k_st = parts
        rows = slice(r0, r0 + chunk)
        v = z_sc[rows, v_cols].astype(BF16)
        o = (jnp.dot(a, v, preferred_element_type=F32)
             + jnp.dot(q_in, s.astype(BF16), preferred_element_type=F32))
        o_ref[rows, :] = _head_norm_gate(o, gain, z_sc[rows, g_cols])
        return cd * s + lax.dot_general(k_st, v, TN_DIMS, preferred_element_type=F32)

    s = jnp.zeros((DK_R, DV_R), F32)
    pending = {}
    for step in range(-lag, n_chunks + lag):
        ahead = step + lag
        if 0 <= ahead < n_chunks and ahead % group == 0:
            project(ahead * chunk)
        if 0 <= step < n_chunks:
            pending[step] = scores(step * chunk)
        if 0 <= step - lag < n_chunks:
            s = outputs((step - lag) * chunk, pending.pop(step - lag), s)
    s_ref[...] = s


def _retention(h3, w_in, cos, sin, gain):
    b, t, d = h3.shape
    c = RET_CHUNK
    n_head = 2 * DK_R + 2 * DV_R
    w_block = lambda width, lo: pl.BlockSpec((None, d, width), lambda bi, h: (0, 0, lo // width + h))
    lg = np.log(1.0 - 2.0 ** (-5.0 - np.arange(H_R, dtype=np.float64)))
    idx = np.arange(c, dtype=np.float64)
    rel = idx[:, None] - idx[None, :]
    dm = np.where(rel >= 0, np.exp(lg[:, None, None] * np.maximum(rel, 0.0)), 0.0)
    qd = np.broadcast_to(np.exp(lg[:, None] * (idx + 1.0))[:, :, None], (H_R, c, DK_R))
    kd = np.broadcast_to(np.exp(lg[:, None] * (c - 1.0 - idx))[:, :, None], (H_R, c, DK_R))
    cd = np.broadcast_to(np.exp(lg * c)[:, None, None], (H_R, 1, DV_R))
    dm, qd, kd, cd = (jnp.asarray(a, F32) for a in (dm, qd, kd, cd))
    head_tab = lambda rows, cols: pl.BlockSpec((None, rows, cols), lambda bi, h: (h, 0, 0))
    return pl.pallas_call(
        functools.partial(_ret_kernel, chunk=c, group=RET_GROUP, lag=RET_LAG),
        grid=(b, H_R),
        in_specs=[
            pl.BlockSpec((None, t, d), lambda bi, h: (bi, 0, 0)),
            w_block(DK_R, 0), w_block(DK_R, QK_R), w_block(DV_R, 2 * QK_R), w_block(DV_R, 2 * QK_R + V_R),
            pl.BlockSpec((t, DK_R), lambda bi, h: (0, 0)),
            pl.BlockSpec((t, DK_R), lambda bi, h: (0, 0)),
            head_tab(c, c), head_tab(c, DK_R), head_tab(c, DK_R), head_tab(1, DV_R),
            pl.BlockSpec((1, DV_R), lambda bi, h: (0, h)),
        ],
        out_specs=[
            pl.BlockSpec((None, t, DV_R), lambda bi, h: (bi, 0, h)),
            pl.BlockSpec((None, None, DK_R, DV_R), lambda bi, h: (bi, h, 0, 0)),
        ],
        out_shape=[
            jax.ShapeDtypeStruct((b, t, V_R), F32),
            jax.ShapeDtypeStruct((b, H_R, DK_R, DV_R), F32),
        ],
        scratch_shapes=[pltpu.VMEM((d, n_head), BF16), pltpu.VMEM((t, n_head), F32)],
        compiler_params=_params(("parallel", "arbitrary")),
        name="retention",
    )(h3, w_in, w_in, w_in, w_in, cos, sin, dm, qd, kd, cd, gain)


def _lower_bound(lb_ref):
    a = lb_ref[...]
    a0, a1 = a[0:1, :], a[1:2, :]
    m = jnp.maximum(a0, a1)
    e0 = jnp.exp(a0 - m)
    e1 = jnp.exp(a1 - m)
    return e0 / (e0 + e1)


def _forget_gate(gf, lb):
    sg = jax.nn.sigmoid(gf)
    f = lb + (1.0 - lb) * sg
    return jnp.log(f), (1.0 - lb) * (1.0 - sg), f


def _cumsum_rows(tri, x):
    hi = x.astype(BF16)
    rest = x - hi.astype(F32)
    mid = rest.astype(BF16)
    lo = (rest - mid.astype(F32)).astype(BF16)
    y = jnp.dot(tri, jnp.concatenate([hi, mid, lo], axis=1), preferred_element_type=F32)
    n = x.shape[1]
    return y[:, :n] + y[:, n:2 * n] + y[:, 2 * n:]


def _hgrn_kernel(h_ref, wq_ref, wf_ref, wi_ref, wg_ref, lb_ref, gain_ref, o_ref, s_ref,
                 w_sc, z_sc, q_sc, b_sc, c_sc, *, chunk, sub, group, lag):
    assert sub == 8 and chunk % (2 * sub) == 0
    g_rows = group * chunk
    n_chunks = h_ref.shape[0] // chunk
    assert n_chunks % group == 0 and lag >= 1
    cols = lambda i: slice(i * DK_G, (i + 1) * DK_G)
    for i, w_ref in enumerate((wq_ref, wf_ref, wi_ref, wg_ref)):
        w_sc[:, cols(i)] = w_ref[...].astype(BF16)

    lb = _lower_bound(lb_ref)
    half_scale = 0.5 * (1.0 - lb)
    f_mid = lb + half_scale
    gain = gain_ref[...]
    r64 = lax.broadcasted_iota(jnp.int32, (chunk, chunk), 0)
    c64 = lax.broadcasted_iota(jnp.int32, (chunk, chunk), 1)
    tri = jnp.where(r64 >= c64, 1.0, 0.0).astype(BF16)
    same_sub = (r64 // sub) == (c64 // sub)
    place = [(c64[:sub] % sub == s) & (r64[:sub] >= s) for s in range(sub)]
    zrows = lambda n: jnp.zeros((n, DK_G), F32)

    def project(r0):
        rows = slice(r0, r0 + g_rows)
        z_sc[rows, :] = jnp.dot(h_ref[rows, :], w_sc[...], preferred_element_type=F32)

    def gates(r0):
        rows = slice(r0, r0 + chunk)
        th_scaled = half_scale * jnp.tanh(0.5 * z_sc[rows, cols(1)])
        b_sc[rows, :] = _cumsum_rows(tri, jnp.log2(f_mid + th_scaled))
        c_sc[rows, :] = b_sc[rows, :] - jnp.log2(half_scale - th_scaled)
        q_sc[rows, :] = _silu(z_sc[rows, cols(0)])

    def diagonal(r0):
        blocks = []
        for o in range(r0, r0 + chunk, sub):
            q, b = q_sc[o:o + sub, :], b_sc[o:o + sub, :]
            acc = jnp.zeros((sub, chunk), F32)
            for s in range(sub):
                p = jnp.exp2(b - c_sc[o + s:o + s + 1, :]) * q
                acc = jnp.where(place[s], jnp.sum(p, axis=-1, keepdims=True), acc)
            blocks.append(acc)
        return jnp.where(same_sub, jnp.concatenate(blocks, axis=0), 0.0)

    def below(r0, w):
        qs, ks = [], []
        for mid in range(r0 + w, r0 + chunk, 2 * w):
            ref_b = b_sc[mid - 1:mid, :]
            qs += [zrows(w), q_sc[mid:mid + w, :] * jnp.exp2(b_sc[mid:mid + w, :] - ref_b)]
            ks += [jnp.exp2(ref_b - c_sc[mid - w:mid, :]), zrows(w)]
        scores = _dot_nt(jnp.concatenate(qs, axis=0), jnp.concatenate(ks, axis=0))
        return jnp.where(c64 // w == r64 // w - 1, scores, 0.0)

    def scores(r0):
        a = diagonal(r0)
        w = chunk // 2
        while w >= sub:
            a = a + below(r0, w)
            w //= 2
        return a

    def outputs(r0, a, st):
        rows = slice(r0, r0 + chunk)
        v = z_sc[rows, cols(2)].astype(BF16)
        b_last = b_sc[rows.stop - 1:rows.stop, :]
        o = _dot(a, v) + _dot_nt(q_sc[rows, :] * jnp.exp2(b_sc[rows, :]), st)
        o_ref[rows, :] = _head_norm_gate(o, gain, z_sc[rows, cols(3)])
        return st * jnp.exp2(b_last) + _dot_tn(v, jnp.exp2(b_last - c_sc[rows, :]))

    st = jnp.zeros((DV_G, DK_G), F32)
    pending = {}
    for step in range(-lag, n_chunks + 2 * lag):
        ahead = step + lag
        if 0 <= ahead < n_chunks and ahead % group == 0:
            project(ahead * chunk)
        if 0 <= step < n_chunks:
            gates(step * chunk)
        if 0 <= step - lag < n_chunks:
            pending[step - lag] = scores((step - lag) * chunk)
        if 0 <= step - 2 * lag < n_chunks:
            st = outputs((step - 2 * lag) * chunk, pending.pop(step - 2 * lag), st)
    s_ref[...] = st.T


def _hgrn(h3, w_in, hg_lb, gain):
    b, t, d = h3.shape
    c, sub = HG_CHUNK, HG_SUB
    first = (2 * QK_R + 2 * V_R) // DK_G
    w_block = lambda role: pl.BlockSpec((None, d, DK_G), lambda bi, h: (0, 0, first + role * H_G + h))
    return pl.pallas_call(
        functools.partial(_hgrn_kernel, chunk=c, sub=sub, group=HG_GROUP, lag=HG_LAG),
        grid=(b, H_G),
        in_specs=[
            pl.BlockSpec((None, t, d), lambda bi, h: (bi, 0, 0)),
            w_block(0), w_block(1), w_block(2), w_block(3),
            pl.BlockSpec((2, DK_G), lambda bi, h: (0, h)),
            pl.BlockSpec((1, DV_G), lambda bi, h: (0, h)),
        ],
        out_specs=[
            pl.BlockSpec((None, t, DV_G), lambda bi, h: (bi, 0, h)),
            pl.BlockSpec((None, None, DK_G, DV_G), lambda bi, h: (bi, h, 0, 0)),
        ],
        out_shape=[
            jax.ShapeDtypeStruct((b, t, V_G), F32),
            jax.ShapeDtypeStruct((b, H_G, DK_G, DV_G), F32),
        ],
        scratch_shapes=[
            pltpu.VMEM((d, 4 * DK_G), BF16),
            pltpu.VMEM((t, 4 * DK_G), F32),
            pltpu.VMEM((t, DK_G), F32),
            pltpu.VMEM((t, DK_G), F32),
            pltpu.VMEM((t, DK_G), F32),
        ],
        compiler_params=_params(("parallel", "arbitrary")),
        name="hgrn",
    )(h3, w_in, w_in, w_in, w_in, hg_lb, gain)


def _step_kernel(z_ref, cos_ref, sin_ref, lb_ref, rgain_ref, ggain_ref, sr0_ref, sg0_ref,
                 or_ref, og_ref, sr_ref, sg_ref, *, bb):
    first = lambda a: jnp.where(lax.broadcasted_iota(jnp.int32, a.shape, 0) == 0, a, 0.0)
    cos = cos_ref[...]
    sin = sin_ref[...]
    lb_all = _lower_bound(lb_ref)
    row8 = lax.broadcasted_iota(jnp.int32, (8, DK_G), 0)
    ones3 = jnp.where(row8 < 3, 1.0, 0.0)
    g0 = 2 * QK_R + 2 * V_R
    work = []
    for i in range(bb):
        z8 = lambda lo, n: jnp.broadcast_to(z_ref[i, :, lo:lo + n], (8, n))
        for h in range(H_R):
            q = _rotate(z8(h * DK_R, DK_R), cos, sin)
            k = _rotate(z8(QK_R + h * DK_R, DK_R), cos, sin) * (DK_R ** -0.5)
            v = z8(2 * QK_R + h * DV_R, DV_R)
            lanes = slice(h * DV_R, (h + 1) * DV_R)
            gate = z_ref[i, :, 2 * QK_R + V_R + h * DV_R:2 * QK_R + V_R + (h + 1) * DV_R]
            work.append((sr0_ref, sr_ref, or_ref, rgain_ref, i, h, lanes, q, gate,
                         1.0 - 2.0 ** (-5.0 - h), _dot_tn(first(k), v)))
        for h in range(H_G):
            lanes = slice(h * DK_G, (h + 1) * DK_G)
            q = _silu(z8(g0 + h * DK_G, DK_G))
            _, k, f = _forget_gate(z8(g0 + F_G + h * DK_G, DK_G), lb_all[:, lanes])
            v = z8(g0 + 2 * F_G + h * DV_G, DV_G)
            gate = z_ref[i, :, g0 + 2 * F_G + V_G + h * DV_G:g0 + 2 * F_G + V_G + (h + 1) * DV_G]
            f_hi = f.astype(BF16).astype(F32)
            f_mid = (f - f_hi).astype(BF16).astype(F32)
            f_lo = f - f_hi - f_mid
            lhs = jnp.where(row8 == 0, f_hi, jnp.where(row8 == 1, f_mid, jnp.where(row8 == 2, f_lo,
                            jnp.where(row8 == 3, k, 0.0))))
            rhs = jnp.concatenate([ones3, jnp.where(row8 == 3, v, 0.0)], axis=1)
            both = _dot_tn(lhs, rhs)
            work.append((sg0_ref, sg_ref, og_ref, ggain_ref, i, h, lanes, q, gate,
                         both[:, :DV_G], both[:, DV_G:]))
    outs = []
    for s0_ref, s_ref, _, _, i, h, _, q, _, decay, kv in work:
        s_new = decay * s0_ref[i, h] + kv
        s_ref[i, h] = s_new
        outs.append(_dot(q, s_new)[0:1, :])
    for (_, _, o_ref, gain_ref, i, _, lanes, _, gate, _, _), o in zip(work, outs):
        o_ref[i, :, lanes] = _head_norm_gate(o, gain_ref[:, lanes], gate)


def _step(z, cos, sin, hg_lb, rgain, ggain, sr0, sg0, bb):
    n, d_in = z.shape
    z3 = z.reshape(n, 1, d_in)
    row = lambda w: pl.BlockSpec((bb, 1, w), lambda i: (i, 0, 0))
    const = lambda a: pl.BlockSpec(a.shape, lambda i: (0,) * a.ndim)
    sr_spec = pl.BlockSpec((bb, H_R, DK_R, DV_R), lambda i: (i, 0, 0, 0))
    sg_spec = pl.BlockSpec((bb, H_G, DK_G, DV_G), lambda i: (i, 0, 0, 0))
    return pl.pallas_call(
        functools.partial(_step_kernel, bb=bb),
        grid=(n // bb,),
        in_specs=[row(d_in), const(cos), const(sin), const(hg_lb), const(rgain), const(ggain),
                  sr_spec, sg_spec],
        out_specs=[row(V_R), row(V_G), sr_spec, sg_spec],
        out_shape=[
            jax.ShapeDtypeStruct((n, 1, V_R), F32),
            jax.ShapeDtypeStruct((n, 1, V_G), F32),
            jax.ShapeDtypeStruct(sr0.shape, F32),
            jax.ShapeDtypeStruct(sg0.shape, F32),
        ],
        compiler_params=_params(("parallel",)),
        name="step",
    )(z3, cos, sin, hg_lb, rgain, ggain, sr0, sg0)


def _post_kernel(x_ref, h_ref, gr_ref, gg_ref, p_ref, wa_ref, wur_ref, wug_ref, wo_ref,
                 nm_ref, w1_ref, w2_ref, np_ref, wg_ref, wp_ref, nf_ref, y_ref):
    d = x_ref.shape[1]
    merge = _sigmoid(jnp.dot(h_ref[...], wa_ref[...], preferred_element_type=F32))
    u_r = jnp.dot(gr_ref[...].astype(BF16), wur_ref[...], preferred_element_type=F32)
    u_g = jnp.dot(gg_ref[...].astype(BF16), wug_ref[...], preferred_element_type=F32)
    m = merge[:, :d] * u_r + merge[:, d:] * u_g
    r = x_ref[...] + jnp.dot(m.astype(BF16), wo_ref[...], preferred_element_type=F32)
    hm = _rms(r, nm_ref[...]).astype(BF16)
    a = jnp.maximum(jnp.dot(hm, w1_ref[...], preferred_element_type=F32), 0.0)
    r = r + jnp.dot((a * a).astype(BF16), w2_ref[...], preferred_element_type=F32)
    hp = _rms(r, np_ref[...]).astype(BF16)
    gate = _sigmoid(jnp.dot(hp, wg_ref[...], preferred_element_type=F32))
    r = r + gate * jnp.dot(p_ref[...].astype(BF16), wp_ref[...], preferred_element_type=F32)
    y_ref[...] = _rms(r, nf_ref[...])


def _post(x, h, g_r, g_g, p, wa, wur, wug, wo, nm, w1, w2, npl, wg, wp, nf, tm):
    n, d = x.shape
    rows = lambda w: pl.BlockSpec((tm, w), lambda i: (i, 0))
    const = lambda a: pl.BlockSpec(a.shape, lambda i: (0, 0), pipeline_mode=pl.Buffered(1))
    return pl.pallas_call(
        _post_kernel,
        grid=(n // tm,),
        in_specs=[rows(d), rows(d), rows(d), rows(d), rows(p.shape[1]),
                  const(wa), const(wur), const(wug), const(wo), const(nm), const(w1), const(w2),
                  const(npl), const(wg), const(wp), const(nf)],
        out_specs=rows(d),
        out_shape=jax.ShapeDtypeStruct((n, d), F32),
        compiler_params=_params(("parallel",)),
        name="post",
    )(x, h, g_r, g_g, p, wa, wur, wug, wo, nm, w1, w2, npl, wg, wp, nf)


def _rope_tables(pos):
    half = DK_R // 2
    inv = ROPE_BASE ** (-jnp.arange(half, dtype=F32) / half)
    ang = pos[:, None] * inv[None, :]
    cos, sin = jnp.cos(ang), jnp.sin(ang)
    return jnp.concatenate([cos, cos], axis=-1), jnp.concatenate([-sin, sin], axis=-1)


def kernel(x_prompt, x_sample, state_ret, state_hgrn, p_prompt, p_sample, norm_mix_g, w_in,
           ret_norm_g, hg_norm_g, hg_lb, w_up_ret, w_up_hg, w_o, norm_mlp_g, w_ff1, w_ff2,
           norm_ple_g, w_ple_gate, w_ple_proj, norm_final_g):
    b, t, d = x_prompt.shape
    nb, td, _ = x_sample.shape
    assert w_in.shape[0] == 1 and td == 1 and hg_lb.shape[0] == 2
    bf = lambda w: w[0].astype(BF16)
    row = lambda g: g.reshape(1, -1)
    n_mix = 2 * QK_R + 2 * V_R + 2 * F_G + 2 * V_G
    post_w = (w_in[0, :, n_mix:].astype(BF16), bf(w_up_ret), bf(w_up_hg), bf(w_o), row(norm_mlp_g[0]),
              bf(w_ff1), bf(w_ff2), row(norm_ple_g[0]), bf(w_ple_gate), bf(w_ple_proj), row(norm_final_g))
    g_mix = row(norm_mix_g[0])
    rgain = row(ret_norm_g[0])
    ggain = row(hg_norm_g[0])

    xp = x_prompt.reshape(b * t, d)
    hp = _norm(xp, g_mix, tm=1024)
    hp3 = hp.reshape(b, t, d)
    cos_p, sin_p = _rope_tables(jnp.arange(t, dtype=F32))
    gr_p, ret_p = _retention(hp3, w_in, cos_p, sin_p, rgain)
    gg_p, hg_p = _hgrn(hp3, w_in, hg_lb, ggain)
    y_p = _post(xp, hp, gr_p.reshape(b * t, -1), gg_p.reshape(b * t, -1),
                p_prompt[0].reshape(b * t, -1), *post_w, tm=256)

    xs = x_sample.reshape(nb, d)
    hs = _norm(xs, g_mix, tm=nb)
    zs = _proj(hs, w_in, tm=nb, tn=1024)
    cos_s, sin_s = _rope_tables(PAST_LEN + jnp.arange(td, dtype=F32))
    gr_s, gg_s, ret_s, hg_s = _step(zs, cos_s, sin_s, hg_lb, rgain, ggain,
                                    state_ret[0], state_hgrn[0], bb=4)
    y_s = _post(xs, hs, gr_s.reshape(nb, -1), gg_s.reshape(nb, -1),
                p_sample[0].reshape(nb, -1), *post_w, tm=nb)

    return (y_p.reshape(b, t, d), y_s.reshape(nb, td, d), ret_p[None], hg_p[None],
            ret_s[None], hg_s[None])
```

```python
import functools

import jax
import jax.numpy as jnp
import numpy as np
from jax import lax
from jax.experimental import pallas as pl
from jax.experimental.pallas import tpu as pltpu

F32 = jnp.float32
BF16 = jnp.bfloat16

PAST_LEN = 16384
ROPE_BASE = 10000.0
EPS = 1e-6

H_R, DK_R, DV_R = 4, 128, 256
H_G, DK_G, DV_G = 8, 128, 128
QK_R = H_R * DK_R
V_R = H_R * DV_R
F_G = H_G * DK_G
V_G = H_G * DV_G

V7X_VMEM_BYTES = 64 * 1024 * 1024
VMEM_LIMIT = V7X_VMEM_BYTES - 8 * 1024 * 1024

RET_CHUNK = 128
RET_GROUP = 2
RET_LAG = 1
HG_CHUNK = 64
HG_SUB = 8
HG_GROUP = 4
HG_LAG = 2
HG_HEADS = 2

NT_DIMS = (((1,), (1,)), ((), ()))
TN_DIMS = (((0,), (0,)), ((), ()))


def _dot(a, b):
    return jnp.dot(a.astype(BF16), b.astype(BF16), preferred_element_type=F32)


def _dot_nt(a, b):
    return lax.dot_general(a.astype(BF16), b.astype(BF16), NT_DIMS, preferred_element_type=F32)


def _dot_tn(a, b):
    return lax.dot_general(a.astype(BF16), b.astype(BF16), TN_DIMS, preferred_element_type=F32)


def _rms(x, g):
    return x * lax.rsqrt(jnp.mean(x * x, axis=-1, keepdims=True) + EPS) * g


def _sigmoid(x):
    return 0.5 * jnp.tanh(0.5 * x) + 0.5


def _silu(x):
    half = 0.5 * x
    return half + half * jnp.tanh(half)


def _params(sem):
    return pltpu.CompilerParams(dimension_semantics=sem, vmem_limit_bytes=VMEM_LIMIT)


def _norm_kernel(x_ref, g_ref, h_ref):
    h_ref[...] = _rms(x_ref[...], g_ref[...]).astype(BF16)


def _norm(x, g, tm):
    n, d = x.shape
    return pl.pallas_call(
        _norm_kernel,
        grid=(n // tm,),
        in_specs=[pl.BlockSpec((tm, d), lambda i: (i, 0)), pl.BlockSpec((1, d), lambda i: (0, 0))],
        out_specs=pl.BlockSpec((tm, d), lambda i: (i, 0)),
        out_shape=jax.ShapeDtypeStruct((n, d), BF16),
        compiler_params=_params(("parallel",)),
        name="norm",
    )(x, g)


def _proj_kernel(h_ref, w_ref, z_ref):
    z_ref[...] = jnp.dot(h_ref[...], w_ref[...].astype(BF16), preferred_element_type=F32)


def _proj(h, w_in, tm, tn):
    n, d = h.shape
    d_out = w_in.shape[2]
    return pl.pallas_call(
        _proj_kernel,
        grid=(n // tm, d_out // tn),
        in_specs=[
            pl.BlockSpec((tm, d), lambda i, j: (i, 0)),
            pl.BlockSpec((None, d, tn), lambda i, j: (0, 0, j)),
        ],
        out_specs=pl.BlockSpec((tm, tn), lambda i, j: (i, j)),
        out_shape=jax.ShapeDtypeStruct((n, d_out), F32),
        compiler_params=_params(("parallel", "arbitrary")),
        name="proj",
    )(h, w_in)


def _rotate(x, cos, sin_signed):
    return x * cos + pltpu.roll(x, DK_R // 2, 1) * sin_signed


def _head_norm_gate(o, gain, gate):
    return o * lax.rsqrt(jnp.mean(o * o, axis=-1, keepdims=True) + EPS) * gain * _silu(gate)


def _ret_kernel(h_ref, wq_ref, wk_ref, wv_ref, wg_ref, cos_ref, sin_ref, dm_ref, qd_ref, kd_ref,
                cd_ref, gain_ref, o_ref, s_ref, w_sc, z_sc, *, chunk, group, lag):
    n_chunks = h_ref.shape[0] // chunk
    assert n_chunks % group == 0 and lag >= 1
    g_rows = group * chunk
    q_cols = slice(0, DK_R)
    k_cols = slice(DK_R, 2 * DK_R)
    v_cols = slice(2 * DK_R, 2 * DK_R + DV_R)
    g_cols = slice(2 * DK_R + DV_R, 2 * DK_R + 2 * DV_R)
    for cols, w_ref in ((q_cols, wq_ref), (k_cols, wk_ref), (v_cols, wv_ref), (g_cols, wg_ref)):
        w_sc[:, cols] = w_ref[...].astype(BF16)
    dm = dm_ref[...]
    qd = qd_ref[...]
    kd = kd_ref[...]
    cd = cd_ref[...]
    gain = gain_ref[...]

    def project(r0):
        rows = slice(r0, r0 + g_rows)
        z_sc[rows, :] = jnp.dot(h_ref[rows, :], w_sc[...], preferred_element_type=F32)

    def scores(r0):
        rows = slice(r0, r0 + chunk)
        cos = cos_ref[rows, :]
        sin = sin_ref[rows, :]
        q = _rotate(z_sc[rows, q_cols], cos, sin)
        k = _rotate(z_sc[rows, k_cols], cos, sin) * (DK_R ** -0.5)
        a = (_dot_nt(q, k) * dm).astype(BF16)
        return a, (q * qd).astype(BF16), (k * kd).astype(BF16)

    def outputs(r0, parts, s):
        a, q_in, k_st = parts
        rows = slice(r0, r0 + chunk)
        v = z_sc[rows, v_cols].astype(BF16)
        o = (jnp.dot(a, v, preferred_element_type=F32)
             + jnp.dot(q_in, s.astype(BF16), preferred_element_type=F32))
        o_ref[rows, :] = _head_norm_gate(o, gain, z_sc[rows, g_cols])
        return cd * s + lax.dot_general(k_st, v, TN_DIMS, preferred_element_type=F32)

    s = jnp.zeros((DK_R, DV_R), F32)
    pending = {}
    for step in range(-lag, n_chunks + lag):
        ahead = step + lag
        if 0 <= ahead < n_chunks and ahead % group == 0:
            project(ahead * chunk)
        if 0 <= step < n_chunks:
            pending[step] = scores(step * chunk)
        if 0 <= step - lag < n_chunks:
            s = outputs((step - lag) * chunk, pending.pop(step - lag), s)
    s_ref[...] = s


def _retention(h3, w_in, cos, sin, gain):
    b, t, d = h3.shape
    c = RET_CHUNK
    n_head = 2 * DK_R + 2 * DV_R
    w_block = lambda width, lo: pl.BlockSpec((None, d, width), lambda bi, h: (0, 0, lo // width + h))
    lg = np.log(1.0 - 2.0 ** (-5.0 - np.arange(H_R, dtype=np.float64)))
    idx = np.arange(c, dtype=np.float64)
    rel = idx[:, None] - idx[None, :]
    dm = np.where(rel >= 0, np.exp(lg[:, None, None] * np.maximum(rel, 0.0)), 0.0)
    qd = np.broadcast_to(np.exp(lg[:, None] * (idx + 1.0))[:, :, None], (H_R, c, DK_R))
    kd = np.broadcast_to(np.exp(lg[:, None] * (c - 1.0 - idx))[:, :, None], (H_R, c, DK_R))
    cd = np.broadcast_to(np.exp(lg * c)[:, None, None], (H_R, 1, DV_R))
    dm, qd, kd, cd = (jnp.asarray(a, F32) for a in (dm, qd, kd, cd))
    head_tab = lambda rows, cols: pl.BlockSpec((None, rows, cols), lambda bi, h: (h, 0, 0))
    return pl.pallas_call(
        functools.partial(_ret_kernel, chunk=c, group=RET_GROUP, lag=RET_LAG),
        grid=(b, H_R),
        in_specs=[
            pl.BlockSpec((None, t, d), lambda bi, h: (bi, 0, 0)),
            w_block(DK_R, 0), w_block(DK_R, QK_R), w_block(DV_R, 2 * QK_R), w_block(DV_R, 2 * QK_R + V_R),
            pl.BlockSpec((t, DK_R), lambda bi, h: (0, 0)),
            pl.BlockSpec((t, DK_R), lambda bi, h: (0, 0)),
            head_tab(c, c), head_tab(c, DK_R), head_tab(c, DK_R), head_tab(1, DV_R),
            pl.BlockSpec((1, DV_R), lambda bi, h: (0, h)),
        ],
        out_specs=[
            pl.BlockSpec((None, t, DV_R), lambda bi, h: (bi, 0, h)),
            pl.BlockSpec((None, None, DK_R, DV_R), lambda bi, h: (bi, h, 0, 0)),
        ],
        out_shape=[
            jax.ShapeDtypeStruct((b, t, V_R), F32),
            jax.ShapeDtypeStruct((b, H_R, DK_R, DV_R), F32),
        ],
        scratch_shapes=[pltpu.VMEM((d, n_head), BF16), pltpu.VMEM((t, n_head), F32)],
        compiler_params=_params(("parallel", "arbitrary")),
        name="retention",
    )(h3, w_in, w_in, w_in, w_in, cos, sin, dm, qd, kd, cd, gain)


def _lower_bound(lb_ref):
    a = lb_ref[...]
    a0, a1 = a[0:1, :], a[1:2, :]
    m = jnp.maximum(a0, a1)
    e0 = jnp.exp(a0 - m)
    e1 = jnp.exp(a1 - m)
    return e0 / (e0 + e1)


def _forget_gate(gf, lb):
    sg = jax.nn.sigmoid(gf)
    f = lb + (1.0 - lb) * sg
    return jnp.log(f), (1.0 - lb) * (1.0 - sg), f


def _cumsum_rows(tri, x):
    hi = x.astype(BF16)
    rest = x - hi.astype(F32)
    mid = rest.astype(BF16)
    lo = (rest - mid.astype(F32)).astype(BF16)
    y = jnp.dot(tri, jnp.concatenate([hi, mid, lo], axis=1), preferred_element_type=F32)
    n = x.shape[1]
    return y[:, :n] + y[:, n:2 * n] + y[:, 2 * n:]


def _hgrn_kernel(h_ref, wq_ref, wf_ref, wi_ref, wg_ref, lb_ref, gain_ref, o_ref, s_ref,
                 w_sc, z_sc, q_sc, b_sc, c_sc, *, chunk, sub, group, lag, heads):
    assert sub == 8 and chunk % (2 * sub) == 0
    g_rows = group * chunk
    n_chunks = h_ref.shape[0] // chunk
    assert n_chunks % group == 0 and lag >= 1
    lanes = lambda hh: slice(hh * DK_G, (hh + 1) * DK_G)
    cols = lambda hh, role: lanes(4 * hh + role)
    for hh in range(heads):
        for role, w_ref in enumerate((wq_ref, wf_ref, wi_ref, wg_ref)):
            w_sc[:, cols(hh, role)] = w_ref[:, lanes(hh)].astype(BF16)

    lb = _lower_bound(lb_ref)
    half_scale = 0.5 * (1.0 - lb)
    f_mid = lb + half_scale
    gain = gain_ref[...]
    r64 = lax.broadcasted_iota(jnp.int32, (chunk, chunk), 0)
    c64 = lax.broadcasted_iota(jnp.int32, (chunk, chunk), 1)
    tri = jnp.where(r64 >= c64, 1.0, 0.0).astype(BF16)
    same_sub = (r64 // sub) == (c64 // sub)
    place = [(c64[:sub] % sub == s) & (r64[:sub] >= s) for s in range(sub)]
    zrows = lambda n: jnp.zeros((n, DK_G), F32)

    def project(r0):
        rows = slice(r0, r0 + g_rows)
        z_sc[rows, :] = jnp.dot(h_ref[rows, :], w_sc[...], preferred_element_type=F32)

    def gates(hh, r0):
        rows = slice(r0, r0 + chunk)
        th_scaled = half_scale[:, lanes(hh)] * jnp.tanh(0.5 * z_sc[rows, cols(hh, 1)])
        b_sc[rows, lanes(hh)] = _cumsum_rows(tri, jnp.log2(f_mid[:, lanes(hh)] + th_scaled))
        c_sc[rows, lanes(hh)] = b_sc[rows, lanes(hh)] - jnp.log2(half_scale[:, lanes(hh)] - th_scaled)
        q_sc[rows, lanes(hh)] = _silu(z_sc[rows, cols(hh, 0)])

    def diagonal(hh, r0):
        blocks = []
        for o in range(r0, r0 + chunk, sub):
            q, b = q_sc[o:o + sub, lanes(hh)], b_sc[o:o + sub, lanes(hh)]
            acc = jnp.zeros((sub, chunk), F32)
            for s in range(sub):
                p = jnp.exp2(b - c_sc[o + s:o + s + 1, lanes(hh)]) * q
                acc = jnp.where(place[s], jnp.sum(p, axis=-1, keepdims=True), acc)
            blocks.append(acc)
        return jnp.where(same_sub, jnp.concatenate(blocks, axis=0), 0.0)

    def below(hh, r0, w):
        qs, ks = [], []
        for mid in range(r0 + w, r0 + chunk, 2 * w):
            ref_b = b_sc[mid - 1:mid, lanes(hh)]
            qs += [zrows(w), q_sc[mid:mid + w, lanes(hh)] * jnp.exp2(b_sc[mid:mid + w, lanes(hh)] - ref_b)]
            ks += [jnp.exp2(ref_b - c_sc[mid - w:mid, lanes(hh)]), zrows(w)]
        scores = _dot_nt(jnp.concatenate(qs, axis=0), jnp.concatenate(ks, axis=0))
        return jnp.where(c64 // w == r64 // w - 1, scores, 0.0)

    def scores(hh, r0):
        a = diagonal(hh, r0)
        w = chunk // 2
        while w >= sub:
            a = a + below(hh, r0, w)
            w //= 2
        return a

    def outputs(hh, r0, a, st):
        rows = slice(r0, r0 + chunk)
        v = z_sc[rows, cols(hh, 2)].astype(BF16)
        b_last = b_sc[rows.stop - 1:rows.stop, lanes(hh)]
        o = _dot(a, v) + _dot_nt(q_sc[rows, lanes(hh)] * jnp.exp2(b_sc[rows, lanes(hh)]), st)
        o_ref[rows, lanes(hh)] = _head_norm_gate(o, gain[:, lanes(hh)], z_sc[rows, cols(hh, 3)])
        return st * jnp.exp2(b_last) + _dot_tn(v, jnp.exp2(b_last - c_sc[rows, lanes(hh)]))

    st = [jnp.zeros((DV_G, DK_G), F32) for _ in range(heads)]
    pending = {}
    for step in range(-lag, n_chunks + 2 * lag):
        ahead = step + lag
        if 0 <= ahead < n_chunks and ahead % group == 0:
            project(ahead * chunk)
        for hh in range(heads):
            if 0 <= step < n_chunks:
                gates(hh, step * chunk)
            if 0 <= step - lag < n_chunks:
                pending[hh, step - lag] = scores(hh, (step - lag) * chunk)
            if 0 <= step - 2 * lag < n_chunks:
                st[hh] = outputs(hh, (step - 2 * lag) * chunk, pending.pop((hh, step - 2 * lag)), st[hh])
    for hh in range(heads):
        s_ref[hh] = st[hh].T


def _hgrn(h3, w_in, hg_lb, gain):
    b, t, d = h3.shape
    c, sub, heads = HG_CHUNK, HG_SUB, HG_HEADS
    wide = heads * DK_G
    first = (2 * QK_R + 2 * V_R) // wide
    w_block = lambda role: pl.BlockSpec((None, d, wide), lambda bi, h: (0, 0, first + role * (H_G // heads) + h))
    return pl.pallas_call(
        functools.partial(_hgrn_kernel, chunk=c, sub=sub, group=HG_GROUP, lag=HG_LAG, heads=heads),
        grid=(b, H_G // heads),
        in_specs=[
            pl.BlockSpec((None, t, d), lambda bi, h: (bi, 0, 0)),
            w_block(0), w_block(1), w_block(2), w_block(3),
            pl.BlockSpec((2, wide), lambda bi, h: (0, h)),
            pl.BlockSpec((1, wide), lambda bi, h: (0, h)),
        ],
        out_specs=[
            pl.BlockSpec((None, t, wide), lambda bi, h: (bi, 0, h)),
            pl.BlockSpec((None, heads, DK_G, DV_G), lambda bi, h: (bi, h, 0, 0)),
        ],
        out_shape=[
            jax.ShapeDtypeStruct((b, t, V_G), F32),
            jax.ShapeDtypeStruct((b, H_G, DK_G, DV_G), F32),
        ],
        scratch_shapes=[
            pltpu.VMEM((d, 4 * wide), BF16),
            pltpu.VMEM((t, 4 * wide), F32),
            pltpu.VMEM((t, wide), F32),
            pltpu.VMEM((t, wide), F32),
            pltpu.VMEM((t, wide), F32),
        ],
        compiler_params=_params(("parallel", "arbitrary")),
        name="hgrn",
    )(h3, w_in, w_in, w_in, w_in, hg_lb, gain)


def _step_kernel(z_ref, cos_ref, sin_ref, lb_ref, rgain_ref, ggain_ref, sr0_ref, sg0_ref,
                 or_ref, og_ref, sr_ref, sg_ref, *, bb):
    first = lambda a: jnp.where(lax.broadcasted_iota(jnp.int32, a.shape, 0) == 0, a, 0.0)
    cos = cos_ref[...]
    sin = sin_ref[...]
    lb_all = _lower_bound(lb_ref)
    row8 = lax.broadcasted_iota(jnp.int32, (8, DK_G), 0)
    ones3 = jnp.where(row8 < 3, 1.0, 0.0)
    g0 = 2 * QK_R + 2 * V_R
    work = []
    for i in range(bb):
        z8 = lambda lo, n: jnp.broadcast_to(z_ref[i, :, lo:lo + n], (8, n))
        for h in range(H_R):
            q = _rotate(z8(h * DK_R, DK_R), cos, sin)
            k = _rotate(z8(QK_R + h * DK_R, DK_R), cos, sin) * (DK_R ** -0.5)
            v = z8(2 * QK_R + h * DV_R, DV_R)
            lanes = slice(h * DV_R, (h + 1) * DV_R)
            gate = z_ref[i, :, 2 * QK_R + V_R + h * DV_R:2 * QK_R + V_R + (h + 1) * DV_R]
            work.append((sr0_ref, sr_ref, or_ref, rgain_ref, i, h, lanes, q, gate,
                         1.0 - 2.0 ** (-5.0 - h), _dot_tn(first(k), v)))
        for h in range(H_G):
            lanes = slice(h * DK_G, (h + 1) * DK_G)
            q = _silu(z8(g0 + h * DK_G, DK_G))
            _, k, f = _forget_gate(z8(g0 + F_G + h * DK_G, DK_G), lb_all[:, lanes])
            v = z8(g0 + 2 * F_G + h * DV_G, DV_G)
            gate = z_ref[i, :, g0 + 2 * F_G + V_G + h * DV_G:g0 + 2 * F_G + V_G + (h + 1) * DV_G]
            f_hi = f.astype(BF16).astype(F32)
            f_mid = (f - f_hi).astype(BF16).astype(F32)
            f_lo = f - f_hi - f_mid
            lhs = jnp.where(row8 == 0, f_hi, jnp.where(row8 == 1, f_mid, jnp.where(row8 == 2, f_lo,
                            jnp.where(row8 == 3, k, 0.0))))
            rhs = jnp.concatenate([ones3, jnp.where(row8 == 3, v, 0.0)], axis=1)
            both = _dot_tn(lhs, rhs)
            work.append((sg0_ref, sg_ref, og_ref, ggain_ref, i, h, lanes, q, gate,
                         both[:, :DV_G], both[:, DV_G:]))
    outs = []
    for s0_ref, s_ref, _, _, i, h, _, q, _, decay, kv in work:
        s_new = decay * s0_ref[i, h] + kv
        s_ref[i, h] = s_new
        outs.append(_dot(q, s_new)[0:1, :])
    for (_, _, o_ref, gain_ref, i, _, lanes, _, gate, _, _), o in zip(work, outs):
        o_ref[i, :, lanes] = _head_norm_gate(o, gain_ref[:, lanes], gate)


def _step(z, cos, sin, hg_lb, rgain, ggain, sr0, sg0, bb):
    n, d_in = z.shape
    z3 = z.reshape(n, 1, d_in)
    row = lambda w: pl.BlockSpec((bb, 1, w), lambda i: (i, 0, 0))
    const = lambda a: pl.BlockSpec(a.shape, lambda i: (0,) * a.ndim)
    sr_spec = pl.BlockSpec((bb, H_R, DK_R, DV_R), lambda i: (i, 0, 0, 0))
    sg_spec = pl.BlockSpec((bb, H_G, DK_G, DV_G), lambda i: (i, 0, 0, 0))
    return pl.pallas_call(
        functools.partial(_step_kernel, bb=bb),
        grid=(n // bb,),
        in_specs=[row(d_in), const(cos), const(sin), const(hg_lb), const(rgain), const(ggain),
                  sr_spec, sg_spec],
        out_specs=[row(V_R), row(V_G), sr_spec, sg_spec],
        out_shape=[
            jax.ShapeDtypeStruct((n, 1, V_R), F32),
            jax.ShapeDtypeStruct((n, 1, V_G), F32),
            jax.ShapeDtypeStruct(sr0.shape, F32),
            jax.ShapeDtypeStruct(sg0.shape, F32),
        ],
        compiler_params=_params(("parallel",)),
        name="step",
    )(z3, cos, sin, hg_lb, rgain, ggain, sr0, sg0)


def _post_kernel(x_ref, h_ref, gr_ref, gg_ref, p_ref, wa_ref, wur_ref, wug_ref, wo_ref,
                 nm_ref, w1_ref, w2_ref, np_ref, wg_ref, wp_ref, nf_ref, y_ref):
    d = x_ref.shape[1]
    merge = _sigmoid(jnp.dot(h_ref[...], wa_ref[...], preferred_element_type=F32))
    u_r = jnp.dot(gr_ref[...].astype(BF16), wur_ref[...], preferred_element_type=F32)
    u_g = jnp.dot(gg_ref[...].astype(BF16), wug_ref[...], preferred_element_type=F32)
    m = merge[:, :d] * u_r + merge[:, d:] * u_g
    r = x_ref[...] + jnp.dot(m.astype(BF16), wo_ref[...], preferred_element_type=F32)
    hm = _rms(r, nm_ref[...]).astype(BF16)
    a = jnp.maximum(jnp.dot(hm, w1_ref[...], preferred_element_type=F32), 0.0)
    r = r + jnp.dot((a * a).astype(BF16), w2_ref[...], preferred_element_type=F32)
    hp = _rms(r, np_ref[...]).astype(BF16)
    gate = _sigmoid(jnp.dot(hp, wg_ref[...], preferred_element_type=F32))
    r = r + gate * jnp.dot(p_ref[...].astype(BF16), wp_ref[...], preferred_element_type=F32)
    y_ref[...] = _rms(r, nf_ref[...])


def _post(x, h, g_r, g_g, p, wa, wur, wug, wo, nm, w1, w2, npl, wg, wp, nf, tm):
    n, d = x.shape
    rows = lambda w: pl.BlockSpec((tm, w), lambda i: (i, 0))
    const = lambda a: pl.BlockSpec(a.shape, lambda i: (0, 0), pipeline_mode=pl.Buffered(1))
    return pl.pallas_call(
        _post_kernel,
        grid=(n // tm,),
        in_specs=[rows(d), rows(d), rows(d), rows(d), rows(p.shape[1]),
                  const(wa), const(wur), const(wug), const(wo), const(nm), const(w1), const(w2),
                  const(npl), const(wg), const(wp), const(nf)],
        out_specs=rows(d),
        out_shape=jax.ShapeDtypeStruct((n, d), F32),
        compiler_params=_params(("parallel",)),
        name="post",
    )(x, h, g_r, g_g, p, wa, wur, wug, wo, nm, w1, w2, npl, wg, wp, nf)


def _rope_tables(pos):
    half = DK_R // 2
    inv = ROPE_BASE ** (-jnp.arange(half, dtype=F32) / half)
    ang = pos[:, None] * inv[None, :]
    cos, sin = jnp.cos(ang), jnp.sin(ang)
    return jnp.concatenate([cos, cos], axis=-1), jnp.concatenate([-sin, sin], axis=-1)


def kernel(x_prompt, x_sample, state_ret, state_hgrn, p_prompt, p_sample, norm_mix_g, w_in,
           ret_norm_g, hg_norm_g, hg_lb, w_up_ret, w_up_hg, w_o, norm_mlp_g, w_ff1, w_ff2,
           norm_ple_g, w_ple_gate, w_ple_proj, norm_final_g):
    b, t, d = x_prompt.shape
    nb, td, _ = x_sample.shape
    assert w_in.shape[0] == 1 and td == 1 and hg_lb.shape[0] == 2
    bf = lambda w: w[0].astype(BF16)
    row = lambda g: g.reshape(1, -1)
    n_mix = 2 * QK_R + 2 * V_R + 2 * F_G + 2 * V_G
    post_w = (w_in[0, :, n_mix:].astype(BF16), bf(w_up_ret), bf(w_up_hg), bf(w_o), row(norm_mlp_g[0]),
              bf(w_ff1), bf(w_ff2), row(norm_ple_g[0]), bf(w_ple_gate), bf(w_ple_proj), row(norm_final_g))
    g_mix = row(norm_mix_g[0])
    rgain = row(ret_norm_g[0])
    ggain = row(hg_norm_g[0])

    xp = x_prompt.reshape(b * t, d)
    hp = _norm(xp, g_mix, tm=1024)
    hp3 = hp.reshape(b, t, d)
    cos_p, sin_p = _rope_tables(jnp.arange(t, dtype=F32))
    gr_p, ret_p = _retention(hp3, w_in, cos_p, sin_p, rgain)
    gg_p, hg_p = _hgrn(hp3, w_in, hg_lb, ggain)
    y_p = _post(xp, hp, gr_p.reshape(b * t, -1), gg_p.reshape(b * t, -1),
                p_prompt[0].reshape(b * t, -1), *post_w, tm=256)

    xs = x_sample.reshape(nb, d)
    hs = _norm(xs, g_mix, tm=nb)
    zs = _proj(hs, w_in, tm=nb, tn=1024)
    cos_s, sin_s = _rope_tables(PAST_LEN + jnp.arange(td, dtype=F32))
    gr_s, gg_s, ret_s, hg_s = _step(zs, cos_s, sin_s, hg_lb, rgain, ggain,
                                    state_ret[0], state_hgrn[0], bb=4)
    y_s = _post(xs, hs, gr_s.reshape(nb, -1), gg_s.reshape(nb, -1),
                p_sample[0].reshape(nb, -1), *post_w, tm=nb)

    return (y_p.reshape(b, t, d), y_s.reshape(nb, td, d), ret_p[None], hg_p[None],
            ret_s[None], hg_s[None])
```

```python
import functools

import jax
import jax.numpy as jnp
import numpy as np
from jax import lax
from jax.experimental import pallas as pl
from jax.experimental.pallas import tpu as pltpu

F32 = jnp.float32
BF16 = jnp.bfloat16

PAST_LEN = 16384
ROPE_BASE = 10000.0
EPS = 1e-6

H_R, DK_R, DV_R = 4, 128, 256
H_G, DK_G, DV_G = 8, 128, 128
QK_R = H_R * DK_R
V_R = H_R * DV_R
F_G = H_G * DK_G
V_G = H_G * DV_G

V7X_VMEM_BYTES = 64 * 1024 * 1024
VMEM_LIMIT = V7X_VMEM_BYTES - 8 * 1024 * 1024

RET_CHUNK = 128
RET_GROUP = 2
RET_LAG = 1
HG_CHUNK = 64
HG_SUB = 8
HG_GROUP = 4
HG_LAG = 2
HG_HEADS = 2

NT_DIMS = (((1,), (1,)), ((), ()))
TN_DIMS = (((0,), (0,)), ((), ()))


def _dot(a, b):
    return jnp.dot(a.astype(BF16), b.astype(BF16), preferred_element_type=F32)


def _dot_nt(a, b):
    return lax.dot_general(a.astype(BF16), b.astype(BF16), NT_DIMS, preferred_element_type=F32)


def _dot_tn(a, b):
    return lax.dot_general(a.astype(BF16), b.astype(BF16), TN_DIMS, preferred_element_type=F32)


def _rms(x, g):
    return x * lax.rsqrt(jnp.mean(x * x, axis=-1, keepdims=True) + EPS) * g


def _sigmoid(x):
    return 0.5 * jnp.tanh(0.5 * x) + 0.5


def _silu(x):
    half = 0.5 * x
    return half + half * jnp.tanh(half)


def _params(sem):
    return pltpu.CompilerParams(dimension_semantics=sem, vmem_limit_bytes=VMEM_LIMIT)


def _norm_kernel(x_ref, g_ref, h_ref):
    h_ref[...] = _rms(x_ref[...], g_ref[...]).astype(BF16)


def _norm(x, g, tm):
    n, d = x.shape
    return pl.pallas_call(
        _norm_kernel,
        grid=(n // tm,),
        in_specs=[pl.BlockSpec((tm, d), lambda i: (i, 0)), pl.BlockSpec((1, d), lambda i: (0, 0))],
        out_specs=pl.BlockSpec((tm, d), lambda i: (i, 0)),
        out_shape=jax.ShapeDtypeStruct((n, d), BF16),
        compiler_params=_params(("parallel",)),
        name="norm",
    )(x, g)


def _proj_kernel(h_ref, w_ref, z_ref):
    z_ref[...] = jnp.dot(h_ref[...], w_ref[...].astype(BF16), preferred_element_type=F32)


def _proj(h, w_in, tm, tn):
    n, d = h.shape
    d_out = w_in.shape[2]
    return pl.pallas_call(
        _proj_kernel,
        grid=(n // tm, d_out // tn),
        in_specs=[
            pl.BlockSpec((tm, d), lambda i, j: (i, 0)),
            pl.BlockSpec((None, d, tn), lambda i, j: (0, 0, j)),
        ],
        out_specs=pl.BlockSpec((tm, tn), lambda i, j: (i, j)),
        out_shape=jax.ShapeDtypeStruct((n, d_out), F32),
        compiler_params=_params(("parallel", "arbitrary")),
        name="proj",
    )(h, w_in)


def _rotate(x, cos, sin_signed):
    return x * cos + pltpu.roll(x, DK_R // 2, 1) * sin_signed


def _head_norm_gate(o, gain, gate):
    return o * lax.rsqrt(jnp.mean(o * o, axis=-1, keepdims=True) + EPS) * gain * _silu(gate)


def _ret_pipeline(h_ref, wq_ref, wk_ref, wv_ref, wg_ref, cos_ref, sin_ref, dm_ref, qd_ref, kd_ref,
                  cd_ref, gain_ref, o_ref, s_ref, w_sc, z_sc, *, chunk, group, lag):
    n_chunks = h_ref.shape[0] // chunk
    assert n_chunks % group == 0 and lag >= 1
    g_rows = group * chunk
    q_cols = slice(0, DK_R)
    k_cols = slice(DK_R, 2 * DK_R)
    v_cols = slice(2 * DK_R, 2 * DK_R + DV_R)
    g_cols = slice(2 * DK_R + DV_R, 2 * DK_R + 2 * DV_R)
    for cols, w_ref in ((q_cols, wq_ref), (k_cols, wk_ref), (v_cols, wv_ref), (g_cols, wg_ref)):
        w_sc[:, cols] = w_ref[...].astype(BF16)
    dm = dm_ref[...]
    qd = qd_ref[...]
    kd = kd_ref[...]
    cd = cd_ref[...]
    gain = gain_ref[...]

    def project(r0):
        rows = slice(r0, r0 + g_rows)
        z_sc[rows, :] = jnp.dot(h_ref[rows, :], w_sc[...], preferred_element_type=F32)

    def scores(r0):
        rows = slice(r0, r0 + chunk)
        cos = cos_ref[rows, :]
        sin = sin_ref[rows, :]
        q = _rotate(z_sc[rows, q_cols], cos, sin)
        k = _rotate(z_sc[rows, k_cols], cos, sin) * (DK_R ** -0.5)
        a = (_dot_nt(q, k) * dm).astype(BF16)
        return a, (q * qd).astype(BF16), (k * kd).astype(BF16)

    def outputs(r0, parts, s):
        a, q_in, k_st = parts
        rows = slice(r0, r0 + chunk)
        v = z_sc[rows, v_cols].astype(BF16)
        o = (jnp.dot(a, v, preferred_element_type=F32)
             + jnp.dot(q_in, s.astype(BF16), preferred_element_type=F32))
        o_ref[rows, :] = _head_norm_gate(o, gain, z_sc[rows, g_cols])
        return cd * s + lax.dot_general(k_st, v, TN_DIMS, preferred_element_type=F32)

    carry = {"s": jnp.zeros((DK_R, DV_R), F32)}
    pending = {}

    def do_scores(c):
        pending[c] = scores(c * chunk)

    def do_outputs(c):
        carry["s"] = outputs(c * chunk, pending.pop(c), carry["s"])

    work = []
    for step in range(-lag, n_chunks + lag):
        if 0 <= step + lag < n_chunks and (step + lag) % group == 0:
            work.append(functools.partial(project, (step + lag) * chunk))
        if 0 <= step < n_chunks:
            work.append(functools.partial(do_scores, step))
        if 0 <= step - lag < n_chunks:
            work.append(functools.partial(do_outputs, step - lag))

    def finish():
        s_ref[...] = carry["s"]

    return work, finish


def _retention_tables(c):
    lg = np.log(1.0 - 2.0 ** (-5.0 - np.arange(H_R, dtype=np.float64)))
    idx = np.arange(c, dtype=np.float64)
    rel = idx[:, None] - idx[None, :]
    dm = np.where(rel >= 0, np.exp(lg[:, None, None] * np.maximum(rel, 0.0)), 0.0)
    qd = np.broadcast_to(np.exp(lg[:, None] * (idx + 1.0))[:, :, None], (H_R, c, DK_R))
    kd = np.broadcast_to(np.exp(lg[:, None] * (c - 1.0 - idx))[:, :, None], (H_R, c, DK_R))
    cd = np.broadcast_to(np.exp(lg * c)[:, None, None], (H_R, 1, DV_R))
    return tuple(jnp.asarray(a, F32) for a in (dm, qd, kd, cd))


def _lower_bound(lb_ref):
    a = lb_ref[...]
    a0, a1 = a[0:1, :], a[1:2, :]
    m = jnp.maximum(a0, a1)
    e0 = jnp.exp(a0 - m)
    e1 = jnp.exp(a1 - m)
    return e0 / (e0 + e1)


def _forget_gate(gf, lb):
    sg = jax.nn.sigmoid(gf)
    f = lb + (1.0 - lb) * sg
    return jnp.log(f), (1.0 - lb) * (1.0 - sg), f


def _cumsum_rows(tri, x):
    hi = x.astype(BF16)
    rest = x - hi.astype(F32)
    mid = rest.astype(BF16)
    lo = (rest - mid.astype(F32)).astype(BF16)
    y = jnp.dot(tri, jnp.concatenate([hi, mid, lo], axis=1), preferred_element_type=F32)
    n = x.shape[1]
    return y[:, :n] + y[:, n:2 * n] + y[:, 2 * n:]


def _hgrn_pipeline(h_ref, wq_ref, wf_ref, wi_ref, wg_ref, lb_ref, gain_ref, o_ref, s_ref,
                   w_sc, z_sc, q_sc, b_sc, c_sc, *, chunk, sub, group, lag, heads):
    assert sub == 8 and chunk % (2 * sub) == 0
    g_rows = group * chunk
    n_chunks = h_ref.shape[0] // chunk
    assert n_chunks % group == 0 and lag >= 1
    lanes = lambda hh: slice(hh * DK_G, (hh + 1) * DK_G)
    cols = lambda hh, role: lanes(4 * hh + role)
    for hh in range(heads):
        for role, w_ref in enumerate((wq_ref, wf_ref, wi_ref, wg_ref)):
            w_sc[:, cols(hh, role)] = w_ref[:, lanes(hh)].astype(BF16)

    lb = _lower_bound(lb_ref)
    half_scale = 0.5 * (1.0 - lb)
    f_mid = lb + half_scale
    gain = gain_ref[...]
    r64 = lax.broadcasted_iota(jnp.int32, (chunk, chunk), 0)
    c64 = lax.broadcasted_iota(jnp.int32, (chunk, chunk), 1)
    tri = jnp.where(r64 >= c64, 1.0, 0.0).astype(BF16)
    same_sub = (r64 // sub) == (c64 // sub)
    place = [(c64[:sub] % sub == s) & (r64[:sub] >= s) for s in range(sub)]
    zrows = lambda n: jnp.zeros((n, DK_G), F32)

    def project(r0):
        rows = slice(r0, r0 + g_rows)
        z_sc[rows, :] = jnp.dot(h_ref[rows, :], w_sc[...], preferred_element_type=F32)

    def gates(hh, r0):
        rows = slice(r0, r0 + chunk)
        th_scaled = half_scale[:, lanes(hh)] * jnp.tanh(0.5 * z_sc[rows, cols(hh, 1)])
        b_sc[rows, lanes(hh)] = _cumsum_rows(tri, jnp.log2(f_mid[:, lanes(hh)] + th_scaled))
        c_sc[rows, lanes(hh)] = b_sc[rows, lanes(hh)] - jnp.log2(half_scale[:, lanes(hh)] - th_scaled)
        q_sc[rows, lanes(hh)] = _silu(z_sc[rows, cols(hh, 0)])

    def diagonal(hh, r0):
        blocks = []
        for o in range(r0, r0 + chunk, sub):
            q, b = q_sc[o:o + sub, lanes(hh)], b_sc[o:o + sub, lanes(hh)]
            acc = jnp.zeros((sub, chunk), F32)
            for s in range(sub):
                p = jnp.exp2(b - c_sc[o + s:o + s + 1, lanes(hh)]) * q
                acc = jnp.where(place[s], jnp.sum(p, axis=-1, keepdims=True), acc)
            blocks.append(acc)
        return jnp.where(same_sub, jnp.concatenate(blocks, axis=0), 0.0)

    def below(hh, r0, w):
        qs, ks = [], []
        for mid in range(r0 + w, r0 + chunk, 2 * w):
            ref_b = b_sc[mid - 1:mid, lanes(hh)]
            qs += [zrows(w), q_sc[mid:mid + w, lanes(hh)] * jnp.exp2(b_sc[mid:mid + w, lanes(hh)] - ref_b)]
            ks += [jnp.exp2(ref_b - c_sc[mid - w:mid, lanes(hh)]), zrows(w)]
        scores = _dot_nt(jnp.concatenate(qs, axis=0), jnp.concatenate(ks, axis=0))
        return jnp.where(c64 // w == r64 // w - 1, scores, 0.0)

    def scores(hh, r0):
        a = diagonal(hh, r0)
        w = chunk // 2
        while w >= sub:
            a = a + below(hh, r0, w)
            w //= 2
        return a

    def outputs(hh, r0, a, st):
        rows = slice(r0, r0 + chunk)
        v = z_sc[rows, cols(hh, 2)].astype(BF16)
        b_last = b_sc[rows.stop - 1:rows.stop, lanes(hh)]
        o = _dot(a, v) + _dot_nt(q_sc[rows, lanes(hh)] * jnp.exp2(b_sc[rows, lanes(hh)]), st)
        o_ref[rows, lanes(hh)] = _head_norm_gate(o, gain[:, lanes(hh)], z_sc[rows, cols(hh, 3)])
        return st * jnp.exp2(b_last) + _dot_tn(v, jnp.exp2(b_last - c_sc[rows, lanes(hh)]))

    st = [jnp.zeros((DV_G, DK_G), F32) for _ in range(heads)]
    pending = {}

    def do_scores(hh, c):
        pending[hh, c] = scores(hh, c * chunk)

    def do_outputs(hh, c):
        st[hh] = outputs(hh, c * chunk, pending.pop((hh, c)), st[hh])

    work = []
    for step in range(-lag, n_chunks + 2 * lag):
        if 0 <= step + lag < n_chunks and (step + lag) % group == 0:
            work.append(functools.partial(project, (step + lag) * chunk))
        for hh in range(heads):
            if 0 <= step < n_chunks:
                work.append(functools.partial(gates, hh, step * chunk))
            if 0 <= step - lag < n_chunks:
                work.append(functools.partial(do_scores, hh, step - lag))
            if 0 <= step - 2 * lag < n_chunks:
                work.append(functools.partial(do_outputs, hh, step - 2 * lag))

    def finish():
        for hh in range(heads):
            s_ref[hh] = st[hh].T

    return work, finish


N_RET_IN = 12
N_HG_IN = 7


def _mixer_kernel(*refs):
    h_ref = refs[0]
    n_in = 1 + (N_RET_IN - 1) + (N_HG_IN - 1)
    ret_in = refs[1:N_RET_IN]
    hg_in = refs[N_RET_IN:n_in]
    gr_ref, rs_ref, gg_ref, gs_ref = refs[n_in:n_in + 4]
    rw_sc, rz_sc, gw_sc, gz_sc, q_sc, b_sc, c_sc = refs[n_in + 4:]
    r_work, r_finish = _ret_pipeline(
        h_ref, *ret_in, gr_ref, rs_ref, rw_sc, rz_sc, chunk=RET_CHUNK, group=RET_GROUP, lag=RET_LAG)
    g_work, g_finish = _hgrn_pipeline(
        h_ref, *hg_in, gg_ref, gs_ref, gw_sc, gz_sc, q_sc, b_sc, c_sc,
        chunk=HG_CHUNK, sub=HG_SUB, group=HG_GROUP, lag=HG_LAG, heads=HG_HEADS)
    done = 0
    for i, item in enumerate(g_work):
        item()
        due = ((i + 1) * len(r_work)) // len(g_work)
        for r_item in r_work[done:due]:
            r_item()
        done = due
    r_finish()
    g_finish()


def _mixers(h3, w_in, cos, sin, rgain, hg_lb, ggain):
    b, t, d = h3.shape
    assert H_G // HG_HEADS == H_R
    n_head = 2 * DK_R + 2 * DV_R
    wide = HG_HEADS * DK_G
    first = (2 * QK_R + 2 * V_R) // wide
    once = dict(pipeline_mode=pl.Buffered(1))
    r_w = lambda width, lo: pl.BlockSpec((None, d, width), lambda bi, h: (0, 0, lo // width + h), **once)
    g_w = lambda role: pl.BlockSpec((None, d, wide), lambda bi, h: (0, 0, first + role * H_R + h), **once)
    head_tab = lambda rows, cols: pl.BlockSpec((None, rows, cols), lambda bi, h: (h, 0, 0))
    table = pl.BlockSpec((t, DK_R), lambda bi, h: (0, 0), **once)
    return pl.pallas_call(
        _mixer_kernel,
        grid=(b, H_R),
        in_specs=[
            pl.BlockSpec((None, t, d), lambda bi, h: (bi, 0, 0), **once),
            r_w(DK_R, 0), r_w(DK_R, QK_R), r_w(DV_R, 2 * QK_R), r_w(DV_R, 2 * QK_R + V_R),
            table, table,
            head_tab(RET_CHUNK, RET_CHUNK), head_tab(RET_CHUNK, DK_R), head_tab(RET_CHUNK, DK_R),
            head_tab(1, DV_R),
            pl.BlockSpec((1, DV_R), lambda bi, h: (0, h)),
            g_w(0), g_w(1), g_w(2), g_w(3),
            pl.BlockSpec((2, wide), lambda bi, h: (0, h)),
            pl.BlockSpec((1, wide), lambda bi, h: (0, h)),
        ],
        out_specs=[
            pl.BlockSpec((None, t, DV_R), lambda bi, h: (bi, 0, h)),
            pl.BlockSpec((None, None, DK_R, DV_R), lambda bi, h: (bi, h, 0, 0)),
            pl.BlockSpec((None, t, wide), lambda bi, h: (bi, 0, h)),
            pl.BlockSpec((None, HG_HEADS, DK_G, DV_G), lambda bi, h: (bi, h, 0, 0)),
        ],
        out_shape=[
            jax.ShapeDtypeStruct((b, t, V_R), F32),
            jax.ShapeDtypeStruct((b, H_R, DK_R, DV_R), F32),
            jax.ShapeDtypeStruct((b, t, V_G), F32),
            jax.ShapeDtypeStruct((b, H_G, DK_G, DV_G), F32),
        ],
        scratch_shapes=[
            pltpu.VMEM((d, n_head), BF16),
            pltpu.VMEM((t, n_head), F32),
            pltpu.VMEM((d, 4 * wide), BF16),
            pltpu.VMEM((t, 4 * wide), F32),
            pltpu.VMEM((t, wide), F32),
            pltpu.VMEM((t, wide), F32),
            pltpu.VMEM((t, wide), F32),
        ],
        compiler_params=_params(("parallel", "arbitrary")),
        name="mixers",
    )(h3, w_in, w_in, w_in, w_in, cos, sin, *_retention_tables(RET_CHUNK), rgain,
      w_in, w_in, w_in, w_in, hg_lb, ggain)


def _step_kernel(z_ref, cos_ref, sin_ref, lb_ref, rgain_ref, ggain_ref, sr0_ref, sg0_ref,
                 or_ref, og_ref, sr_ref, sg_ref, *, bb):
    first = lambda a: jnp.where(lax.broadcasted_iota(jnp.int32, a.shape, 0) == 0, a, 0.0)
    cos = cos_ref[...]
    sin = sin_ref[...]
    lb_all = _lower_bound(lb_ref)
    row8 = lax.broadcasted_iota(jnp.int32, (8, DK_G), 0)
    ones3 = jnp.where(row8 < 3, 1.0, 0.0)
    g0 = 2 * QK_R + 2 * V_R
    work = []
    for i in range(bb):
        z8 = lambda lo, n: jnp.broadcast_to(z_ref[i, :, lo:lo + n], (8, n))
        for h in range(H_R):
            q = _rotate(z8(h * DK_R, DK_R), cos, sin)
            k = _rotate(z8(QK_R + h * DK_R, DK_R), cos, sin) * (DK_R ** -0.5)
            v = z8(2 * QK_R + h * DV_R, DV_R)
            lanes = slice(h * DV_R, (h + 1) * DV_R)
            gate = z_ref[i, :, 2 * QK_R + V_R + h * DV_R:2 * QK_R + V_R + (h + 1) * DV_R]
            work.append((sr0_ref, sr_ref, or_ref, rgain_ref, i, h, lanes, q, gate,
                         1.0 - 2.0 ** (-5.0 - h), _dot_tn(first(k), v)))
        for h in range(H_G):
            lanes = slice(h * DK_G, (h + 1) * DK_G)
            q = _silu(z8(g0 + h * DK_G, DK_G))
            _, k, f = _forget_gate(z8(g0 + F_G + h * DK_G, DK_G), lb_all[:, lanes])
            v = z8(g0 + 2 * F_G + h * DV_G, DV_G)
            gate = z_ref[i, :, g0 + 2 * F_G + V_G + h * DV_G:g0 + 2 * F_G + V_G + (h + 1) * DV_G]
            f_hi = f.astype(BF16).astype(F32)
            f_mid = (f - f_hi).astype(BF16).astype(F32)
            f_lo = f - f_hi - f_mid
            lhs = jnp.where(row8 == 0, f_hi, jnp.where(row8 == 1, f_mid, jnp.where(row8 == 2, f_lo,
                            jnp.where(row8 == 3, k, 0.0))))
            rhs = jnp.concatenate([ones3, jnp.where(row8 == 3, v, 0.0)], axis=1)
            both = _dot_tn(lhs, rhs)
            work.append((sg0_ref, sg_ref, og_ref, ggain_ref, i, h, lanes, q, gate,
                         both[:, :DV_G], both[:, DV_G:]))
    outs = []
    for s0_ref, s_ref, _, _, i, h, _, q, _, decay, kv in work:
        s_new = decay * s0_ref[i, h] + kv
        s_ref[i, h] = s_new
        outs.append(_dot(q, s_new)[0:1, :])
    for (_, _, o_ref, gain_ref, i, _, lanes, _, gate, _, _), o in zip(work, outs):
        o_ref[i, :, lanes] = _head_norm_gate(o, gain_ref[:, lanes], gate)


def _step(z, cos, sin, hg_lb, rgain, ggain, sr0, sg0, bb):
    n, d_in = z.shape
    z3 = z.reshape(n, 1, d_in)
    row = lambda w: pl.BlockSpec((bb, 1, w), lambda i: (i, 0, 0))
    const = lambda a: pl.BlockSpec(a.shape, lambda i: (0,) * a.ndim)
    sr_spec = pl.BlockSpec((bb, H_R, DK_R, DV_R), lambda i: (i, 0, 0, 0))
    sg_spec = pl.BlockSpec((bb, H_G, DK_G, DV_G), lambda i: (i, 0, 0, 0))
    return pl.pallas_call(
        functools.partial(_step_kernel, bb=bb),
        grid=(n // bb,),
        in_specs=[row(d_in), const(cos), const(sin), const(hg_lb), const(rgain), const(ggain),
                  sr_spec, sg_spec],
        out_specs=[row(V_R), row(V_G), sr_spec, sg_spec],
        out_shape=[
            jax.ShapeDtypeStruct((n, 1, V_R), F32),
            jax.ShapeDtypeStruct((n, 1, V_G), F32),
            jax.ShapeDtypeStruct(sr0.shape, F32),
            jax.ShapeDtypeStruct(sg0.shape, F32),
        ],
        compiler_params=_params(("parallel",)),
        name="step",
    )(z3, cos, sin, hg_lb, rgain, ggain, sr0, sg0)


def _post_kernel(x_ref, h_ref, gr_ref, gg_ref, p_ref, wa_ref, wur_ref, wug_ref, wo_ref,
                 nm_ref, w1_ref, w2_ref, np_ref, wg_ref, wp_ref, nf_ref, y_ref):
    d = x_ref.shape[1]
    merge = _sigmoid(jnp.dot(h_ref[...], wa_ref[...], preferred_element_type=F32))
    u_r = jnp.dot(gr_ref[...].astype(BF16), wur_ref[...], preferred_element_type=F32)
    u_g = jnp.dot(gg_ref[...].astype(BF16), wug_ref[...], preferred_element_type=F32)
    m = merge[:, :d] * u_r + merge[:, d:] * u_g
    r = x_ref[...] + jnp.dot(m.astype(BF16), wo_ref[...], preferred_element_type=F32)
    hm = _rms(r, nm_ref[...]).astype(BF16)
    a = jnp.maximum(jnp.dot(hm, w1_ref[...], preferred_element_type=F32), 0.0)
    r = r + jnp.dot((a * a).astype(BF16), w2_ref[...], preferred_element_type=F32)
    hp = _rms(r, np_ref[...]).astype(BF16)
    gate = _sigmoid(jnp.dot(hp, wg_ref[...], preferred_element_type=F32))
    r = r + gate * jnp.dot(p_ref[...].astype(BF16), wp_ref[...], preferred_element_type=F32)
    y_ref[...] = _rms(r, nf_ref[...])


def _post(x, h, g_r, g_g, p, wa, wur, wug, wo, nm, w1, w2, npl, wg, wp, nf, tm):
    n, d = x.shape
    rows = lambda w: pl.BlockSpec((tm, w), lambda i: (i, 0))
    const = lambda a: pl.BlockSpec(a.shape, lambda i: (0, 0), pipeline_mode=pl.Buffered(1))
    return pl.pallas_call(
        _post_kernel,
        grid=(n // tm,),
        in_specs=[rows(d), rows(d), rows(d), rows(d), rows(p.shape[1]),
                  const(wa), const(wur), const(wug), const(wo), const(nm), const(w1), const(w2),
                  const(npl), const(wg), const(wp), const(nf)],
        out_specs=rows(d),
        out_shape=jax.ShapeDtypeStruct((n, d), F32),
        compiler_params=_params(("parallel",)),
        name="post",
    )(x, h, g_r, g_g, p, wa, wur, wug, wo, nm, w1, w2, npl, wg, wp, nf)


def _rope_tables(pos):
    half = DK_R // 2
    inv = ROPE_BASE ** (-jnp.arange(half, dtype=F32) / half)
    ang = pos[:, None] * inv[None, :]
    cos, sin = jnp.cos(ang), jnp.sin(ang)
    return jnp.concatenate([cos, cos], axis=-1), jnp.concatenate([-sin, sin], axis=-1)


def kernel(x_prompt, x_sample, state_ret, state_hgrn, p_prompt, p_sample, norm_mix_g, w_in,
           ret_norm_g, hg_norm_g, hg_lb, w_up_ret, w_up_hg, w_o, norm_mlp_g, w_ff1, w_ff2,
           norm_ple_g, w_ple_gate, w_ple_proj, norm_final_g):
    b, t, d = x_prompt.shape
    nb, td, _ = x_sample.shape
    assert w_in.shape[0] == 1 and td == 1 and hg_lb.shape[0] == 2
    bf = lambda w: w[0].astype(BF16)
    row = lambda g: g.reshape(1, -1)
    n_mix = 2 * QK_R + 2 * V_R + 2 * F_G + 2 * V_G
    post_w = (w_in[0, :, n_mix:].astype(BF16), bf(w_up_ret), bf(w_up_hg), bf(w_o), row(norm_mlp_g[0]),
              bf(w_ff1), bf(w_ff2), row(norm_ple_g[0]), bf(w_ple_gate), bf(w_ple_proj), row(norm_final_g))
    g_mix = row(norm_mix_g[0])
    rgain = row(ret_norm_g[0])
    ggain = row(hg_norm_g[0])

    xp = x_prompt.reshape(b * t, d)
    hp = _norm(xp, g_mix, tm=1024)
    hp3 = hp.reshape(b, t, d)
    cos_p, sin_p = _rope_tables(jnp.arange(t, dtype=F32))
    gr_p, ret_p, gg_p, hg_p = _mixers(hp3, w_in, cos_p, sin_p, rgain, hg_lb, ggain)
    y_p = _post(xp, hp, gr_p.reshape(b * t, -1), gg_p.reshape(b * t, -1),
                p_prompt[0].reshape(b * t, -1), *post_w, tm=256)

    xs = x_sample.reshape(nb, d)
    hs = _norm(xs, g_mix, tm=nb)
    zs = _proj(hs, w_in, tm=nb, tn=1024)
    cos_s, sin_s = _rope_tables(PAST_LEN + jnp.arange(td, dtype=F32))
    gr_s, gg_s, ret_s, hg_s = _step(zs, cos_s, sin_s, hg_lb, rgain, ggain,
                                    state_ret[0], state_hgrn[0], bb=4)
    y_s = _post(xs, hs, gr_s.reshape(nb, -1), gg_s.reshape(nb, -1),
                p_sample[0].reshape(nb, -1), *post_w, tm=nb)

    return (y_p.reshape(b, t, d), y_s.reshape(nb, td, d), ret_p[None], hg_p[None],
            ret_s[None], hg_s[None])
```

```python
import functools

import jax
import jax.numpy as jnp
import numpy as np
from jax import lax
from jax.experimental import pallas as pl
from jax.experimental.pallas import tpu as pltpu

F32 = jnp.float32
BF16 = jnp.bfloat16

PAST_LEN = 16384
ROPE_BASE = 10000.0
EPS = 1e-6

H_R, DK_R, DV_R = 4, 128, 256
H_G, DK_G, DV_G = 8, 128, 128
QK_R = H_R * DK_R
V_R = H_R * DV_R
F_G = H_G * DK_G
V_G = H_G * DV_G

V7X_VMEM_BYTES = 64 * 1024 * 1024
VMEM_LIMIT = V7X_VMEM_BYTES - 8 * 1024 * 1024

NORM_ROWS = 256
RET_CHUNK = 128
RET_GROUP = 2
RET_LAG = 1
HG_CHUNK = 64
HG_SUB = 8
HG_GROUP = 4
HG_LAG = 2
HG_HEADS = 2

NT_DIMS = (((1,), (1,)), ((), ()))
TN_DIMS = (((0,), (0,)), ((), ()))


def _dot(a, b):
    return jnp.dot(a.astype(BF16), b.astype(BF16), preferred_element_type=F32)


def _dot_nt(a, b):
    return lax.dot_general(a.astype(BF16), b.astype(BF16), NT_DIMS, preferred_element_type=F32)


def _dot_tn(a, b):
    return lax.dot_general(a.astype(BF16), b.astype(BF16), TN_DIMS, preferred_element_type=F32)


def _rms(x, g):
    return x * lax.rsqrt(jnp.mean(x * x, axis=-1, keepdims=True) + EPS) * g


def _sigmoid(x):
    return 0.5 * jnp.tanh(0.5 * x) + 0.5


def _silu(x):
    half = 0.5 * x
    return half + half * jnp.tanh(half)


def _params(sem):
    return pltpu.CompilerParams(dimension_semantics=sem, vmem_limit_bytes=VMEM_LIMIT)


def _norm_kernel(x_ref, g_ref, h_ref):
    h_ref[...] = _rms(x_ref[...], g_ref[...]).astype(BF16)


def _norm(x, g, tm):
    n, d = x.shape
    return pl.pallas_call(
        _norm_kernel,
        grid=(n // tm,),
        in_specs=[pl.BlockSpec((tm, d), lambda i: (i, 0)), pl.BlockSpec((1, d), lambda i: (0, 0))],
        out_specs=pl.BlockSpec((tm, d), lambda i: (i, 0)),
        out_shape=jax.ShapeDtypeStruct((n, d), BF16),
        compiler_params=_params(("parallel",)),
        name="norm",
    )(x, g)


def _proj_kernel(h_ref, w_ref, z_ref):
    z_ref[...] = jnp.dot(h_ref[...], w_ref[...].astype(BF16), preferred_element_type=F32)


def _proj(h, w_in, tm, tn):
    n, d = h.shape
    d_out = w_in.shape[2]
    return pl.pallas_call(
        _proj_kernel,
        grid=(n // tm, d_out // tn),
        in_specs=[
            pl.BlockSpec((tm, d), lambda i, j: (i, 0)),
            pl.BlockSpec((None, d, tn), lambda i, j: (0, 0, j)),
        ],
        out_specs=pl.BlockSpec((tm, tn), lambda i, j: (i, j)),
        out_shape=jax.ShapeDtypeStruct((n, d_out), F32),
        compiler_params=_params(("parallel", "arbitrary")),
        name="proj",
    )(h, w_in)


def _rotate(x, cos, sin_signed):
    return x * cos + pltpu.roll(x, DK_R // 2, 1) * sin_signed


def _head_norm_gate(o, gain, gate):
    return o * lax.rsqrt(jnp.mean(o * o, axis=-1, keepdims=True) + EPS) * gain * _silu(gate)


def _ret_pipeline(h_ref, wq_ref, wk_ref, wv_ref, wg_ref, cos_ref, sin_ref, dm_ref, qd_ref, kd_ref,
                  cd_ref, gain_ref, o_ref, s_ref, w_sc, z_sc, *, chunk, group, lag):
    n_chunks = h_ref.shape[0] // chunk
    assert n_chunks % group == 0 and lag >= 1
    g_rows = group * chunk
    q_cols = slice(0, DK_R)
    k_cols = slice(DK_R, 2 * DK_R)
    v_cols = slice(2 * DK_R, 2 * DK_R + DV_R)
    g_cols = slice(2 * DK_R + DV_R, 2 * DK_R + 2 * DV_R)
    for cols, w_ref in ((q_cols, wq_ref), (k_cols, wk_ref), (v_cols, wv_ref), (g_cols, wg_ref)):
        w_sc[:, cols] = w_ref[...].astype(BF16)
    dm = dm_ref[...]
    qd = qd_ref[...]
    kd = kd_ref[...]
    cd = cd_ref[...]
    gain = gain_ref[...]

    def project(r0):
        rows = slice(r0, r0 + g_rows)
        z_sc[rows, :] = jnp.dot(h_ref[rows, :], w_sc[...], preferred_element_type=F32)

    def scores(r0):
        rows = slice(r0, r0 + chunk)
        cos = cos_ref[rows, :]
        sin = sin_ref[rows, :]
        q = _rotate(z_sc[rows, q_cols], cos, sin)
        k = _rotate(z_sc[rows, k_cols], cos, sin) * (DK_R ** -0.5)
        a = (_dot_nt(q, k) * dm).astype(BF16)
        return a, (q * qd).astype(BF16), (k * kd).astype(BF16)

    def outputs(r0, parts, s):
        a, q_in, k_st = parts
        rows = slice(r0, r0 + chunk)
        v = z_sc[rows, v_cols].astype(BF16)
        o = (jnp.dot(a, v, preferred_element_type=F32)
             + jnp.dot(q_in, s.astype(BF16), preferred_element_type=F32))
        o_ref[rows, :] = _head_norm_gate(o, gain, z_sc[rows, g_cols])
        return cd * s + lax.dot_general(k_st, v, TN_DIMS, preferred_element_type=F32)

    carry = {"s": jnp.zeros((DK_R, DV_R), F32)}
    pending = {}

    def do_scores(c):
        pending[c] = scores(c * chunk)

    def do_outputs(c):
        carry["s"] = outputs(c * chunk, pending.pop(c), carry["s"])

    work = []
    for step in range(-lag, n_chunks + lag):
        if 0 <= step + lag < n_chunks and (step + lag) % group == 0:
            work.append(functools.partial(project, (step + lag) * chunk))
        if 0 <= step < n_chunks:
            work.append(functools.partial(do_scores, step))
        if 0 <= step - lag < n_chunks:
            work.append(functools.partial(do_outputs, step - lag))

    def finish():
        s_ref[...] = carry["s"]

    return work, finish


def _retention_tables(c):
    lg = np.log(1.0 - 2.0 ** (-5.0 - np.arange(H_R, dtype=np.float64)))
    idx = np.arange(c, dtype=np.float64)
    rel = idx[:, None] - idx[None, :]
    dm = np.where(rel >= 0, np.exp(lg[:, None, None] * np.maximum(rel, 0.0)), 0.0)
    qd = np.broadcast_to(np.exp(lg[:, None] * (idx + 1.0))[:, :, None], (H_R, c, DK_R))
    kd = np.broadcast_to(np.exp(lg[:, None] * (c - 1.0 - idx))[:, :, None], (H_R, c, DK_R))
    cd = np.broadcast_to(np.exp(lg * c)[:, None, None], (H_R, 1, DV_R))
    return tuple(jnp.asarray(a, F32) for a in (dm, qd, kd, cd))


def _lower_bound(lb_ref):
    a = lb_ref[...]
    a0, a1 = a[0:1, :], a[1:2, :]
    m = jnp.maximum(a0, a1)
    e0 = jnp.exp(a0 - m)
    e1 = jnp.exp(a1 - m)
    return e0 / (e0 + e1)


def _forget_gate(gf, lb):
    sg = jax.nn.sigmoid(gf)
    f = lb + (1.0 - lb) * sg
    return jnp.log(f), (1.0 - lb) * (1.0 - sg), f


def _cumsum_rows(tri, x):
    hi = x.astype(BF16)
    rest = x - hi.astype(F32)
    mid = rest.astype(BF16)
    lo = (rest - mid.astype(F32)).astype(BF16)
    y = jnp.dot(tri, jnp.concatenate([hi, mid, lo], axis=1), preferred_element_type=F32)
    n = x.shape[1]
    return y[:, :n] + y[:, n:2 * n] + y[:, 2 * n:]


def _hgrn_pipeline(h_ref, wq_ref, wf_ref, wi_ref, wg_ref, lb_ref, gain_ref, o_ref, s_ref,
                   w_sc, z_sc, q_sc, b_sc, c_sc, *, chunk, sub, group, lag, heads):
    assert sub == 8 and chunk % (2 * sub) == 0
    g_rows = group * chunk
    n_chunks = h_ref.shape[0] // chunk
    assert n_chunks % group == 0 and lag >= 1
    lanes = lambda hh: slice(hh * DK_G, (hh + 1) * DK_G)
    cols = lambda hh, role: lanes(4 * hh + role)
    for hh in range(heads):
        for role, w_ref in enumerate((wq_ref, wf_ref, wi_ref, wg_ref)):
            w_sc[:, cols(hh, role)] = w_ref[:, lanes(hh)].astype(BF16)

    lb = _lower_bound(lb_ref)
    half_scale = 0.5 * (1.0 - lb)
    f_mid = lb + half_scale
    gain = gain_ref[...]
    r64 = lax.broadcasted_iota(jnp.int32, (chunk, chunk), 0)
    c64 = lax.broadcasted_iota(jnp.int32, (chunk, chunk), 1)
    tri = jnp.where(r64 >= c64, 1.0, 0.0).astype(BF16)
    same_sub = (r64 // sub) == (c64 // sub)
    place = [(c64[:sub] % sub == s) & (r64[:sub] >= s) for s in range(sub)]
    zrows = lambda n: jnp.zeros((n, DK_G), F32)

    def project(r0):
        rows = slice(r0, r0 + g_rows)
        z_sc[rows, :] = jnp.dot(h_ref[rows, :], w_sc[...], preferred_element_type=F32)

    def gates(hh, r0):
        rows = slice(r0, r0 + chunk)
        th_scaled = half_scale[:, lanes(hh)] * jnp.tanh(0.5 * z_sc[rows, cols(hh, 1)])
        b_sc[rows, lanes(hh)] = _cumsum_rows(tri, jnp.log2(f_mid[:, lanes(hh)] + th_scaled))
        c_sc[rows, lanes(hh)] = b_sc[rows, lanes(hh)] - jnp.log2(half_scale[:, lanes(hh)] - th_scaled)
        q_sc[rows, lanes(hh)] = _silu(z_sc[rows, cols(hh, 0)])

    def diagonal(hh, r0):
        blocks = []
        for o in range(r0, r0 + chunk, sub):
            q, b = q_sc[o:o + sub, lanes(hh)], b_sc[o:o + sub, lanes(hh)]
            acc = jnp.zeros((sub, chunk), F32)
            for s in range(sub):
                p = jnp.exp2(b - c_sc[o + s:o + s + 1, lanes(hh)]) * q
                acc = jnp.where(place[s], jnp.sum(p, axis=-1, keepdims=True), acc)
            blocks.append(acc)
        return jnp.where(same_sub, jnp.concatenate(blocks, axis=0), 0.0)

    def below(hh, r0, w):
        qs, ks = [], []
        for mid in range(r0 + w, r0 + chunk, 2 * w):
            ref_b = b_sc[mid - 1:mid, lanes(hh)]
            qs += [zrows(w), q_sc[mid:mid + w, lanes(hh)] * jnp.exp2(b_sc[mid:mid + w, lanes(hh)] - ref_b)]
            ks += [jnp.exp2(ref_b - c_sc[mid - w:mid, lanes(hh)]), zrows(w)]
        scores = _dot_nt(jnp.concatenate(qs, axis=0), jnp.concatenate(ks, axis=0))
        return jnp.where(c64 // w == r64 // w - 1, scores, 0.0)

    def scores(hh, r0):
        a = diagonal(hh, r0)
        w = chunk // 2
        while w >= sub:
            a = a + below(hh, r0, w)
            w //= 2
        return a

    def outputs(hh, r0, a, st):
        rows = slice(r0, r0 + chunk)
        v = z_sc[rows, cols(hh, 2)].astype(BF16)
        b_last = b_sc[rows.stop - 1:rows.stop, lanes(hh)]
        o = _dot(a, v) + _dot_nt(q_sc[rows, lanes(hh)] * jnp.exp2(b_sc[rows, lanes(hh)]), st)
        o_ref[rows, lanes(hh)] = _head_norm_gate(o, gain[:, lanes(hh)], z_sc[rows, cols(hh, 3)])
        return st * jnp.exp2(b_last) + _dot_tn(v, jnp.exp2(b_last - c_sc[rows, lanes(hh)]))

    st = [jnp.zeros((DV_G, DK_G), F32) for _ in range(heads)]
    pending = {}

    def do_scores(hh, c):
        pending[hh, c] = scores(hh, c * chunk)

    def do_outputs(hh, c):
        st[hh] = outputs(hh, c * chunk, pending.pop((hh, c)), st[hh])

    work = []
    for step in range(-lag, n_chunks + 2 * lag):
        if 0 <= step + lag < n_chunks and (step + lag) % group == 0:
            work.append(functools.partial(project, (step + lag) * chunk))
        for hh in range(heads):
            if 0 <= step < n_chunks:
                work.append(functools.partial(gates, hh, step * chunk))
            if 0 <= step - lag < n_chunks:
                work.append(functools.partial(do_scores, hh, step - lag))
            if 0 <= step - 2 * lag < n_chunks:
                work.append(functools.partial(do_outputs, hh, step - 2 * lag))

    def finish():
        for hh in range(heads):
            s_ref[hh] = st[hh].T

    return work, finish


def _run_pipeline(build, **config):
    def kernel(*refs):
        work, finish = build(*refs, **config)
        for item in work:
            item()
        finish()
    return kernel


def _ret_kernel(x_ref, g_ref, *refs, norm_rows, **config):
    hn_ref = refs[-3]

    @pl.when(pl.program_id(1) == 0)
    def _():
        def body(i, carry):
            rows = pl.ds(pl.multiple_of(i * norm_rows, norm_rows), norm_rows)
            hn_ref[rows, :] = _rms(x_ref[rows, :], g_ref[...]).astype(BF16)
            return carry

        lax.fori_loop(0, x_ref.shape[0] // norm_rows, body, 0)

    work, finish = _ret_pipeline(hn_ref, *refs[:-3], *refs[-2:], **config)
    for item in work:
        item()
    finish()


def _retention(x3, g_mix, w_in, cos, sin, gain):
    b, t, d = x3.shape
    n_head = 2 * DK_R + 2 * DV_R
    w_block = lambda width, lo: pl.BlockSpec((None, d, width), lambda bi, h: (0, 0, lo // width + h))
    head_tab = lambda rows, cols: pl.BlockSpec((None, rows, cols), lambda bi, h: (h, 0, 0))
    return pl.pallas_call(
        functools.partial(_ret_kernel, norm_rows=NORM_ROWS, chunk=RET_CHUNK, group=RET_GROUP, lag=RET_LAG),
        grid=(b, H_R),
        in_specs=[
            pl.BlockSpec((None, t, d), lambda bi, h: (bi, 0, 0)),
            pl.BlockSpec((1, d), lambda bi, h: (0, 0)),
            w_block(DK_R, 0), w_block(DK_R, QK_R), w_block(DV_R, 2 * QK_R), w_block(DV_R, 2 * QK_R + V_R),
            pl.BlockSpec((t, DK_R), lambda bi, h: (0, 0)),
            pl.BlockSpec((t, DK_R), lambda bi, h: (0, 0)),
            head_tab(RET_CHUNK, RET_CHUNK), head_tab(RET_CHUNK, DK_R), head_tab(RET_CHUNK, DK_R),
            head_tab(1, DV_R),
            pl.BlockSpec((1, DV_R), lambda bi, h: (0, h)),
        ],
        out_specs=[
            pl.BlockSpec((None, t, DV_R), lambda bi, h: (bi, 0, h)),
            pl.BlockSpec((None, None, DK_R, DV_R), lambda bi, h: (bi, h, 0, 0)),
            pl.BlockSpec((None, t, d), lambda bi, h: (bi, 0, 0)),
        ],
        out_shape=[
            jax.ShapeDtypeStruct((b, t, V_R), F32),
            jax.ShapeDtypeStruct((b, H_R, DK_R, DV_R), F32),
            jax.ShapeDtypeStruct((b, t, d), BF16),
        ],
        scratch_shapes=[pltpu.VMEM((d, n_head), BF16), pltpu.VMEM((t, n_head), F32)],
        compiler_params=_params(("parallel", "arbitrary")),
        name="retention",
    )(x3, g_mix, w_in, w_in, w_in, w_in, cos, sin, *_retention_tables(RET_CHUNK), gain)


def _hgrn(h3, w_in, hg_lb, gain):
    b, t, d = h3.shape
    heads = HG_HEADS
    wide = heads * DK_G
    first = (2 * QK_R + 2 * V_R) // wide
    w_block = lambda role: pl.BlockSpec((None, d, wide), lambda bi, h: (0, 0, first + role * (H_G // heads) + h))
    return pl.pallas_call(
        _run_pipeline(_hgrn_pipeline, chunk=HG_CHUNK, sub=HG_SUB, group=HG_GROUP, lag=HG_LAG, heads=heads),
        grid=(b, H_G // heads),
        in_specs=[
            pl.BlockSpec((None, t, d), lambda bi, h: (bi, 0, 0)),
            w_block(0), w_block(1), w_block(2), w_block(3),
            pl.BlockSpec((2, wide), lambda bi, h: (0, h)),
            pl.BlockSpec((1, wide), lambda bi, h: (0, h)),
        ],
        out_specs=[
            pl.BlockSpec((None, t, wide), lambda bi, h: (bi, 0, h)),
            pl.BlockSpec((None, heads, DK_G, DV_G), lambda bi, h: (bi, h, 0, 0)),
        ],
        out_shape=[
            jax.ShapeDtypeStruct((b, t, V_G), F32),
            jax.ShapeDtypeStruct((b, H_G, DK_G, DV_G), F32),
        ],
        scratch_shapes=[
            pltpu.VMEM((d, 4 * wide), BF16),
            pltpu.VMEM((t, 4 * wide), F32),
            pltpu.VMEM((t, wide), F32),
            pltpu.VMEM((t, wide), F32),
            pltpu.VMEM((t, wide), F32),
        ],
        compiler_params=_params(("parallel", "arbitrary")),
        name="hgrn",
    )(h3, w_in, w_in, w_in, w_in, hg_lb, gain)


def _step_kernel(z_ref, cos_ref, sin_ref, lb_ref, rgain_ref, ggain_ref, sr0_ref, sg0_ref,
                 or_ref, og_ref, sr_ref, sg_ref, *, bb):
    first = lambda a: jnp.where(lax.broadcasted_iota(jnp.int32, a.shape, 0) == 0, a, 0.0)
    cos = cos_ref[...]
    sin = sin_ref[...]
    lb_all = _lower_bound(lb_ref)
    row8 = lax.broadcasted_iota(jnp.int32, (8, DK_G), 0)
    ones3 = jnp.where(row8 < 3, 1.0, 0.0)
    g0 = 2 * QK_R + 2 * V_R
    work = []
    for i in range(bb):
        z8 = lambda lo, n: jnp.broadcast_to(z_ref[i, :, lo:lo + n], (8, n))
        for h in range(H_R):
            q = _rotate(z8(h * DK_R, DK_R), cos, sin)
            k = _rotate(z8(QK_R + h * DK_R, DK_R), cos, sin) * (DK_R ** -0.5)
            v = z8(2 * QK_R + h * DV_R, DV_R)
            lanes = slice(h * DV_R, (h + 1) * DV_R)
            gate = z_ref[i, :, 2 * QK_R + V_R + h * DV_R:2 * QK_R + V_R + (h + 1) * DV_R]
            work.append((sr0_ref, sr_ref, or_ref, rgain_ref, i, h, lanes, q, gate,
                         1.0 - 2.0 ** (-5.0 - h), _dot_tn(first(k), v)))
        for h in range(H_G):
            lanes = slice(h * DK_G, (h + 1) * DK_G)
            q = _silu(z8(g0 + h * DK_G, DK_G))
            _, k, f = _forget_gate(z8(g0 + F_G + h * DK_G, DK_G), lb_all[:, lanes])
            v = z8(g0 + 2 * F_G + h * DV_G, DV_G)
            gate = z_ref[i, :, g0 + 2 * F_G + V_G + h * DV_G:g0 + 2 * F_G + V_G + (h + 1) * DV_G]
            f_hi = f.astype(BF16).astype(F32)
            f_mid = (f - f_hi).astype(BF16).astype(F32)
            f_lo = f - f_hi - f_mid
            lhs = jnp.where(row8 == 0, f_hi, jnp.where(row8 == 1, f_mid, jnp.where(row8 == 2, f_lo,
                            jnp.where(row8 == 3, k, 0.0))))
            rhs = jnp.concatenate([ones3, jnp.where(row8 == 3, v, 0.0)], axis=1)
            both = _dot_tn(lhs, rhs)
            work.append((sg0_ref, sg_ref, og_ref, ggain_ref, i, h, lanes, q, gate,
                         both[:, :DV_G], both[:, DV_G:]))
    outs = []
    for s0_ref, s_ref, _, _, i, h, _, q, _, decay, kv in work:
        s_new = decay * s0_ref[i, h] + kv
        s_ref[i, h] = s_new
        outs.append(_dot(q, s_new)[0:1, :])
    for (_, _, o_ref, gain_ref, i, _, lanes, _, gate, _, _), o in zip(work, outs):
        o_ref[i, :, lanes] = _head_norm_gate(o, gain_ref[:, lanes], gate)


def _step(z, cos, sin, hg_lb, rgain, ggain, sr0, sg0, bb):
    n, d_in = z.shape
    z3 = z.reshape(n, 1, d_in)
    row = lambda w: pl.BlockSpec((bb, 1, w), lambda i: (i, 0, 0))
    const = lambda a: pl.BlockSpec(a.shape, lambda i: (0,) * a.ndim)
    sr_spec = pl.BlockSpec((bb, H_R, DK_R, DV_R), lambda i: (i, 0, 0, 0))
    sg_spec = pl.BlockSpec((bb, H_G, DK_G, DV_G), lambda i: (i, 0, 0, 0))
    return pl.pallas_call(
        functools.partial(_step_kernel, bb=bb),
        grid=(n // bb,),
        in_specs=[row(d_in), const(cos), const(sin), const(hg_lb), const(rgain), const(ggain),
                  sr_spec, sg_spec],
        out_specs=[row(V_R), row(V_G), sr_spec, sg_spec],
        out_shape=[
            jax.ShapeDtypeStruct((n, 1, V_R), F32),
            jax.ShapeDtypeStruct((n, 1, V_G), F32),
            jax.ShapeDtypeStruct(sr0.shape, F32),
            jax.ShapeDtypeStruct(sg0.shape, F32),
        ],
        compiler_params=_params(("parallel",)),
        name="step",
    )(z3, cos, sin, hg_lb, rgain, ggain, sr0, sg0)


def _post_kernel(x_ref, h_ref, gr_ref, gg_ref, p_ref, wa_ref, wur_ref, wug_ref, wo_ref,
                 nm_ref, w1_ref, w2_ref, np_ref, wg_ref, wp_ref, nf_ref, y_ref):
    d = x_ref.shape[1]
    merge = _sigmoid(jnp.dot(h_ref[...], wa_ref[...], preferred_element_type=F32))
    u_r = jnp.dot(gr_ref[...].astype(BF16), wur_ref[...], preferred_element_type=F32)
    u_g = jnp.dot(gg_ref[...].astype(BF16), wug_ref[...], preferred_element_type=F32)
    m = merge[:, :d] * u_r + merge[:, d:] * u_g
    r = x_ref[...] + jnp.dot(m.astype(BF16), wo_ref[...], preferred_element_type=F32)
    hm = _rms(r, nm_ref[...]).astype(BF16)
    a = jnp.maximum(jnp.dot(hm, w1_ref[...], preferred_element_type=F32), 0.0)
    r = r + jnp.dot((a * a).astype(BF16), w2_ref[...], preferred_element_type=F32)
    hp = _rms(r, np_ref[...]).astype(BF16)
    gate = _sigmoid(jnp.dot(hp, wg_ref[...], preferred_element_type=F32))
    r = r + gate * jnp.dot(p_ref[...].astype(BF16), wp_ref[...], preferred_element_type=F32)
    y_ref[...] = _rms(r, nf_ref[...])


def _post(x, h, g_r, g_g, p, wa, wur, wug, wo, nm, w1, w2, npl, wg, wp, nf, tm):
    n, d = x.shape
    rows = lambda w: pl.BlockSpec((tm, w), lambda i: (i, 0))
    const = lambda a: pl.BlockSpec(a.shape, lambda i: (0, 0), pipeline_mode=pl.Buffered(1))
    return pl.pallas_call(
        _post_kernel,
        grid=(n // tm,),
        in_specs=[rows(d), rows(d), rows(d), rows(d), rows(p.shape[1]),
                  const(wa), const(wur), const(wug), const(wo), const(nm), const(w1), const(w2),
                  const(npl), const(wg), const(wp), const(nf)],
        out_specs=rows(d),
        out_shape=jax.ShapeDtypeStruct((n, d), F32),
        compiler_params=_params(("parallel",)),
        name="post",
    )(x, h, g_r, g_g, p, wa, wur, wug, wo, nm, w1, w2, npl, wg, wp, nf)


def _rope_tables(pos):
    half = DK_R // 2
    inv = ROPE_BASE ** (-jnp.arange(half, dtype=F32) / half)
    ang = pos[:, None] * inv[None, :]
    cos, sin = jnp.cos(ang), jnp.sin(ang)
    return jnp.concatenate([cos, cos], axis=-1), jnp.concatenate([-sin, sin], axis=-1)


def kernel(x_prompt, x_sample, state_ret, state_hgrn, p_prompt, p_sample, norm_mix_g, w_in,
           ret_norm_g, hg_norm_g, hg_lb, w_up_ret, w_up_hg, w_o, norm_mlp_g, w_ff1, w_ff2,
           norm_ple_g, w_ple_gate, w_ple_proj, norm_final_g):
    b, t, d = x_prompt.shape
    nb, td, _ = x_sample.shape
    assert w_in.shape[0] == 1 and td == 1 and hg_lb.shape[0] == 2
    bf = lambda w: w[0].astype(BF16)
    row = lambda g: g.reshape(1, -1)
    n_mix = 2 * QK_R + 2 * V_R + 2 * F_G + 2 * V_G
    post_w = (w_in[0, :, n_mix:].astype(BF16), bf(w_up_ret), bf(w_up_hg), bf(w_o), row(norm_mlp_g[0]),
              bf(w_ff1), bf(w_ff2), row(norm_ple_g[0]), bf(w_ple_gate), bf(w_ple_proj), row(norm_final_g))
    g_mix = row(norm_mix_g[0])
    rgain = row(ret_norm_g[0])
    ggain = row(hg_norm_g[0])

    xp = x_prompt.reshape(b * t, d)
    cos_p, sin_p = _rope_tables(jnp.arange(t, dtype=F32))
    gr_p, ret_p, hp3 = _retention(x_prompt, g_mix, w_in, cos_p, sin_p, rgain)
    gg_p, hg_p = _hgrn(hp3, w_in, hg_lb, ggain)
    y_p = _post(xp, hp3.reshape(b * t, d), gr_p.reshape(b * t, -1), gg_p.reshape(b * t, -1),
                p_prompt[0].reshape(b * t, -1), *post_w, tm=256)

    xs = x_sample.reshape(nb, d)
    hs = _norm(xs, g_mix, tm=nb)
    zs = _proj(hs, w_in, tm=nb, tn=1024)
    cos_s, sin_s = _rope_tables(PAST_LEN + jnp.arange(td, dtype=F32))
    gr_s, gg_s, ret_s, hg_s = _step(zs, cos_s, sin_s, hg_lb, rgain, ggain,
                                    state_ret[0], state_hgrn[0], bb=4)
    y_s = _post(xs, hs, gr_s.reshape(nb, -1), gg_s.reshape(nb, -1),
                p_sample[0].reshape(nb, -1), *post_w, tm=nb)

    return (y_p.reshape(b, t, d), y_s.reshape(nb, td, d), ret_p[None], hg_p[None],
            ret_s[None], hg_s[None])
```

```python
import functools

import jax
import jax.numpy as jnp
import numpy as np
from jax import lax
from jax.experimental import pallas as pl
from jax.experimental.pallas import tpu as pltpu

F32 = jnp.float32
BF16 = jnp.bfloat16

PAST_LEN = 16384
ROPE_BASE = 10000.0
EPS = 1e-6

H_R, DK_R, DV_R = 4, 128, 256
H_G, DK_G, DV_G = 8, 128, 128
QK_R = H_R * DK_R
V_R = H_R * DV_R
F_G = H_G * DK_G
V_G = H_G * DV_G

V7X_VMEM_BYTES = 64 * 1024 * 1024
VMEM_LIMIT = V7X_VMEM_BYTES - 8 * 1024 * 1024

NORM_ROWS = 256
RET_CHUNK = 128
RET_GROUP = 2
RET_LAG = 1
HG_CHUNK = 64
HG_SUB = 8
HG_GROUP = 4
HG_LAG = 2
HG_HEADS = 2

NT_DIMS = (((1,), (1,)), ((), ()))
TN_DIMS = (((0,), (0,)), ((), ()))


def _dot(a, b):
    return jnp.dot(a.astype(BF16), b.astype(BF16), preferred_element_type=F32)


def _dot_nt(a, b):
    return lax.dot_general(a.astype(BF16), b.astype(BF16), NT_DIMS, preferred_element_type=F32)


def _dot_tn(a, b):
    return lax.dot_general(a.astype(BF16), b.astype(BF16), TN_DIMS, preferred_element_type=F32)


def _rms(x, g):
    return x * lax.rsqrt(jnp.mean(x * x, axis=-1, keepdims=True) + EPS) * g


def _sigmoid(x):
    return 0.5 * jnp.tanh(0.5 * x) + 0.5


def _silu(x):
    half = 0.5 * x
    return half + half * jnp.tanh(half)


def _params(sem):
    return pltpu.CompilerParams(dimension_semantics=sem, vmem_limit_bytes=VMEM_LIMIT)


def _norm_kernel(x_ref, g_ref, h_ref):
    h_ref[...] = _rms(x_ref[...], g_ref[...]).astype(BF16)


def _norm(x, g, tm):
    n, d = x.shape
    return pl.pallas_call(
        _norm_kernel,
        grid=(n // tm,),
        in_specs=[pl.BlockSpec((tm, d), lambda i: (i, 0)), pl.BlockSpec((1, d), lambda i: (0, 0))],
        out_specs=pl.BlockSpec((tm, d), lambda i: (i, 0)),
        out_shape=jax.ShapeDtypeStruct((n, d), BF16),
        compiler_params=_params(("parallel",)),
        name="norm",
    )(x, g)


def _proj_kernel(h_ref, w_ref, z_ref):
    z_ref[...] = jnp.dot(h_ref[...], w_ref[...].astype(BF16), preferred_element_type=F32)


def _proj(h, w_in, tm, tn):
    n, d = h.shape
    d_out = w_in.shape[2]
    return pl.pallas_call(
        _proj_kernel,
        grid=(n // tm, d_out // tn),
        in_specs=[
            pl.BlockSpec((tm, d), lambda i, j: (i, 0)),
            pl.BlockSpec((None, d, tn), lambda i, j: (0, 0, j)),
        ],
        out_specs=pl.BlockSpec((tm, tn), lambda i, j: (i, j)),
        out_shape=jax.ShapeDtypeStruct((n, d_out), F32),
        compiler_params=_params(("parallel", "arbitrary")),
        name="proj",
    )(h, w_in)


def _rotate(x, cos, sin_signed):
    return x * cos + pltpu.roll(x, DK_R // 2, 1) * sin_signed


def _head_norm_gate(o, gain, gate):
    return o * lax.rsqrt(jnp.mean(o * o, axis=-1, keepdims=True) + EPS) * gain * _silu(gate)


def _ret_pipeline(h_ref, wq_ref, wk_ref, wv_ref, wg_ref, cos_ref, sin_ref, dm_ref, qd_ref, kd_ref,
                  cd_ref, gain_ref, o_ref, s_ref, w_sc, z_sc, *, chunk, group, lag):
    n_chunks = h_ref.shape[0] // chunk
    assert n_chunks % group == 0 and lag >= 1
    g_rows = group * chunk
    q_cols = slice(0, DK_R)
    k_cols = slice(DK_R, 2 * DK_R)
    v_cols = slice(2 * DK_R, 2 * DK_R + DV_R)
    g_cols = slice(2 * DK_R + DV_R, 2 * DK_R + 2 * DV_R)
    for cols, w_ref in ((q_cols, wq_ref), (k_cols, wk_ref), (v_cols, wv_ref), (g_cols, wg_ref)):
        w_sc[:, cols] = w_ref[...].astype(BF16)
    dm = dm_ref[...]
    qd = qd_ref[...]
    kd = kd_ref[...]
    cd = cd_ref[...]
    gain = gain_ref[...]

    def project(r0):
        rows = slice(r0, r0 + g_rows)
        z_sc[rows, :] = jnp.dot(h_ref[rows, :], w_sc[...], preferred_element_type=F32)

    def scores(r0):
        rows = slice(r0, r0 + chunk)
        cos = cos_ref[rows, :]
        sin = sin_ref[rows, :]
        q = _rotate(z_sc[rows, q_cols], cos, sin)
        k = _rotate(z_sc[rows, k_cols], cos, sin) * (DK_R ** -0.5)
        a = (_dot_nt(q, k) * dm).astype(BF16)
        return a, (q * qd).astype(BF16), (k * kd).astype(BF16)

    def outputs(r0, parts, s):
        a, q_in, k_st = parts
        rows = slice(r0, r0 + chunk)
        v = z_sc[rows, v_cols].astype(BF16)
        o = (jnp.dot(a, v, preferred_element_type=F32)
             + jnp.dot(q_in, s.astype(BF16), preferred_element_type=F32))
        o_ref[rows, :] = _head_norm_gate(o, gain, z_sc[rows, g_cols])
        return cd * s + lax.dot_general(k_st, v, TN_DIMS, preferred_element_type=F32)

    carry = {"s": jnp.zeros((DK_R, DV_R), F32)}
    pending = {}

    def do_scores(c):
        pending[c] = scores(c * chunk)

    def do_outputs(c):
        carry["s"] = outputs(c * chunk, pending.pop(c), carry["s"])

    work = []
    for step in range(-lag, n_chunks + lag):
        if 0 <= step + lag < n_chunks and (step + lag) % group == 0:
            work.append(functools.partial(project, (step + lag) * chunk))
        if 0 <= step < n_chunks:
            work.append(functools.partial(do_scores, step))
        if 0 <= step - lag < n_chunks:
            work.append(functools.partial(do_outputs, step - lag))

    def finish():
        s_ref[...] = carry["s"]

    return work, finish


def _retention_tables(c):
    lg = np.log(1.0 - 2.0 ** (-5.0 - np.arange(H_R, dtype=np.float64)))
    idx = np.arange(c, dtype=np.float64)
    rel = idx[:, None] - idx[None, :]
    dm = np.where(rel >= 0, np.exp(lg[:, None, None] * np.maximum(rel, 0.0)), 0.0)
    qd = np.broadcast_to(np.exp(lg[:, None] * (idx + 1.0))[:, :, None], (H_R, c, DK_R))
    kd = np.broadcast_to(np.exp(lg[:, None] * (c - 1.0 - idx))[:, :, None], (H_R, c, DK_R))
    cd = np.broadcast_to(np.exp(lg * c)[:, None, None], (H_R, 1, DV_R))
    return tuple(jnp.asarray(a, F32) for a in (dm, qd, kd, cd))


def _lower_bound(lb_ref):
    a = lb_ref[...]
    a0, a1 = a[0:1, :], a[1:2, :]
    m = jnp.maximum(a0, a1)
    e0 = jnp.exp(a0 - m)
    e1 = jnp.exp(a1 - m)
    return e0 / (e0 + e1)


def _forget_gate(gf, lb):
    sg = jax.nn.sigmoid(gf)
    f = lb + (1.0 - lb) * sg
    return jnp.log(f), (1.0 - lb) * (1.0 - sg), f


def _cumsum_rows(tri, x):
    hi = x.astype(BF16)
    rest = x - hi.astype(F32)
    mid = rest.astype(BF16)
    lo = (rest - mid.astype(F32)).astype(BF16)
    y = jnp.dot(tri, jnp.concatenate([hi, mid, lo], axis=1), preferred_element_type=F32)
    n = x.shape[1]
    return y[:, :n] + y[:, n:2 * n] + y[:, 2 * n:]


def _hgrn_pipeline(h_ref, wq_ref, wf_ref, wi_ref, wg_ref, lb_ref, gain_ref, o_ref, s_ref,
                   w_sc, z_sc, q_sc, b_sc, c_sc, *, chunk, sub, group, lag, heads):
    assert sub == 8 and chunk % (2 * sub) == 0
    g_rows = group * chunk
    n_chunks = h_ref.shape[0] // chunk
    assert n_chunks % group == 0 and lag >= 1
    lanes = lambda hh: slice(hh * DK_G, (hh + 1) * DK_G)
    cols = lambda hh, role: lanes(4 * hh + role)
    for hh in range(heads):
        for role, w_ref in enumerate((wq_ref, wf_ref, wi_ref, wg_ref)):
            w_sc[:, cols(hh, role)] = w_ref[:, lanes(hh)].astype(BF16)

    lb = _lower_bound(lb_ref)
    half_scale = 0.5 * (1.0 - lb)
    f_mid = lb + half_scale
    gain = gain_ref[...]
    r64 = lax.broadcasted_iota(jnp.int32, (chunk, chunk), 0)
    c64 = lax.broadcasted_iota(jnp.int32, (chunk, chunk), 1)
    tri = jnp.where(r64 >= c64, 1.0, 0.0).astype(BF16)
    same_sub = (r64 // sub) == (c64 // sub)
    place = [(c64[:sub] % sub == s) & (r64[:sub] >= s) for s in range(sub)]
    zrows = lambda n: jnp.zeros((n, DK_G), F32)

    def project(r0):
        rows = slice(r0, r0 + g_rows)
        z_sc[rows, :] = jnp.dot(h_ref[rows, :], w_sc[...], preferred_element_type=F32)

    def gates(hh, r0):
        rows = slice(r0, r0 + chunk)
        th_scaled = half_scale[:, lanes(hh)] * jnp.tanh(0.5 * z_sc[rows, cols(hh, 1)])
        b_sc[rows, lanes(hh)] = _cumsum_rows(tri, jnp.log2(f_mid[:, lanes(hh)] + th_scaled))
        c_sc[rows, lanes(hh)] = b_sc[rows, lanes(hh)] - jnp.log2(half_scale[:, lanes(hh)] - th_scaled)
        q_sc[rows, lanes(hh)] = _silu(z_sc[rows, cols(hh, 0)])

    def diagonal(hh, r0):
        blocks = []
        for o in range(r0, r0 + chunk, sub):
            q, b = q_sc[o:o + sub, lanes(hh)], b_sc[o:o + sub, lanes(hh)]
            acc = jnp.zeros((sub, chunk), F32)
            for s in range(sub):
                p = jnp.exp2(b - c_sc[o + s:o + s + 1, lanes(hh)]) * q
                acc = jnp.where(place[s], jnp.sum(p, axis=-1, keepdims=True), acc)
            blocks.append(acc)
        return jnp.where(same_sub, jnp.concatenate(blocks, axis=0), 0.0)

    def below(hh, r0, w):
        qs, ks = [], []
        for mid in range(r0 + w, r0 + chunk, 2 * w):
            ref_b = b_sc[mid - 1:mid, lanes(hh)]
            qs += [zrows(w), q_sc[mid:mid + w, lanes(hh)] * jnp.exp2(b_sc[mid:mid + w, lanes(hh)] - ref_b)]
            ks += [jnp.exp2(ref_b - c_sc[mid - w:mid, lanes(hh)]), zrows(w)]
        scores = _dot_nt(jnp.concatenate(qs, axis=0), jnp.concatenate(ks, axis=0))
        return jnp.where(c64 // w == r64 // w - 1, scores, 0.0)

    def scores(hh, r0):
        a = diagonal(hh, r0)
        w = chunk // 2
        while w >= sub:
            a = a + below(hh, r0, w)
            w //= 2
        return a

    def outputs(hh, r0, a, st):
        rows = slice(r0, r0 + chunk)
        v = z_sc[rows, cols(hh, 2)].astype(BF16)
        b_last = b_sc[rows.stop - 1:rows.stop, lanes(hh)]
        o = _dot(a, v) + _dot_nt(q_sc[rows, lanes(hh)] * jnp.exp2(b_sc[rows, lanes(hh)]), st)
        o_ref[rows, lanes(hh)] = _head_norm_gate(o, gain[:, lanes(hh)], z_sc[rows, cols(hh, 3)])
        return st * jnp.exp2(b_last) + _dot_tn(v, jnp.exp2(b_last - c_sc[rows, lanes(hh)]))

    st = [jnp.zeros((DV_G, DK_G), F32) for _ in range(heads)]
    pending = {}

    def do_scores(hh, c):
        pending[hh, c] = scores(hh, c * chunk)

    def do_outputs(hh, c):
        st[hh] = outputs(hh, c * chunk, pending.pop((hh, c)), st[hh])

    work = []
    for step in range(-lag, n_chunks + 2 * lag):
        if 0 <= step + lag < n_chunks and (step + lag) % group == 0:
            work.append(functools.partial(project, (step + lag) * chunk))
        for hh in range(heads):
            if 0 <= step < n_chunks:
                work.append(functools.partial(gates, hh, step * chunk))
            if 0 <= step - lag < n_chunks:
                work.append(functools.partial(do_scores, hh, step - lag))
            if 0 <= step - 2 * lag < n_chunks:
                work.append(functools.partial(do_outputs, hh, step - 2 * lag))

    def finish():
        for hh in range(heads):
            s_ref[hh] = st[hh].T

    return work, finish


def _run_pipeline(build, **config):
    def kernel(*refs):
        work, finish = build(*refs, **config)
        for item in work:
            item()
        finish()
    return kernel


def _ret_kernel(x_ref, g_ref, *refs, norm_rows, **config):
    hn_ref = refs[-3]

    @pl.when(pl.program_id(1) == 0)
    def _():
        def body(i, carry):
            rows = pl.ds(pl.multiple_of(i * norm_rows, norm_rows), norm_rows)
            hn_ref[rows, :] = _rms(x_ref[rows, :], g_ref[...]).astype(BF16)
            return carry

        lax.fori_loop(0, x_ref.shape[0] // norm_rows, body, 0)

    work, finish = _ret_pipeline(hn_ref, *refs[:-3], *refs[-2:], **config)
    for item in work:
        item()
    finish()


def _retention(x3, g_mix, w_in, cos, sin, gain):
    b, t, d = x3.shape
    n_head = 2 * DK_R + 2 * DV_R
    w_block = lambda width, lo: pl.BlockSpec((None, d, width), lambda bi, h: (0, 0, lo // width + h))
    head_tab = lambda rows, cols: pl.BlockSpec((None, rows, cols), lambda bi, h: (h, 0, 0))
    return pl.pallas_call(
        functools.partial(_ret_kernel, norm_rows=NORM_ROWS, chunk=RET_CHUNK, group=RET_GROUP, lag=RET_LAG),
        grid=(b, H_R),
        in_specs=[
            pl.BlockSpec((None, t, d), lambda bi, h: (bi, 0, 0)),
            pl.BlockSpec((1, d), lambda bi, h: (0, 0)),
            w_block(DK_R, 0), w_block(DK_R, QK_R), w_block(DV_R, 2 * QK_R), w_block(DV_R, 2 * QK_R + V_R),
            pl.BlockSpec((t, DK_R), lambda bi, h: (0, 0)),
            pl.BlockSpec((t, DK_R), lambda bi, h: (0, 0)),
            head_tab(RET_CHUNK, RET_CHUNK), head_tab(RET_CHUNK, DK_R), head_tab(RET_CHUNK, DK_R),
            head_tab(1, DV_R),
            pl.BlockSpec((1, DV_R), lambda bi, h: (0, h)),
        ],
        out_specs=[
            pl.BlockSpec((None, t, DV_R), lambda bi, h: (bi, 0, h)),
            pl.BlockSpec((None, None, DK_R, DV_R), lambda bi, h: (bi, h, 0, 0)),
            pl.BlockSpec((None, t, d), lambda bi, h: (bi, 0, 0)),
        ],
        out_shape=[
            jax.ShapeDtypeStruct((b, t, V_R), F32),
            jax.ShapeDtypeStruct((b, H_R, DK_R, DV_R), F32),
            jax.ShapeDtypeStruct((b, t, d), BF16),
        ],
        scratch_shapes=[pltpu.VMEM((d, n_head), BF16), pltpu.VMEM((t, n_head), F32)],
        compiler_params=_params(("parallel", "arbitrary")),
        name="retention",
    )(x3, g_mix, w_in, w_in, w_in, w_in, cos, sin, *_retention_tables(RET_CHUNK), gain)


def _hgrn(h3, w_in, hg_lb, gain):
    b, t, d = h3.shape
    heads = HG_HEADS
    wide = heads * DK_G
    first = (2 * QK_R + 2 * V_R) // wide
    w_block = lambda role: pl.BlockSpec((None, d, wide), lambda bi, h: (0, 0, first + role * (H_G // heads) + h))
    return pl.pallas_call(
        _run_pipeline(_hgrn_pipeline, chunk=HG_CHUNK, sub=HG_SUB, group=HG_GROUP, lag=HG_LAG, heads=heads),
        grid=(b, H_G // heads),
        in_specs=[
            pl.BlockSpec((None, t, d), lambda bi, h: (bi, 0, 0)),
            w_block(0), w_block(1), w_block(2), w_block(3),
            pl.BlockSpec((2, wide), lambda bi, h: (0, h)),
            pl.BlockSpec((1, wide), lambda bi, h: (0, h)),
        ],
        out_specs=[
            pl.BlockSpec((None, t, wide), lambda bi, h: (bi, 0, h)),
            pl.BlockSpec((None, heads, DK_G, DV_G), lambda bi, h: (bi, h, 0, 0)),
        ],
        out_shape=[
            jax.ShapeDtypeStruct((b, t, V_G), F32),
            jax.ShapeDtypeStruct((b, H_G, DK_G, DV_G), F32),
        ],
        scratch_shapes=[
            pltpu.VMEM((d, 4 * wide), BF16),
            pltpu.VMEM((t, 4 * wide), F32),
            pltpu.VMEM((t, wide), F32),
            pltpu.VMEM((t, wide), F32),
            pltpu.VMEM((t, wide), F32),
        ],
        compiler_params=_params(("parallel", "arbitrary")),
        name="hgrn",
    )(h3, w_in, w_in, w_in, w_in, hg_lb, gain)


def _step_kernel(z_ref, cos_ref, sin_ref, lb_ref, rgain_ref, ggain_ref, sr0_ref, sg0_ref,
                 or_ref, og_ref, sr_ref, sg_ref, *, bb):
    first = lambda a: jnp.where(lax.broadcasted_iota(jnp.int32, a.shape, 0) == 0, a, 0.0)
    cos = cos_ref[...]
    sin = sin_ref[...]
    lb_all = _lower_bound(lb_ref)
    row8 = lax.broadcasted_iota(jnp.int32, (8, DK_G), 0)
    ones3 = jnp.where(row8 < 3, 1.0, 0.0)
    g0 = 2 * QK_R + 2 * V_R
    work = []
    for i in range(bb):
        z8 = lambda lo, n: jnp.broadcast_to(z_ref[i, :, lo:lo + n], (8, n))
        for h in range(H_R):
            q = _rotate(z8(h * DK_R, DK_R), cos, sin)
            k = _rotate(z8(QK_R + h * DK_R, DK_R), cos, sin) * (DK_R ** -0.5)
            v = z8(2 * QK_R + h * DV_R, DV_R)
            lanes = slice(h * DV_R, (h + 1) * DV_R)
            gate = z_ref[i, :, 2 * QK_R + V_R + h * DV_R:2 * QK_R + V_R + (h + 1) * DV_R]
            work.append((sr0_ref, sr_ref, or_ref, rgain_ref, i, h, lanes, q, gate,
                         1.0 - 2.0 ** (-5.0 - h), _dot_tn(first(k), v)))
        for h in range(H_G):
            lanes = slice(h * DK_G, (h + 1) * DK_G)
            q = _silu(z8(g0 + h * DK_G, DK_G))
            _, k, f = _forget_gate(z8(g0 + F_G + h * DK_G, DK_G), lb_all[:, lanes])
            v = z8(g0 + 2 * F_G + h * DV_G, DV_G)
            gate = z_ref[i, :, g0 + 2 * F_G + V_G + h * DV_G:g0 + 2 * F_G + V_G + (h + 1) * DV_G]
            f_hi = f.astype(BF16).astype(F32)
            f_mid = (f - f_hi).astype(BF16).astype(F32)
            f_lo = f - f_hi - f_mid
            lhs = jnp.where(row8 == 0, f_hi, jnp.where(row8 == 1, f_mid, jnp.where(row8 == 2, f_lo,
                            jnp.where(row8 == 3, k, 0.0))))
            rhs = jnp.concatenate([ones3, jnp.where(row8 == 3, v, 0.0)], axis=1)
            both = _dot_tn(lhs, rhs)
            work.append((sg0_ref, sg_ref, og_ref, ggain_ref, i, h, lanes, q, gate,
                         both[:, :DV_G], both[:, DV_G:]))
    outs = []
    for s0_ref, s_ref, _, _, i, h, _, q, _, decay, kv in work:
        s_new = decay * s0_ref[i, h] + kv
        s_ref[i, h] = s_new
        outs.append(_dot(q, s_new)[0:1, :])
    for (_, _, o_ref, gain_ref, i, _, lanes, _, gate, _, _), o in zip(work, outs):
        o_ref[i, :, lanes] = _head_norm_gate(o, gain_ref[:, lanes], gate)


def _step_specs(z, cos, sin, hg_lb, rgain, ggain, sr0, sg0, bb):
    n, d_in = z.shape
    row = lambda w: pl.BlockSpec((bb, 1, w), lambda i: (i, 0, 0))
    const = lambda a: pl.BlockSpec(a.shape, lambda i: (0,) * a.ndim)
    sr_spec = pl.BlockSpec((bb, H_R, DK_R, DV_R), lambda i: (i, 0, 0, 0))
    sg_spec = pl.BlockSpec((bb, H_G, DK_G, DV_G), lambda i: (i, 0, 0, 0))
    operands = (z.reshape(n, 1, d_in), cos, sin, hg_lb, rgain, ggain, sr0, sg0)
    in_specs = [row(d_in), const(cos), const(sin), const(hg_lb), const(rgain), const(ggain), sr_spec, sg_spec]
    out_specs = [row(V_R), row(V_G), sr_spec, sg_spec]
    out_shape = [
        jax.ShapeDtypeStruct((n, 1, V_R), F32),
        jax.ShapeDtypeStruct((n, 1, V_G), F32),
        jax.ShapeDtypeStruct(sr0.shape, F32),
        jax.ShapeDtypeStruct(sg0.shape, F32),
    ]
    return operands, in_specs, out_specs, out_shape


N_POST_IN = 16
N_STEP_IN = 8


def _post_kernel(*refs, bb):
    if bb:
        _post_body(*refs[:N_POST_IN], refs[N_POST_IN + N_STEP_IN])
        _step_kernel(*refs[N_POST_IN:N_POST_IN + N_STEP_IN], *refs[N_POST_IN + N_STEP_IN + 1:], bb=bb)
    else:
        _post_body(*refs)


def _post_body(x_ref, h_ref, gr_ref, gg_ref, p_ref, wa_ref, wur_ref, wug_ref, wo_ref,
               nm_ref, w1_ref, w2_ref, np_ref, wg_ref, wp_ref, nf_ref, y_ref):
    d = x_ref.shape[1]
    merge = _sigmoid(jnp.dot(h_ref[...], wa_ref[...], preferred_element_type=F32))
    u_r = jnp.dot(gr_ref[...].astype(BF16), wur_ref[...], preferred_element_type=F32)
    u_g = jnp.dot(gg_ref[...].astype(BF16), wug_ref[...], preferred_element_type=F32)
    m = merge[:, :d] * u_r + merge[:, d:] * u_g
    r = x_ref[...] + jnp.dot(m.astype(BF16), wo_ref[...], preferred_element_type=F32)
    hm = _rms(r, nm_ref[...]).astype(BF16)
    a = jnp.maximum(jnp.dot(hm, w1_ref[...], preferred_element_type=F32), 0.0)
    r = r + jnp.dot((a * a).astype(BF16), w2_ref[...], preferred_element_type=F32)
    hp = _rms(r, np_ref[...]).astype(BF16)
    gate = _sigmoid(jnp.dot(hp, wg_ref[...], preferred_element_type=F32))
    r = r + gate * jnp.dot(p_ref[...].astype(BF16), wp_ref[...], preferred_element_type=F32)
    y_ref[...] = _rms(r, nf_ref[...])


def _post(x, h, g_r, g_g, p, wa, wur, wug, wo, nm, w1, w2, npl, wg, wp, nf, tm, step=None):
    n, d = x.shape
    grid = n // tm
    rows = lambda w: pl.BlockSpec((tm, w), lambda i: (i, 0))
    const = lambda a: pl.BlockSpec(a.shape, lambda i: (0, 0), pipeline_mode=pl.Buffered(1))
    operands = (x, h, g_r, g_g, p, wa, wur, wug, wo, nm, w1, w2, npl, wg, wp, nf)
    in_specs = [rows(d), rows(d), rows(d), rows(d), rows(p.shape[1]),
                const(wa), const(wur), const(wug), const(wo), const(nm), const(w1), const(w2),
                const(npl), const(wg), const(wp), const(nf)]
    assert len(operands) == N_POST_IN
    out_specs = [rows(d)]
    out_shape = [jax.ShapeDtypeStruct((n, d), F32)]
    bb = 0
    if step is not None:
        bb = step[0].shape[0] // grid
        assert bb * grid == step[0].shape[0] and len(step) == N_STEP_IN
        s_operands, s_in, s_out, s_shape = _step_specs(*step, bb=bb)
        operands += s_operands
        in_specs += s_in
        out_specs += s_out
        out_shape += s_shape
    out = pl.pallas_call(
        functools.partial(_post_kernel, bb=bb),
        grid=(grid,),
        in_specs=in_specs,
        out_specs=out_specs,
        out_shape=out_shape,
        compiler_params=_params(("parallel",)),
        name="post",
    )(*operands)
    return out if bb else out[0]


def _rope_tables(pos):
    half = DK_R // 2
    inv = ROPE_BASE ** (-jnp.arange(half, dtype=F32) / half)
    ang = pos[:, None] * inv[None, :]
    cos, sin = jnp.cos(ang), jnp.sin(ang)
    return jnp.concatenate([cos, cos], axis=-1), jnp.concatenate([-sin, sin], axis=-1)


def kernel(x_prompt, x_sample, state_ret, state_hgrn, p_prompt, p_sample, norm_mix_g, w_in,
           ret_norm_g, hg_norm_g, hg_lb, w_up_ret, w_up_hg, w_o, norm_mlp_g, w_ff1, w_ff2,
           norm_ple_g, w_ple_gate, w_ple_proj, norm_final_g):
    b, t, d = x_prompt.shape
    nb, td, _ = x_sample.shape
    assert w_in.shape[0] == 1 and td == 1 and hg_lb.shape[0] == 2
    bf = lambda w: w[0].astype(BF16)
    row = lambda g: g.reshape(1, -1)
    n_mix = 2 * QK_R + 2 * V_R + 2 * F_G + 2 * V_G
    post_w = (w_in[0, :, n_mix:].astype(BF16), bf(w_up_ret), bf(w_up_hg), bf(w_o), row(norm_mlp_g[0]),
              bf(w_ff1), bf(w_ff2), row(norm_ple_g[0]), bf(w_ple_gate), bf(w_ple_proj), row(norm_final_g))
    g_mix = row(norm_mix_g[0])
    rgain = row(ret_norm_g[0])
    ggain = row(hg_norm_g[0])

    xs = x_sample.reshape(nb, d)
    hs = _norm(xs, g_mix, tm=nb)
    zs = _proj(hs, w_in, tm=nb, tn=1024)
    cos_s, sin_s = _rope_tables(PAST_LEN + jnp.arange(td, dtype=F32))
    step = (zs, cos_s, sin_s, hg_lb, rgain, ggain, state_ret[0], state_hgrn[0])

    xp = x_prompt.reshape(b * t, d)
    cos_p, sin_p = _rope_tables(jnp.arange(t, dtype=F32))
    gr_p, ret_p, hp3 = _retention(x_prompt, g_mix, w_in, cos_p, sin_p, rgain)
    gg_p, hg_p = _hgrn(hp3, w_in, hg_lb, ggain)
    y_p, gr_s, gg_s, ret_s, hg_s = _post(
        xp, hp3.reshape(b * t, d), gr_p.reshape(b * t, -1), gg_p.reshape(b * t, -1),
        p_prompt[0].reshape(b * t, -1), *post_w, tm=256, step=step)

    y_s = _post(xs, hs, gr_s.reshape(nb, -1), gg_s.reshape(nb, -1),
                p_sample[0].reshape(nb, -1), *post_w, tm=nb)

    return (y_p.reshape(b, t, d), y_s.reshape(nb, td, d), ret_p[None], hg_p[None],
            ret_s[None], hg_s[None])
```

```python
import functools

import jax
import jax.numpy as jnp
import numpy as np
from jax import lax
from jax.experimental import pallas as pl
from jax.experimental.pallas import tpu as pltpu

F32 = jnp.float32
BF16 = jnp.bfloat16

PAST_LEN = 16384
ROPE_BASE = 10000.0
EPS = 1e-6

H_R, DK_R, DV_R = 4, 128, 256
H_G, DK_G, DV_G = 8, 128, 128
QK_R = H_R * DK_R
V_R = H_R * DV_R
F_G = H_G * DK_G
V_G = H_G * DV_G

V7X_VMEM_BYTES = 64 * 1024 * 1024
VMEM_LIMIT = V7X_VMEM_BYTES - 8 * 1024 * 1024

NORM_ROWS = 256
RET_CHUNK = 128
RET_GROUP = 2
RET_LAG = 1
HG_CHUNK = 64
HG_SUB = 8
HG_GROUP = 4
HG_LAG = 2
HG_HEADS = 2

NT_DIMS = (((1,), (1,)), ((), ()))
TN_DIMS = (((0,), (0,)), ((), ()))


def _dot(a, b):
    return jnp.dot(a.astype(BF16), b.astype(BF16), preferred_element_type=F32)


def _dot_nt(a, b):
    return lax.dot_general(a.astype(BF16), b.astype(BF16), NT_DIMS, preferred_element_type=F32)


def _dot_tn(a, b):
    return lax.dot_general(a.astype(BF16), b.astype(BF16), TN_DIMS, preferred_element_type=F32)


def _rms(x, g):
    return x * lax.rsqrt(jnp.mean(x * x, axis=-1, keepdims=True) + EPS) * g


def _sigmoid(x):
    return 0.5 * jnp.tanh(0.5 * x) + 0.5


def _silu(x):
    half = 0.5 * x
    return half + half * jnp.tanh(half)


def _params(sem):
    return pltpu.CompilerParams(dimension_semantics=sem, vmem_limit_bytes=VMEM_LIMIT)


def _norm_kernel(x_ref, g_ref, h_ref):
    h_ref[...] = _rms(x_ref[...], g_ref[...]).astype(BF16)


def _norm(x, g, tm):
    n, d = x.shape
    return pl.pallas_call(
        _norm_kernel,
        grid=(n // tm,),
        in_specs=[pl.BlockSpec((tm, d), lambda i: (i, 0)), pl.BlockSpec((1, d), lambda i: (0, 0))],
        out_specs=pl.BlockSpec((tm, d), lambda i: (i, 0)),
        out_shape=jax.ShapeDtypeStruct((n, d), BF16),
        compiler_params=_params(("parallel",)),
        name="norm",
    )(x, g)


def _proj_kernel(h_ref, w_ref, z_ref):
    z_ref[...] = jnp.dot(h_ref[...], w_ref[...].astype(BF16), preferred_element_type=F32)


def _proj(h, w_in, tm, tn):
    n, d = h.shape
    d_out = w_in.shape[2]
    return pl.pallas_call(
        _proj_kernel,
        grid=(n // tm, d_out // tn),
        in_specs=[
            pl.BlockSpec((tm, d), lambda i, j: (i, 0)),
            pl.BlockSpec((None, d, tn), lambda i, j: (0, 0, j)),
        ],
        out_specs=pl.BlockSpec((tm, tn), lambda i, j: (i, j)),
        out_shape=jax.ShapeDtypeStruct((n, d_out), F32),
        compiler_params=_params(("parallel", "arbitrary")),
        name="proj",
    )(h, w_in)


def _rotate(x, cos, sin_signed):
    return x * cos + pltpu.roll(x, DK_R // 2, 1) * sin_signed


def _head_norm_gate(o, gain, gate):
    return o * lax.rsqrt(jnp.mean(o * o, axis=-1, keepdims=True) + EPS) * gain * _silu(gate)


def _ret_pipeline(h_ref, wq_ref, wk_ref, wv_ref, wg_ref, cos_ref, sin_ref, dm_ref, qd_ref, kd_ref,
                  cd_ref, gain_ref, o_ref, s_ref, w_sc, z_sc, *, chunk, group, lag):
    n_chunks = h_ref.shape[0] // chunk
    assert n_chunks % group == 0 and lag >= 1
    g_rows = group * chunk
    q_cols = slice(0, DK_R)
    k_cols = slice(DK_R, 2 * DK_R)
    v_cols = slice(2 * DK_R, 2 * DK_R + DV_R)
    g_cols = slice(2 * DK_R + DV_R, 2 * DK_R + 2 * DV_R)
    for cols, w_ref in ((q_cols, wq_ref), (k_cols, wk_ref), (v_cols, wv_ref), (g_cols, wg_ref)):
        w_sc[:, cols] = w_ref[...].astype(BF16)
    dm = dm_ref[...]
    qd = qd_ref[...]
    kd = kd_ref[...]
    cd = cd_ref[...]
    gain = gain_ref[...]

    assert group == 2
    pieces = (slice(0, g_cols.start), g_cols)

    def project(c):
        rows = slice(c // group * g_rows, (c // group + 1) * g_rows)
        pcols = pieces[c % group]
        z_sc[rows, pcols] = jnp.dot(h_ref[rows, :], w_sc[:, pcols], preferred_element_type=F32)

    def scores(r0):
        rows = slice(r0, r0 + chunk)
        cos = cos_ref[rows, :]
        sin = sin_ref[rows, :]
        q = _rotate(z_sc[rows, q_cols], cos, sin)
        k = _rotate(z_sc[rows, k_cols], cos, sin) * (DK_R ** -0.5)
        a = (_dot_nt(q, k) * dm).astype(BF16)
        return a, (q * qd).astype(BF16), (k * kd).astype(BF16)

    def outputs(r0, parts, s):
        a, q_in, k_st = parts
        rows = slice(r0, r0 + chunk)
        v = z_sc[rows, v_cols].astype(BF16)
        o = jnp.dot(jnp.concatenate([a, q_in], axis=1), jnp.concatenate([v, s.astype(BF16)], axis=0),
                    preferred_element_type=F32)
        o_ref[rows, :] = _head_norm_gate(o, gain, z_sc[rows, g_cols])
        return cd * s + lax.dot_general(k_st, v, TN_DIMS, preferred_element_type=F32)

    carry = {"s": jnp.zeros((DK_R, DV_R), F32)}
    pending = {}

    def do_scores(c):
        pending[c] = scores(c * chunk)

    def do_outputs(c):
        carry["s"] = outputs(c * chunk, pending.pop(c), carry["s"])

    work = []
    for step in range(-group, n_chunks + lag):
        if step + group < n_chunks:
            work.append(functools.partial(project, step + group))
        if 0 <= step < n_chunks:
            work.append(functools.partial(do_scores, step))
        if 0 <= step - lag < n_chunks:
            work.append(functools.partial(do_outputs, step - lag))

    def finish():
        s_ref[...] = carry["s"]

    return work, finish


def _retention_tables(c):
    lg = np.log(1.0 - 2.0 ** (-5.0 - np.arange(H_R, dtype=np.float64)))
    idx = np.arange(c, dtype=np.float64)
    rel = idx[:, None] - idx[None, :]
    dm = np.where(rel >= 0, np.exp(lg[:, None, None] * np.maximum(rel, 0.0)), 0.0)
    qd = np.broadcast_to(np.exp(lg[:, None] * (idx + 1.0))[:, :, None], (H_R, c, DK_R))
    kd = np.broadcast_to(np.exp(lg[:, None] * (c - 1.0 - idx))[:, :, None], (H_R, c, DK_R))
    cd = np.broadcast_to(np.exp(lg * c)[:, None, None], (H_R, 1, DV_R))
    return tuple(jnp.asarray(a, F32) for a in (dm, qd, kd, cd))


def _lower_bound(lb_ref):
    a = lb_ref[...]
    a0, a1 = a[0:1, :], a[1:2, :]
    m = jnp.maximum(a0, a1)
    e0 = jnp.exp(a0 - m)
    e1 = jnp.exp(a1 - m)
    return e0 / (e0 + e1)


def _forget_gate(gf, lb):
    sg = jax.nn.sigmoid(gf)
    f = lb + (1.0 - lb) * sg
    return jnp.log(f), (1.0 - lb) * (1.0 - sg), f


def _cumsum_rows(tri, x):
    hi = x.astype(BF16)
    rest = x - hi.astype(F32)
    mid = rest.astype(BF16)
    lo = (rest - mid.astype(F32)).astype(BF16)
    y = jnp.dot(tri, jnp.concatenate([hi, mid, lo], axis=1), preferred_element_type=F32)
    n = x.shape[1]
    return y[:, :n] + y[:, n:2 * n] + y[:, 2 * n:]


def _hgrn_pipeline(h_ref, wq_ref, wf_ref, wi_ref, wg_ref, lb_ref, gain_ref, o_ref, s_ref,
                   w_sc, z_sc, q_sc, b_sc, c_sc, *, chunk, sub, group, lag, heads):
    assert sub == 8 and chunk % (2 * sub) == 0
    g_rows = group * chunk
    n_chunks = h_ref.shape[0] // chunk
    assert n_chunks % group == 0 and lag >= 1
    lanes = lambda hh: slice(hh * DK_G, (hh + 1) * DK_G)
    cols = lambda hh, role: lanes(4 * hh + role)
    for hh in range(heads):
        for role, w_ref in enumerate((wq_ref, wf_ref, wi_ref, wg_ref)):
            w_sc[:, cols(hh, role)] = w_ref[:, lanes(hh)].astype(BF16)

    lb = _lower_bound(lb_ref)
    half_scale = 0.5 * (1.0 - lb)
    f_mid = lb + half_scale
    gain = gain_ref[...]
    r64 = lax.broadcasted_iota(jnp.int32, (chunk, chunk), 0)
    c64 = lax.broadcasted_iota(jnp.int32, (chunk, chunk), 1)
    tri = jnp.where(r64 >= c64, 1.0, 0.0).astype(BF16)
    same_sub = (r64 // sub) == (c64 // sub)
    place = [(c64[:sub] % sub == s) & (r64[:sub] >= s) for s in range(sub)]
    zrows = lambda n: jnp.zeros((n, DK_G), F32)

    piece_cols = z_sc.shape[1] // group
    assert piece_cols % DK_G == 0

    def project(c):
        rows = slice(c // group * g_rows, (c // group + 1) * g_rows)
        pcols = slice(c % group * piece_cols, (c % group + 1) * piece_cols)
        z_sc[rows, pcols] = jnp.dot(h_ref[rows, :], w_sc[:, pcols], preferred_element_type=F32)

    def gates(hh, r0):
        rows = slice(r0, r0 + chunk)
        th_scaled = half_scale[:, lanes(hh)] * jnp.tanh(0.5 * z_sc[rows, cols(hh, 1)])
        b_sc[rows, lanes(hh)] = _cumsum_rows(tri, jnp.log2(f_mid[:, lanes(hh)] + th_scaled))
        c_sc[rows, lanes(hh)] = b_sc[rows, lanes(hh)] - jnp.log2(half_scale[:, lanes(hh)] - th_scaled)
        q_sc[rows, lanes(hh)] = _silu(z_sc[rows, cols(hh, 0)])

    def diagonal(hh, r0):
        blocks = []
        for o in range(r0, r0 + chunk, sub):
            q, b = q_sc[o:o + sub, lanes(hh)], b_sc[o:o + sub, lanes(hh)]
            acc = jnp.zeros((sub, chunk), F32)
            for s in range(sub):
                p = jnp.exp2(b - c_sc[o + s:o + s + 1, lanes(hh)]) * q
                acc = jnp.where(place[s], jnp.sum(p, axis=-1, keepdims=True), acc)
            blocks.append(acc)
        return jnp.where(same_sub, jnp.concatenate(blocks, axis=0), 0.0)

    def below(hh, r0, w):
        qs, ks = [], []
        for mid in range(r0 + w, r0 + chunk, 2 * w):
            ref_b = b_sc[mid - 1:mid, lanes(hh)]
            qs += [zrows(w), q_sc[mid:mid + w, lanes(hh)] * jnp.exp2(b_sc[mid:mid + w, lanes(hh)] - ref_b)]
            ks += [jnp.exp2(ref_b - c_sc[mid - w:mid, lanes(hh)]), zrows(w)]
        scores = _dot_nt(jnp.concatenate(qs, axis=0), jnp.concatenate(ks, axis=0))
        return jnp.where(c64 // w == r64 // w - 1, scores, 0.0)

    def scores(hh, r0):
        a = diagonal(hh, r0)
        w = chunk // 2
        while w >= sub:
            a = a + below(hh, r0, w)
            w //= 2
        return a

    def outputs(hh, r0, a, st):
        rows = slice(r0, r0 + chunk)
        v = z_sc[rows, cols(hh, 2)].astype(BF16)
        b_last = b_sc[rows.stop - 1:rows.stop, lanes(hh)]
        o = _dot(a, v) + _dot_nt(q_sc[rows, lanes(hh)] * jnp.exp2(b_sc[rows, lanes(hh)]), st)
        o_ref[rows, lanes(hh)] = _head_norm_gate(o, gain[:, lanes(hh)], z_sc[rows, cols(hh, 3)])
        return st * jnp.exp2(b_last) + _dot_tn(v, jnp.exp2(b_last - c_sc[rows, lanes(hh)]))

    st = [jnp.zeros((DV_G, DK_G), F32) for _ in range(heads)]
    pending = {}

    def do_scores(hh, c):
        pending[hh, c] = scores(hh, c * chunk)

    def do_outputs(hh, c):
        st[hh] = outputs(hh, c * chunk, pending.pop((hh, c)), st[hh])

    work = []
    for step in range(-group, n_chunks + 2 * lag):
        if step + group < n_chunks:
            work.append(functools.partial(project, step + group))
        for hh in range(heads):
            if 0 <= step < n_chunks:
                work.append(functools.partial(gates, hh, step * chunk))
            if 0 <= step - lag < n_chunks:
                work.append(functools.partial(do_scores, hh, step - lag))
            if 0 <= step - 2 * lag < n_chunks:
                work.append(functools.partial(do_outputs, hh, step - 2 * lag))

    def finish():
        for hh in range(heads):
            s_ref[hh] = st[hh].T

    return work, finish


def _run_pipeline(build, **config):
    def kernel(*refs):
        work, finish = build(*refs, **config)
        for item in work:
            item()
        finish()
    return kernel


def _ret_kernel(x_ref, g_ref, *refs, norm_rows, **config):
    hn_ref = refs[-3]

    @pl.when(pl.program_id(1) == 0)
    def _():
        def body(i, carry):
            rows = pl.ds(pl.multiple_of(i * norm_rows, norm_rows), norm_rows)
            hn_ref[rows, :] = _rms(x_ref[rows, :], g_ref[...]).astype(BF16)
            return carry

        lax.fori_loop(0, x_ref.shape[0] // norm_rows, body, 0)

    work, finish = _ret_pipeline(hn_ref, *refs[:-3], *refs[-2:], **config)
    for item in work:
        item()
    finish()


def _retention(x3, g_mix, w_in, cos, sin, gain):
    b, t, d = x3.shape
    n_head = 2 * DK_R + 2 * DV_R
    w_block = lambda width, lo: pl.BlockSpec((None, d, width), lambda bi, h: (0, 0, lo // width + h))
    head_tab = lambda rows, cols: pl.BlockSpec((None, rows, cols), lambda bi, h: (h, 0, 0))
    return pl.pallas_call(
        functools.partial(_ret_kernel, norm_rows=NORM_ROWS, chunk=RET_CHUNK, group=RET_GROUP, lag=RET_LAG),
        grid=(b, H_R),
        in_specs=[
            pl.BlockSpec((None, t, d), lambda bi, h: (bi, 0, 0)),
            pl.BlockSpec((1, d), lambda bi, h: (0, 0)),
            w_block(DK_R, 0), w_block(DK_R, QK_R), w_block(DV_R, 2 * QK_R), w_block(DV_R, 2 * QK_R + V_R),
            pl.BlockSpec((t, DK_R), lambda bi, h: (0, 0)),
            pl.BlockSpec((t, DK_R), lambda bi, h: (0, 0)),
            head_tab(RET_CHUNK, RET_CHUNK), head_tab(RET_CHUNK, DK_R), head_tab(RET_CHUNK, DK_R),
            head_tab(1, DV_R),
            pl.BlockSpec((1, DV_R), lambda bi, h: (0, h)),
        ],
        out_specs=[
            pl.BlockSpec((None, t, DV_R), lambda bi, h: (bi, 0, h)),
            pl.BlockSpec((None, None, DK_R, DV_R), lambda bi, h: (bi, h, 0, 0)),
            pl.BlockSpec((None, t, d), lambda bi, h: (bi, 0, 0)),
        ],
        out_shape=[
            jax.ShapeDtypeStruct((b, t, V_R), F32),
            jax.ShapeDtypeStruct((b, H_R, DK_R, DV_R), F32),
            jax.ShapeDtypeStruct((b, t, d), BF16),
        ],
        scratch_shapes=[pltpu.VMEM((d, n_head), BF16), pltpu.VMEM((t, n_head), F32)],
        compiler_params=_params(("parallel", "arbitrary")),
        name="retention",
    )(x3, g_mix, w_in, w_in, w_in, w_in, cos, sin, *_retention_tables(RET_CHUNK), gain)


def _hgrn(h3, w_in, hg_lb, gain):
    b, t, d = h3.shape
    heads = HG_HEADS
    wide = heads * DK_G
    first = (2 * QK_R + 2 * V_R) // wide
    w_block = lambda role: pl.BlockSpec((None, d, wide), lambda bi, h: (0, 0, first + role * (H_G // heads) + h))
    return pl.pallas_call(
        _run_pipeline(_hgrn_pipeline, chunk=HG_CHUNK, sub=HG_SUB, group=HG_GROUP, lag=HG_LAG, heads=heads),
        grid=(b, H_G // heads),
        in_specs=[
            pl.BlockSpec((None, t, d), lambda bi, h: (bi, 0, 0)),
            w_block(0), w_block(1), w_block(2), w_block(3),
            pl.BlockSpec((2, wide), lambda bi, h: (0, h)),
            pl.BlockSpec((1, wide), lambda bi, h: (0, h)),
        ],
        out_specs=[
            pl.BlockSpec((None, t, wide), lambda bi, h: (bi, 0, h)),
            pl.BlockSpec((None, heads, DK_G, DV_G), lambda bi, h: (bi, h, 0, 0)),
        ],
        out_shape=[
            jax.ShapeDtypeStruct((b, t, V_G), F32),
            jax.ShapeDtypeStruct((b, H_G, DK_G, DV_G), F32),
        ],
        scratch_shapes=[
            pltpu.VMEM((d, 4 * wide), BF16),
            pltpu.VMEM((t, 4 * wide), F32),
            pltpu.VMEM((t, wide), F32),
            pltpu.VMEM((t, wide), F32),
            pltpu.VMEM((t, wide), F32),
        ],
        compiler_params=_params(("parallel", "arbitrary")),
        name="hgrn",
    )(h3, w_in, w_in, w_in, w_in, hg_lb, gain)


def _step_kernel(z_ref, cos_ref, sin_ref, lb_ref, rgain_ref, ggain_ref, sr0_ref, sg0_ref,
                 or_ref, og_ref, sr_ref, sg_ref, *, bb):
    first = lambda a: jnp.where(lax.broadcasted_iota(jnp.int32, a.shape, 0) == 0, a, 0.0)
    cos = cos_ref[...]
    sin = sin_ref[...]
    lb_all = _lower_bound(lb_ref)
    row8 = lax.broadcasted_iota(jnp.int32, (8, DK_G), 0)
    ones3 = jnp.where(row8 < 3, 1.0, 0.0)
    g0 = 2 * QK_R + 2 * V_R
    work = []
    for i in range(bb):
        z8 = lambda lo, n: jnp.broadcast_to(z_ref[i, :, lo:lo + n], (8, n))
        for h in range(H_R):
            q = _rotate(z8(h * DK_R, DK_R), cos, sin)
            k = _rotate(z8(QK_R + h * DK_R, DK_R), cos, sin) * (DK_R ** -0.5)
            v = z8(2 * QK_R + h * DV_R, DV_R)
            lanes = slice(h * DV_R, (h + 1) * DV_R)
            gate = z_ref[i, :, 2 * QK_R + V_R + h * DV_R:2 * QK_R + V_R + (h + 1) * DV_R]
            work.append((sr0_ref, sr_ref, or_ref, rgain_ref, i, h, lanes, q, gate,
                         1.0 - 2.0 ** (-5.0 - h), _dot_tn(first(k), v)))
        for h in range(H_G):
            lanes = slice(h * DK_G, (h + 1) * DK_G)
            q = _silu(z8(g0 + h * DK_G, DK_G))
            _, k, f = _forget_gate(z8(g0 + F_G + h * DK_G, DK_G), lb_all[:, lanes])
            v = z8(g0 + 2 * F_G + h * DV_G, DV_G)
            gate = z_ref[i, :, g0 + 2 * F_G + V_G + h * DV_G:g0 + 2 * F_G + V_G + (h + 1) * DV_G]
            f_hi = f.astype(BF16).astype(F32)
            f_mid = (f - f_hi).astype(BF16).astype(F32)
            f_lo = f - f_hi - f_mid
            lhs = jnp.where(row8 == 0, f_hi, jnp.where(row8 == 1, f_mid, jnp.where(row8 == 2, f_lo,
                            jnp.where(row8 == 3, k, 0.0))))
            rhs = jnp.concatenate([ones3, jnp.where(row8 == 3, v, 0.0)], axis=1)
            both = _dot_tn(lhs, rhs)
            work.append((sg0_ref, sg_ref, og_ref, ggain_ref, i, h, lanes, q, gate,
                         both[:, :DV_G], both[:, DV_G:]))
    outs = []
    for s0_ref, s_ref, _, _, i, h, _, q, _, decay, kv in work:
        s_new = decay * s0_ref[i, h] + kv
        s_ref[i, h] = s_new
        outs.append(_dot(q, s_new)[0:1, :])
    for (_, _, o_ref, gain_ref, i, _, lanes, _, gate, _, _), o in zip(work, outs):
        o_ref[i, :, lanes] = _head_norm_gate(o, gain_ref[:, lanes], gate)


def _step_specs(z, cos, sin, hg_lb, rgain, ggain, sr0, sg0, bb):
    n, d_in = z.shape
    row = lambda w: pl.BlockSpec((bb, 1, w), lambda i: (i, 0, 0))
    const = lambda a: pl.BlockSpec(a.shape, lambda i: (0,) * a.ndim)
    sr_spec = pl.BlockSpec((bb, H_R, DK_R, DV_R), lambda i: (i, 0, 0, 0))
    sg_spec = pl.BlockSpec((bb, H_G, DK_G, DV_G), lambda i: (i, 0, 0, 0))
    operands = (z.reshape(n, 1, d_in), cos, sin, hg_lb, rgain, ggain, sr0, sg0)
    in_specs = [row(d_in), const(cos), const(sin), const(hg_lb), const(rgain), const(ggain), sr_spec, sg_spec]
    out_specs = [row(V_R), row(V_G), sr_spec, sg_spec]
    out_shape = [
        jax.ShapeDtypeStruct((n, 1, V_R), F32),
        jax.ShapeDtypeStruct((n, 1, V_G), F32),
        jax.ShapeDtypeStruct(sr0.shape, F32),
        jax.ShapeDtypeStruct(sg0.shape, F32),
    ]
    return operands, in_specs, out_specs, out_shape


N_POST_IN = 16
N_STEP_IN = 8


def _post_kernel(*refs, bb):
    if bb:
        _post_body(*refs[:N_POST_IN], refs[N_POST_IN + N_STEP_IN])
        _step_kernel(*refs[N_POST_IN:N_POST_IN + N_STEP_IN], *refs[N_POST_IN + N_STEP_IN + 1:], bb=bb)
    else:
        _post_body(*refs)


def _post_body(x_ref, h_ref, gr_ref, gg_ref, p_ref, wa_ref, wur_ref, wug_ref, wo_ref,
               nm_ref, w1_ref, w2_ref, np_ref, wg_ref, wp_ref, nf_ref, y_ref):
    d = x_ref.shape[1]
    merge = _sigmoid(jnp.dot(h_ref[...], wa_ref[...], preferred_element_type=F32))
    u_r = jnp.dot(gr_ref[...].astype(BF16), wur_ref[...], preferred_element_type=F32)
    u_g = jnp.dot(gg_ref[...].astype(BF16), wug_ref[...], preferred_element_type=F32)
    m = merge[:, :d] * u_r + merge[:, d:] * u_g
    r = x_ref[...] + jnp.dot(m.astype(BF16), wo_ref[...], preferred_element_type=F32)
    hm = _rms(r, nm_ref[...]).astype(BF16)
    a = jnp.maximum(jnp.dot(hm, w1_ref[...], preferred_element_type=F32), 0.0)
    r = r + jnp.dot((a * a).astype(BF16), w2_ref[...], preferred_element_type=F32)
    hp = _rms(r, np_ref[...]).astype(BF16)
    gate = _sigmoid(jnp.dot(hp, wg_ref[...], preferred_element_type=F32))
    r = r + gate * jnp.dot(p_ref[...].astype(BF16), wp_ref[...], preferred_element_type=F32)
    y_ref[...] = _rms(r, nf_ref[...])


def _post(x, h, g_r, g_g, p, wa, wur, wug, wo, nm, w1, w2, npl, wg, wp, nf, tm, step=None):
    n, d = x.shape
    grid = n // tm
    rows = lambda w: pl.BlockSpec((tm, w), lambda i: (i, 0))
    const = lambda a: pl.BlockSpec(a.shape, lambda i: (0, 0), pipeline_mode=pl.Buffered(1))
    operands = (x, h, g_r, g_g, p, wa, wur, wug, wo, nm, w1, w2, npl, wg, wp, nf)
    in_specs = [rows(d), rows(d), rows(d), rows(d), rows(p.shape[1]),
                const(wa), const(wur), const(wug), const(wo), const(nm), const(w1), const(w2),
                const(npl), const(wg), const(wp), const(nf)]
    assert len(operands) == N_POST_IN
    out_specs = [rows(d)]
    out_shape = [jax.ShapeDtypeStruct((n, d), F32)]
    bb = 0
    if step is not None:
        bb = step[0].shape[0] // grid
        assert bb * grid == step[0].shape[0] and len(step) == N_STEP_IN
        s_operands, s_in, s_out, s_shape = _step_specs(*step, bb=bb)
        operands += s_operands
        in_specs += s_in
        out_specs += s_out
        out_shape += s_shape
    out = pl.pallas_call(
        functools.partial(_post_kernel, bb=bb),
        grid=(grid,),
        in_specs=in_specs,
        out_specs=out_specs,
        out_shape=out_shape,
        compiler_params=_params(("parallel",)),
        name="post",
    )(*operands)
    return out if bb else out[0]


def _rope_tables(pos):
    half = DK_R // 2
    inv = ROPE_BASE ** (-jnp.arange(half, dtype=F32) / half)
    ang = pos[:, None] * inv[None, :]
    cos, sin = jnp.cos(ang), jnp.sin(ang)
    return jnp.concatenate([cos, cos], axis=-1), jnp.concatenate([-sin, sin], axis=-1)


def kernel(x_prompt, x_sample, state_ret, state_hgrn, p_prompt, p_sample, norm_mix_g, w_in,
           ret_norm_g, hg_norm_g, hg_lb, w_up_ret, w_up_hg, w_o, norm_mlp_g, w_ff1, w_ff2,
           norm_ple_g, w_ple_gate, w_ple_proj, norm_final_g):
    b, t, d = x_prompt.shape
    nb, td, _ = x_sample.shape
    assert w_in.shape[0] == 1 and td == 1 and hg_lb.shape[0] == 2
    bf = lambda w: w[0].astype(BF16)
    row = lambda g: g.reshape(1, -1)
    n_mix = 2 * QK_R + 2 * V_R + 2 * F_G + 2 * V_G
    post_w = (w_in[0, :, n_mix:].astype(BF16), bf(w_up_ret), bf(w_up_hg), bf(w_o), row(norm_mlp_g[0]),
              bf(w_ff1), bf(w_ff2), row(norm_ple_g[0]), bf(w_ple_gate), bf(w_ple_proj), row(norm_final_g))
    g_mix = row(norm_mix_g[0])
    rgain = row(ret_norm_g[0])
    ggain = row(hg_norm_g[0])

    xs = x_sample.reshape(nb, d)
    hs = _norm(xs, g_mix, tm=nb)
    zs = _proj(hs, w_in, tm=nb, tn=1024)
    cos_s, sin_s = _rope_tables(PAST_LEN + jnp.arange(td, dtype=F32))
    step = (zs, cos_s, sin_s, hg_lb, rgain, ggain, state_ret[0], state_hgrn[0])

    xp = x_prompt.reshape(b * t, d)
    cos_p, sin_p = _rope_tables(jnp.arange(t, dtype=F32))
    gr_p, ret_p, hp3 = _retention(x_prompt, g_mix, w_in, cos_p, sin_p, rgain)
    gg_p, hg_p = _hgrn(hp3, w_in, hg_lb, ggain)
    y_p, gr_s, gg_s, ret_s, hg_s = _post(
        xp, hp3.reshape(b * t, d), gr_p.reshape(b * t, -1), gg_p.reshape(b * t, -1),
        p_prompt[0].reshape(b * t, -1), *post_w, tm=256, step=step)

    y_s = _post(xs, hs, gr_s.reshape(nb, -1), gg_s.reshape(nb, -1),
                p_sample[0].reshape(nb, -1), *post_w, tm=nb)

    return (y_p.reshape(b, t, d), y_s.reshape(nb, td, d), ret_p[None], hg_p[None],
            ret_s[None], hg_s[None])
```

```python
import functools

import jax
import jax.numpy as jnp
import numpy as np
from jax import lax
from jax.experimental import pallas as pl
from jax.experimental.pallas import tpu as pltpu

F32 = jnp.float32
BF16 = jnp.bfloat16

PAST_LEN = 16384
ROPE_BASE = 10000.0
EPS = 1e-6

H_R, DK_R, DV_R = 4, 128, 256
H_G, DK_G, DV_G = 8, 128, 128
QK_R = H_R * DK_R
V_R = H_R * DV_R
F_G = H_G * DK_G
V_G = H_G * DV_G

V7X_VMEM_BYTES = 64 * 1024 * 1024
VMEM_LIMIT = V7X_VMEM_BYTES - 8 * 1024 * 1024

NORM_ROWS = 256
RET_CHUNK = 128
RET_GROUP = 2
RET_LAG = 1
HG_CHUNK = 64
HG_SUB = 8
HG_GROUP = 4
HG_LAG = 2
HG_HEADS = 2

NT_DIMS = (((1,), (1,)), ((), ()))
TN_DIMS = (((0,), (0,)), ((), ()))


def _dot(a, b):
    return jnp.dot(a.astype(BF16), b.astype(BF16), preferred_element_type=F32)


def _dot_nt(a, b):
    return lax.dot_general(a.astype(BF16), b.astype(BF16), NT_DIMS, preferred_element_type=F32)


def _dot_tn(a, b):
    return lax.dot_general(a.astype(BF16), b.astype(BF16), TN_DIMS, preferred_element_type=F32)


def _rms(x, g):
    return x * lax.rsqrt(jnp.mean(x * x, axis=-1, keepdims=True) + EPS) * g


def _sigmoid(x):
    return 0.5 * jnp.tanh(0.5 * x) + 0.5


def _silu(x):
    half = 0.5 * x
    return half + half * jnp.tanh(half)


def _params(sem):
    return pltpu.CompilerParams(dimension_semantics=sem, vmem_limit_bytes=VMEM_LIMIT)


def _norm_kernel(x_ref, g_ref, h_ref):
    h_ref[...] = _rms(x_ref[...], g_ref[...]).astype(BF16)


def _norm(x, g, tm):
    n, d = x.shape
    return pl.pallas_call(
        _norm_kernel,
        grid=(n // tm,),
        in_specs=[pl.BlockSpec((tm, d), lambda i: (i, 0)), pl.BlockSpec((1, d), lambda i: (0, 0))],
        out_specs=pl.BlockSpec((tm, d), lambda i: (i, 0)),
        out_shape=jax.ShapeDtypeStruct((n, d), BF16),
        compiler_params=_params(("parallel",)),
        name="norm",
    )(x, g)


def _proj_kernel(h_ref, w_ref, z_ref):
    z_ref[...] = jnp.dot(h_ref[...], w_ref[...].astype(BF16), preferred_element_type=F32)


def _proj(h, w_in, tm, tn):
    n, d = h.shape
    d_out = w_in.shape[2]
    return pl.pallas_call(
        _proj_kernel,
        grid=(n // tm, d_out // tn),
        in_specs=[
            pl.BlockSpec((tm, d), lambda i, j: (i, 0)),
            pl.BlockSpec((None, d, tn), lambda i, j: (0, 0, j)),
        ],
        out_specs=pl.BlockSpec((tm, tn), lambda i, j: (i, j)),
        out_shape=jax.ShapeDtypeStruct((n, d_out), F32),
        compiler_params=_params(("parallel", "arbitrary")),
        name="proj",
    )(h, w_in)


def _rotate(x, cos, sin_signed):
    return x * cos + pltpu.roll(x, DK_R // 2, 1) * sin_signed


def _head_norm_gate(o, gain, gate):
    return o * lax.rsqrt(jnp.mean(o * o, axis=-1, keepdims=True) + EPS) * gain * _silu(gate)


def _ret_pipeline(h_ref, wq_ref, wk_ref, wv_ref, wg_ref, cos_ref, sin_ref, dm_ref, qd_ref, kd_ref,
                  cd_ref, gain_ref, o_ref, s_ref, w_sc, z_sc, *, chunk, group, lag):
    n_chunks = h_ref.shape[0] // chunk
    assert n_chunks % group == 0 and lag >= 1
    g_rows = group * chunk
    q_cols = slice(0, DK_R)
    k_cols = slice(DK_R, 2 * DK_R)
    v_cols = slice(2 * DK_R, 2 * DK_R + DV_R)
    g_cols = slice(2 * DK_R + DV_R, 2 * DK_R + 2 * DV_R)
    for cols, w_ref in ((q_cols, wq_ref), (k_cols, wk_ref), (v_cols, wv_ref), (g_cols, wg_ref)):
        w_sc[:, cols] = w_ref[...].astype(BF16)
    dm = dm_ref[...]
    qd = qd_ref[...]
    kd = kd_ref[...]
    cd = cd_ref[...]
    gain = gain_ref[...]

    assert group == 2
    pieces = (slice(0, g_cols.start), g_cols)

    def project(c):
        rows = slice(c // group * g_rows, (c // group + 1) * g_rows)
        pcols = pieces[c % group]
        z_sc[rows, pcols] = jnp.dot(h_ref[rows, :], w_sc[:, pcols], preferred_element_type=F32)

    def scores(r0):
        rows = slice(r0, r0 + chunk)
        cos = cos_ref[rows, :]
        sin = sin_ref[rows, :]
        q = _rotate(z_sc[rows, q_cols], cos, sin)
        k = _rotate(z_sc[rows, k_cols], cos, sin) * (DK_R ** -0.5)
        a = (_dot_nt(q, k) * dm).astype(BF16)
        return a, (q * qd).astype(BF16), (k * kd).astype(BF16)

    def outputs(r0, parts, s):
        a, q_in, k_st = parts
        rows = slice(r0, r0 + chunk)
        v = z_sc[rows, v_cols].astype(BF16)
        o = jnp.dot(jnp.concatenate([a, q_in], axis=1), jnp.concatenate([v, s.astype(BF16)], axis=0),
                    preferred_element_type=F32)
        o_ref[rows, :] = _head_norm_gate(o, gain, z_sc[rows, g_cols])
        return cd * s + lax.dot_general(k_st, v, TN_DIMS, preferred_element_type=F32)

    carry = {"s": jnp.zeros((DK_R, DV_R), F32)}
    pending = {}

    def do_scores(c):
        pending[c] = scores(c * chunk)

    def do_outputs(c):
        carry["s"] = outputs(c * chunk, pending.pop(c), carry["s"])

    work = []
    for step in range(-group, n_chunks + lag):
        if step + group < n_chunks:
            work.append(functools.partial(project, step + group))
        if 0 <= step < n_chunks:
            work.append(functools.partial(do_scores, step))
        if 0 <= step - lag < n_chunks:
            work.append(functools.partial(do_outputs, step - lag))

    def finish():
        s_ref[...] = carry["s"]

    return work, finish


def _retention_tables(c):
    lg = np.log(1.0 - 2.0 ** (-5.0 - np.arange(H_R, dtype=np.float64)))
    idx = np.arange(c, dtype=np.float64)
    rel = idx[:, None] - idx[None, :]
    dm = np.where(rel >= 0, np.exp(lg[:, None, None] * np.maximum(rel, 0.0)), 0.0)
    qd = np.broadcast_to(np.exp(lg[:, None] * (idx + 1.0))[:, :, None], (H_R, c, DK_R))
    kd = np.broadcast_to(np.exp(lg[:, None] * (c - 1.0 - idx))[:, :, None], (H_R, c, DK_R))
    cd = np.broadcast_to(np.exp(lg * c)[:, None, None], (H_R, 1, DV_R))
    return tuple(jnp.asarray(a, F32) for a in (dm, qd, kd, cd))


def _lower_bound(lb_ref):
    a = lb_ref[...]
    a0, a1 = a[0:1, :], a[1:2, :]
    m = jnp.maximum(a0, a1)
    e0 = jnp.exp(a0 - m)
    e1 = jnp.exp(a1 - m)
    return e0 / (e0 + e1)


def _forget_gate(gf, lb):
    sg = jax.nn.sigmoid(gf)
    f = lb + (1.0 - lb) * sg
    return jnp.log(f), (1.0 - lb) * (1.0 - sg), f


def _cumsum_rows(tri, x):
    hi = x.astype(BF16)
    rest = x - hi.astype(F32)
    mid = rest.astype(BF16)
    lo = (rest - mid.astype(F32)).astype(BF16)
    y = jnp.dot(tri, jnp.concatenate([hi, mid, lo], axis=1), preferred_element_type=F32)
    n = x.shape[1]
    return y[:, :n] + y[:, n:2 * n] + y[:, 2 * n:]


def _hgrn_pipeline(h_ref, wq_ref, wf_ref, wi_ref, wg_ref, lb_ref, gain_ref, o_ref, s_ref,
                   w_sc, z_sc, q_sc, b_sc, c_sc, *, chunk, sub, group, lag, heads):
    assert sub == 8 and chunk % (2 * sub) == 0
    g_rows = group * chunk
    n_chunks = h_ref.shape[0] // chunk
    assert n_chunks % group == 0 and lag >= 1
    lanes = lambda hh: slice(hh * DK_G, (hh + 1) * DK_G)
    cols = lambda hh, role: lanes(4 * hh + role)
    for hh in range(heads):
        for role, w_ref in enumerate((wq_ref, wf_ref, wi_ref, wg_ref)):
            w_sc[:, cols(hh, role)] = w_ref[:, lanes(hh)].astype(BF16)

    lb = _lower_bound(lb_ref)
    half_scale = 0.5 * (1.0 - lb)
    f_mid = lb + half_scale
    gain = gain_ref[...]
    r64 = lax.broadcasted_iota(jnp.int32, (chunk, chunk), 0)
    c64 = lax.broadcasted_iota(jnp.int32, (chunk, chunk), 1)
    tri = jnp.where(r64 >= c64, 1.0, 0.0).astype(BF16)
    same_sub = (r64 // sub) == (c64 // sub)
    place = [(c64[:sub] % sub == s) & (r64[:sub] >= s) for s in range(sub)]
    zrows = lambda n: jnp.zeros((n, DK_G), F32)

    piece_cols = z_sc.shape[1] // group
    assert piece_cols % DK_G == 0

    def project(c):
        rows = slice(c // group * g_rows, (c // group + 1) * g_rows)
        pcols = slice(c % group * piece_cols, (c % group + 1) * piece_cols)
        z_sc[rows, pcols] = jnp.dot(h_ref[rows, :], w_sc[:, pcols], preferred_element_type=F32)

    def gates(hh, r0):
        rows = slice(r0, r0 + chunk)
        th_scaled = half_scale[:, lanes(hh)] * jnp.tanh(0.5 * z_sc[rows, cols(hh, 1)])
        b_sc[rows, lanes(hh)] = _cumsum_rows(tri, jnp.log2(f_mid[:, lanes(hh)] + th_scaled))
        c_sc[rows, lanes(hh)] = b_sc[rows, lanes(hh)] - jnp.log2(half_scale[:, lanes(hh)] - th_scaled)
        q_sc[rows, lanes(hh)] = _silu(z_sc[rows, cols(hh, 0)])

    def diagonal(hh, r0):
        blocks = []
        for o in range(r0, r0 + chunk, sub):
            q, b = q_sc[o:o + sub, lanes(hh)], b_sc[o:o + sub, lanes(hh)]
            acc = jnp.zeros((sub, chunk), F32)
            for s in range(sub):
                p = jnp.exp2(b - c_sc[o + s:o + s + 1, lanes(hh)]) * q
                acc = jnp.where(place[s], jnp.sum(p, axis=-1, keepdims=True), acc)
            blocks.append(acc)
        return jnp.where(same_sub, jnp.concatenate(blocks, axis=0), 0.0)

    def below(hh, r0, w):
        qs, ks = [], []
        for mid in range(r0 + w, r0 + chunk, 2 * w):
            ref_b = b_sc[mid - 1:mid, lanes(hh)]
            qs += [zrows(w), q_sc[mid:mid + w, lanes(hh)] * jnp.exp2(b_sc[mid:mid + w, lanes(hh)] - ref_b)]
            ks += [jnp.exp2(ref_b - c_sc[mid - w:mid, lanes(hh)]), zrows(w)]
        scores = _dot_nt(jnp.concatenate(qs, axis=0), jnp.concatenate(ks, axis=0))
        return jnp.where(c64 // w == r64 // w - 1, scores, 0.0)

    def scores(hh, r0):
        a = diagonal(hh, r0)
        w = chunk // 2
        while w >= sub:
            a = a + below(hh, r0, w)
            w //= 2
        return a

    def outputs(hh, r0, a, st):
        rows = slice(r0, r0 + chunk)
        v = z_sc[rows, cols(hh, 2)].astype(BF16)
        b_last = b_sc[rows.stop - 1:rows.stop, lanes(hh)]
        o = _dot(a, v) + _dot_nt(q_sc[rows, lanes(hh)] * jnp.exp2(b_sc[rows, lanes(hh)]), st)
        o_ref[rows, lanes(hh)] = _head_norm_gate(o, gain[:, lanes(hh)], z_sc[rows, cols(hh, 3)])
        return st * jnp.exp2(b_last) + _dot_tn(v, jnp.exp2(b_last - c_sc[rows, lanes(hh)]))

    st = [jnp.zeros((DV_G, DK_G), F32) for _ in range(heads)]
    pending = {}

    def do_scores(hh, c):
        pending[hh, c] = scores(hh, c * chunk)

    def do_outputs(hh, c):
        st[hh] = outputs(hh, c * chunk, pending.pop((hh, c)), st[hh])

    work = []
    for step in range(-group, n_chunks + 2 * lag):
        if step + group < n_chunks:
            work.append(functools.partial(project, step + group))
        for hh in range(heads):
            if 0 <= step < n_chunks:
                work.append(functools.partial(gates, hh, step * chunk))
            if 0 <= step - lag < n_chunks:
                work.append(functools.partial(do_scores, hh, step - lag))
            if 0 <= step - 2 * lag < n_chunks:
                work.append(functools.partial(do_outputs, hh, step - 2 * lag))

    def finish():
        for hh in range(heads):
            s_ref[hh] = st[hh].T

    return work, finish


def _run_pipeline(build, **config):
    def kernel(*refs):
        work, finish = build(*refs, **config)
        for item in work:
            item()
        finish()
    return kernel


N_RET_IN = 11


def _ret_kernel(x_ref, g_ref, *refs, norm_rows, n_cast, **config):
    ret_in, cast_in = refs[:N_RET_IN], refs[N_RET_IN:N_RET_IN + n_cast]
    outs = refs[N_RET_IN + n_cast:]
    (o_ref, s_ref, hn_ref), cast_out, scratch = outs[:3], outs[3:3 + n_cast], outs[3 + n_cast:]

    @pl.when(pl.program_id(1) == 0)
    def _():
        def body(i, carry):
            rows = pl.ds(pl.multiple_of(i * norm_rows, norm_rows), norm_rows)
            hn_ref[rows, :] = _rms(x_ref[rows, :], g_ref[...]).astype(BF16)
            return carry

        lax.fori_loop(0, x_ref.shape[0] // norm_rows, body, 0)

    for src, dst in zip(cast_in, cast_out):
        dst[...] = src[...].astype(BF16)
    work, finish = _ret_pipeline(hn_ref, *ret_in, o_ref, s_ref, *scratch, **config)
    for item in work:
        item()
    finish()


def _retention(x3, g_mix, w_in, cos, sin, gain, casts):
    b, t, d = x3.shape
    n_head = 2 * DK_R + 2 * DV_R
    steps = b * H_R
    w_block = lambda width, lo: pl.BlockSpec((None, d, width), lambda bi, h: (0, 0, lo // width + h))
    head_tab = lambda rows, cols: pl.BlockSpec((None, rows, cols), lambda bi, h: (h, 0, 0))
    cast_in, cast_out, cast_shape = [], [], []
    for a, axis, col_block in casts:
        _, rows, cols = a.shape
        if axis == 1:
            n = rows // steps
            cols = cols if col_block is None else d
            cb = 0 if col_block is None else col_block
            cast_in.append(pl.BlockSpec((None, n, cols), lambda bi, h, cb=cb: (0, bi * H_R + h, cb)))
            cast_out.append(pl.BlockSpec((n, cols), lambda bi, h: (bi * H_R + h, 0)))
        else:
            n = cols // steps
            cast_in.append(pl.BlockSpec((None, rows, n), lambda bi, h: (0, 0, bi * H_R + h)))
            cast_out.append(pl.BlockSpec((rows, n), lambda bi, h: (0, bi * H_R + h)))
        cast_shape.append(jax.ShapeDtypeStruct((rows, cols), BF16))
    return pl.pallas_call(
        functools.partial(_ret_kernel, norm_rows=NORM_ROWS, n_cast=len(casts),
                          chunk=RET_CHUNK, group=RET_GROUP, lag=RET_LAG),
        grid=(b, H_R),
        in_specs=[
            pl.BlockSpec((None, t, d), lambda bi, h: (bi, 0, 0)),
            pl.BlockSpec((1, d), lambda bi, h: (0, 0)),
            w_block(DK_R, 0), w_block(DK_R, QK_R), w_block(DV_R, 2 * QK_R), w_block(DV_R, 2 * QK_R + V_R),
            pl.BlockSpec((t, DK_R), lambda bi, h: (0, 0)),
            pl.BlockSpec((t, DK_R), lambda bi, h: (0, 0)),
            head_tab(RET_CHUNK, RET_CHUNK), head_tab(RET_CHUNK, DK_R), head_tab(RET_CHUNK, DK_R),
            head_tab(1, DV_R),
            pl.BlockSpec((1, DV_R), lambda bi, h: (0, h)),
        ] + cast_in,
        out_specs=[
            pl.BlockSpec((None, t, DV_R), lambda bi, h: (bi, 0, h)),
            pl.BlockSpec((None, None, DK_R, DV_R), lambda bi, h: (bi, h, 0, 0)),
            pl.BlockSpec((None, t, d), lambda bi, h: (bi, 0, 0)),
        ] + cast_out,
        out_shape=[
            jax.ShapeDtypeStruct((b, t, V_R), F32),
            jax.ShapeDtypeStruct((b, H_R, DK_R, DV_R), F32),
            jax.ShapeDtypeStruct((b, t, d), BF16),
        ] + cast_shape,
        scratch_shapes=[pltpu.VMEM((d, n_head), BF16), pltpu.VMEM((t, n_head), F32)],
        compiler_params=_params(("parallel", "arbitrary")),
        name="retention",
    )(x3, g_mix, w_in, w_in, w_in, w_in, cos, sin, *_retention_tables(RET_CHUNK), gain,
      *(a for a, _, _ in casts))


def _hgrn(h3, w_in, hg_lb, gain):
    b, t, d = h3.shape
    heads = HG_HEADS
    wide = heads * DK_G
    first = (2 * QK_R + 2 * V_R) // wide
    w_block = lambda role: pl.BlockSpec((None, d, wide), lambda bi, h: (0, 0, first + role * (H_G // heads) + h))
    return pl.pallas_call(
        _run_pipeline(_hgrn_pipeline, chunk=HG_CHUNK, sub=HG_SUB, group=HG_GROUP, lag=HG_LAG, heads=heads),
        grid=(b, H_G // heads),
        in_specs=[
            pl.BlockSpec((None, t, d), lambda bi, h: (bi, 0, 0)),
            w_block(0), w_block(1), w_block(2), w_block(3),
            pl.BlockSpec((2, wide), lambda bi, h: (0, h)),
            pl.BlockSpec((1, wide), lambda bi, h: (0, h)),
        ],
        out_specs=[
            pl.BlockSpec((None, t, wide), lambda bi, h: (bi, 0, h)),
            pl.BlockSpec((None, heads, DK_G, DV_G), lambda bi, h: (bi, h, 0, 0)),
        ],
        out_shape=[
            jax.ShapeDtypeStruct((b, t, V_G), F32),
            jax.ShapeDtypeStruct((b, H_G, DK_G, DV_G), F32),
        ],
        scratch_shapes=[
            pltpu.VMEM((d, 4 * wide), BF16),
            pltpu.VMEM((t, 4 * wide), F32),
            pltpu.VMEM((t, wide), F32),
            pltpu.VMEM((t, wide), F32),
            pltpu.VMEM((t, wide), F32),
        ],
        compiler_params=_params(("parallel", "arbitrary")),
        name="hgrn",
    )(h3, w_in, w_in, w_in, w_in, hg_lb, gain)


def _step_kernel(z_ref, cos_ref, sin_ref, lb_ref, rgain_ref, ggain_ref, sr0_ref, sg0_ref,
                 or_ref, og_ref, sr_ref, sg_ref, *, bb):
    first = lambda a: jnp.where(lax.broadcasted_iota(jnp.int32, a.shape, 0) == 0, a, 0.0)
    cos = cos_ref[...]
    sin = sin_ref[...]
    lb_all = _lower_bound(lb_ref)
    row8 = lax.broadcasted_iota(jnp.int32, (8, DK_G), 0)
    ones3 = jnp.where(row8 < 3, 1.0, 0.0)
    g0 = 2 * QK_R + 2 * V_R
    base = (pl.program_id(0) % (8 // bb)) * bb
    work = []
    for i in range(bb):
        def z_row(lo, n, i=i):
            mine = lax.broadcasted_iota(jnp.int32, (8, n), 0) == base + i
            return jnp.sum(jnp.where(mine, z_ref[:, lo:lo + n], 0.0), axis=0, keepdims=True)

        z8 = lambda lo, n, z_row=z_row: jnp.broadcast_to(z_row(lo, n), (8, n))
        for h in range(H_R):
            q = _rotate(z8(h * DK_R, DK_R), cos, sin)
            k = _rotate(z8(QK_R + h * DK_R, DK_R), cos, sin) * (DK_R ** -0.5)
            v = z8(2 * QK_R + h * DV_R, DV_R)
            lanes = slice(h * DV_R, (h + 1) * DV_R)
            gate = z_row(2 * QK_R + V_R + h * DV_R, DV_R)
            work.append((sr0_ref, sr_ref, or_ref, rgain_ref, i, h, lanes, q, gate,
                         1.0 - 2.0 ** (-5.0 - h), _dot_tn(first(k), v)))
        for h in range(H_G):
            lanes = slice(h * DK_G, (h + 1) * DK_G)
            q = _silu(z8(g0 + h * DK_G, DK_G))
            _, k, f = _forget_gate(z8(g0 + F_G + h * DK_G, DK_G), lb_all[:, lanes])
            v = z8(g0 + 2 * F_G + h * DV_G, DV_G)
            gate = z_row(g0 + 2 * F_G + V_G + h * DV_G, DV_G)
            f_hi = f.astype(BF16).astype(F32)
            f_mid = (f - f_hi).astype(BF16).astype(F32)
            f_lo = f - f_hi - f_mid
            lhs = jnp.where(row8 == 0, f_hi, jnp.where(row8 == 1, f_mid, jnp.where(row8 == 2, f_lo,
                            jnp.where(row8 == 3, k, 0.0))))
            rhs = jnp.concatenate([ones3, jnp.where(row8 == 3, v, 0.0)], axis=1)
            both = _dot_tn(lhs, rhs)
            work.append((sg0_ref, sg_ref, og_ref, ggain_ref, i, h, lanes, q, gate,
                         both[:, :DV_G], both[:, DV_G:]))
    outs = []
    for s0_ref, s_ref, _, _, i, h, _, q, _, decay, kv in work:
        s_new = decay * s0_ref[i, h] + kv
        s_ref[i, h] = s_new
        outs.append(_dot(q, s_new)[0:1, :])
    for (_, _, o_ref, gain_ref, i, _, lanes, _, gate, _, _), o in zip(work, outs):
        o_ref[i, :, lanes] = _head_norm_gate(o, gain_ref[:, lanes], gate)


def _step_specs(z, cos, sin, hg_lb, rgain, ggain, sr0, sg0, bb):
    n, d_in = z.shape
    assert 8 % bb == 0
    row = lambda w: pl.BlockSpec((bb, 1, w), lambda i: (i, 0, 0))
    const = lambda a: pl.BlockSpec(a.shape, lambda i: (0,) * a.ndim)
    sr_spec = pl.BlockSpec((bb, H_R, DK_R, DV_R), lambda i: (i, 0, 0, 0))
    sg_spec = pl.BlockSpec((bb, H_G, DK_G, DV_G), lambda i: (i, 0, 0, 0))
    operands = (z, cos, sin, hg_lb, rgain, ggain, sr0, sg0)
    in_specs = [pl.BlockSpec((8, d_in), lambda i: (i * bb // 8, 0)),
                const(cos), const(sin), const(hg_lb), const(rgain), const(ggain), sr_spec, sg_spec]
    out_specs = [row(V_R), row(V_G), sr_spec, sg_spec]
    out_shape = [
        jax.ShapeDtypeStruct((n, 1, V_R), F32),
        jax.ShapeDtypeStruct((n, 1, V_G), F32),
        jax.ShapeDtypeStruct(sr0.shape, F32),
        jax.ShapeDtypeStruct(sg0.shape, F32),
    ]
    return operands, in_specs, out_specs, out_shape


N_POST_IN = 17
N_STEP_IN = 8


def _post_kernel(*refs, bb):
    if bb:
        _post_body(*refs[:N_POST_IN], refs[N_POST_IN + N_STEP_IN])
        _step_kernel(*refs[N_POST_IN:N_POST_IN + N_STEP_IN], *refs[N_POST_IN + N_STEP_IN + 1:], bb=bb)
    else:
        _post_body(*refs)


def _post_body(x_ref, h_ref, gr_ref, gg_ref, p_ref, war_ref, wag_ref, wur_ref, wug_ref, wo_ref,
               nm_ref, w1_ref, w2_ref, np_ref, wg_ref, wp_ref, nf_ref, y_ref):
    h = h_ref[...]
    u_r = jnp.dot(gr_ref[...].astype(BF16), wur_ref[...], preferred_element_type=F32)
    u_g = jnp.dot(gg_ref[...].astype(BF16), wug_ref[...], preferred_element_type=F32)
    m = (_sigmoid(jnp.dot(h, war_ref[...], preferred_element_type=F32)) * u_r
         + _sigmoid(jnp.dot(h, wag_ref[...], preferred_element_type=F32)) * u_g)
    r = x_ref[...] + jnp.dot(m.astype(BF16), wo_ref[...], preferred_element_type=F32)
    hm = _rms(r, nm_ref[...]).astype(BF16)
    a = jnp.maximum(jnp.dot(hm, w1_ref[...], preferred_element_type=F32), 0.0)
    r = r + jnp.dot((a * a).astype(BF16), w2_ref[...], preferred_element_type=F32)
    hp = _rms(r, np_ref[...]).astype(BF16)
    gate = _sigmoid(jnp.dot(hp, wg_ref[...], preferred_element_type=F32))
    r = r + gate * jnp.dot(p_ref[...].astype(BF16), wp_ref[...], preferred_element_type=F32)
    y_ref[...] = _rms(r, nf_ref[...])


def _post(x, h, g_r, g_g, p, war, wag, wur, wug, wo, nm, w1, w2, npl, wg, wp, nf, tm, step=None):
    n, d = x.shape
    grid = n // tm
    rows = lambda w: pl.BlockSpec((tm, w), lambda i: (i, 0))
    const = lambda a: pl.BlockSpec(a.shape, lambda i: (0, 0), pipeline_mode=pl.Buffered(1))
    operands = (x, h, g_r, g_g, p, war, wag, wur, wug, wo, nm, w1, w2, npl, wg, wp, nf)
    in_specs = [rows(d), rows(d), rows(d), rows(d), rows(p.shape[1]),
                const(war), const(wag), const(wur), const(wug), const(wo), const(nm), const(w1), const(w2),
                const(npl), const(wg), const(wp), const(nf)]
    assert len(operands) == N_POST_IN
    out_specs = [rows(d)]
    out_shape = [jax.ShapeDtypeStruct((n, d), F32)]
    bb = 0
    if step is not None:
        bb = step[0].shape[0] // grid
        assert bb * grid == step[0].shape[0] and len(step) == N_STEP_IN
        s_operands, s_in, s_out, s_shape = _step_specs(*step, bb=bb)
        operands += s_operands
        in_specs += s_in
        out_specs += s_out
        out_shape += s_shape
    out = pl.pallas_call(
        functools.partial(_post_kernel, bb=bb),
        grid=(grid,),
        in_specs=in_specs,
        out_specs=out_specs,
        out_shape=out_shape,
        compiler_params=_params(("parallel",)),
        name="post",
    )(*operands)
    return out if bb else out[0]


def _rope_tables(pos):
    half = DK_R // 2
    inv = ROPE_BASE ** (-jnp.arange(half, dtype=F32) / half)
    ang = pos[:, None] * inv[None, :]
    cos, sin = jnp.cos(ang), jnp.sin(ang)
    return jnp.concatenate([cos, cos], axis=-1), jnp.concatenate([-sin, sin], axis=-1)


def kernel(x_prompt, x_sample, state_ret, state_hgrn, p_prompt, p_sample, norm_mix_g, w_in,
           ret_norm_g, hg_norm_g, hg_lb, w_up_ret, w_up_hg, w_o, norm_mlp_g, w_ff1, w_ff2,
           norm_ple_g, w_ple_gate, w_ple_proj, norm_final_g):
    b, t, d = x_prompt.shape
    nb, td, _ = x_sample.shape
    assert w_in.shape[0] == 1 and td == 1 and hg_lb.shape[0] == 2
    row = lambda g: g.reshape(1, -1)
    merge_block = (2 * QK_R + 2 * V_R + 2 * F_G + 2 * V_G) // d
    g_mix = row(norm_mix_g[0])
    rgain = row(ret_norm_g[0])
    ggain = row(hg_norm_g[0])

    xs = x_sample.reshape(nb, d)
    hs = _norm(xs, g_mix, tm=nb)
    zs = _proj(hs, w_in, tm=nb, tn=1024)
    cos_s, sin_s = _rope_tables(PAST_LEN + jnp.arange(td, dtype=F32))
    step = (zs, cos_s, sin_s, hg_lb, rgain, ggain, state_ret[0], state_hgrn[0])

    xp = x_prompt.reshape(b * t, d)
    cos_p, sin_p = _rope_tables(jnp.arange(t, dtype=F32))
    casts = [(w_in, 1, merge_block), (w_in, 1, merge_block + 1), (w_up_ret, 1, None), (w_up_hg, 1, None),
             (w_o, 1, None), (w_ff1, 2, None), (w_ff2, 1, None), (w_ple_gate, 1, None)]
    gr_p, ret_p, hp3, war, wag, wur, wug, wo, w1, w2, wg = _retention(
        x_prompt, g_mix, w_in, cos_p, sin_p, rgain, casts)
    post_w = (war, wag, wur, wug, wo, row(norm_mlp_g[0]), w1, w2, row(norm_ple_g[0]), wg,
              w_ple_proj[0].astype(BF16), row(norm_final_g))
    gg_p, hg_p = _hgrn(hp3, w_in, hg_lb, ggain)
    y_p, gr_s, gg_s, ret_s, hg_s = _post(
        xp, hp3.reshape(b * t, d), gr_p.reshape(b * t, -1), gg_p.reshape(b * t, -1),
        p_prompt[0].reshape(b * t, -1), *post_w, tm=256, step=step)

    y_s = _post(xs, hs, gr_s.reshape(nb, -1), gg_s.reshape(nb, -1),
                p_sample[0].reshape(nb, -1), *post_w, tm=nb)

    return (y_p.reshape(b, t, d), y_s.reshape(nb, td, d), ret_p[None], hg_p[None],
            ret_s[None], hg_s[None])
```

```python
import functools

import jax
import jax.numpy as jnp
import numpy as np
from jax import lax
from jax.experimental import pallas as pl
from jax.experimental.pallas import tpu as pltpu

F32 = jnp.float32
BF16 = jnp.bfloat16

PAST_LEN = 16384
ROPE_BASE = 10000.0
EPS = 1e-6

H_R, DK_R, DV_R = 4, 128, 256
H_G, DK_G, DV_G = 8, 128, 128
QK_R = H_R * DK_R
V_R = H_R * DV_R
F_G = H_G * DK_G
V_G = H_G * DV_G

V7X_VMEM_BYTES = 64 * 1024 * 1024
VMEM_LIMIT = V7X_VMEM_BYTES - 8 * 1024 * 1024

NORM_ROWS = 256
RET_CHUNK = 128
RET_GROUP = 2
RET_LAG = 1
HG_CHUNK = 64
HG_SUB = 8
HG_GROUP = 4
HG_LAG = 2
HG_HEADS = 2

NT_DIMS = (((1,), (1,)), ((), ()))
TN_DIMS = (((0,), (0,)), ((), ()))


def _dot(a, b):
    return jnp.dot(a.astype(BF16), b.astype(BF16), preferred_element_type=F32)


def _dot_nt(a, b):
    return lax.dot_general(a.astype(BF16), b.astype(BF16), NT_DIMS, preferred_element_type=F32)


def _dot_tn(a, b):
    return lax.dot_general(a.astype(BF16), b.astype(BF16), TN_DIMS, preferred_element_type=F32)


def _rms(x, g):
    return x * lax.rsqrt(jnp.mean(x * x, axis=-1, keepdims=True) + EPS) * g


def _sigmoid(x):
    return 0.5 * jnp.tanh(0.5 * x) + 0.5


def _silu(x):
    half = 0.5 * x
    return half + half * jnp.tanh(half)


def _params(sem):
    return pltpu.CompilerParams(dimension_semantics=sem, vmem_limit_bytes=VMEM_LIMIT)


def _norm_kernel(x_ref, g_ref, h_ref):
    h_ref[...] = _rms(x_ref[...], g_ref[...]).astype(BF16)


def _norm(x, g, tm):
    n, d = x.shape
    return pl.pallas_call(
        _norm_kernel,
        grid=(n // tm,),
        in_specs=[pl.BlockSpec((tm, d), lambda i: (i, 0)), pl.BlockSpec((1, d), lambda i: (0, 0))],
        out_specs=pl.BlockSpec((tm, d), lambda i: (i, 0)),
        out_shape=jax.ShapeDtypeStruct((n, d), BF16),
        compiler_params=_params(("parallel",)),
        name="norm",
    )(x, g)


def _proj_kernel(h_ref, w_ref, z_ref):
    z_ref[...] = jnp.dot(h_ref[...], w_ref[...].astype(BF16), preferred_element_type=F32)


def _proj(h, w_in, tm, tn):
    n, d = h.shape
    d_out = w_in.shape[2]
    return pl.pallas_call(
        _proj_kernel,
        grid=(n // tm, d_out // tn),
        in_specs=[
            pl.BlockSpec((tm, d), lambda i, j: (i, 0)),
            pl.BlockSpec((None, d, tn), lambda i, j: (0, 0, j)),
        ],
        out_specs=pl.BlockSpec((tm, tn), lambda i, j: (i, j)),
        out_shape=jax.ShapeDtypeStruct((n, d_out), F32),
        compiler_params=_params(("parallel", "arbitrary")),
        name="proj",
    )(h, w_in)


def _rotate(x, cos, sin_signed):
    return x * cos + pltpu.roll(x, DK_R // 2, 1) * sin_signed


def _head_norm_gate(o, gain, gate):
    return o * lax.rsqrt(jnp.mean(o * o, axis=-1, keepdims=True) + EPS) * gain * _silu(gate)


def _ret_pipeline(h_ref, wq_ref, wk_ref, wv_ref, wg_ref, cos_ref, sin_ref, dm_ref, qd_ref, kd_ref,
                  cd_ref, gain_ref, o_ref, s_ref, w_sc, z_sc, *, chunk, group, lag):
    n_chunks = h_ref.shape[0] // chunk
    assert n_chunks % group == 0 and lag >= 1
    g_rows = group * chunk
    q_cols = slice(0, DK_R)
    k_cols = slice(DK_R, 2 * DK_R)
    v_cols = slice(2 * DK_R, 2 * DK_R + DV_R)
    g_cols = slice(2 * DK_R + DV_R, 2 * DK_R + 2 * DV_R)
    for cols, w_ref in ((q_cols, wq_ref), (k_cols, wk_ref), (v_cols, wv_ref), (g_cols, wg_ref)):
        w_sc[:, cols] = w_ref[...].astype(BF16)
    dm = dm_ref[...]
    qd = qd_ref[...]
    kd = kd_ref[...]
    cd = cd_ref[...]
    gain = gain_ref[...]

    assert group == 2
    pieces = (slice(0, g_cols.start), g_cols)

    def project(c):
        rows = slice(c // group * g_rows, (c // group + 1) * g_rows)
        pcols = pieces[c % group]
        z_sc[rows, pcols] = jnp.dot(h_ref[rows, :], w_sc[:, pcols], preferred_element_type=F32)

    def scores(r0):
        rows = slice(r0, r0 + chunk)
        cos = cos_ref[rows, :]
        sin = sin_ref[rows, :]
        q = _rotate(z_sc[rows, q_cols], cos, sin)
        k = _rotate(z_sc[rows, k_cols], cos, sin) * (DK_R ** -0.5)
        a = (_dot_nt(q, k) * dm).astype(BF16)
        return a, (q * qd).astype(BF16), (k * kd).astype(BF16)

    def outputs(r0, parts, s):
        a, q_in, k_st = parts
        rows = slice(r0, r0 + chunk)
        v = z_sc[rows, v_cols].astype(BF16)
        o = jnp.dot(jnp.concatenate([a, q_in], axis=1), jnp.concatenate([v, s.astype(BF16)], axis=0),
                    preferred_element_type=F32)
        o_ref[rows, :] = _head_norm_gate(o, gain, z_sc[rows, g_cols])
        return cd * s + lax.dot_general(k_st, v, TN_DIMS, preferred_element_type=F32)

    carry = {"s": jnp.zeros((DK_R, DV_R), F32)}
    pending = {}

    def do_scores(c):
        pending[c] = scores(c * chunk)

    def do_outputs(c):
        carry["s"] = outputs(c * chunk, pending.pop(c), carry["s"])

    work = []
    for step in range(-group, n_chunks + lag):
        if step + group < n_chunks:
            work.append(functools.partial(project, step + group))
        if 0 <= step < n_chunks:
            work.append(functools.partial(do_scores, step))
        if 0 <= step - lag < n_chunks:
            work.append(functools.partial(do_outputs, step - lag))

    def finish():
        s_ref[...] = carry["s"]

    return work, finish


def _retention_tables(c):
    lg = np.log(1.0 - 2.0 ** (-5.0 - np.arange(H_R, dtype=np.float64)))
    idx = np.arange(c, dtype=np.float64)
    rel = idx[:, None] - idx[None, :]
    dm = np.where(rel >= 0, np.exp(lg[:, None, None] * np.maximum(rel, 0.0)), 0.0)
    qd = np.broadcast_to(np.exp(lg[:, None] * (idx + 1.0))[:, :, None], (H_R, c, DK_R))
    kd = np.broadcast_to(np.exp(lg[:, None] * (c - 1.0 - idx))[:, :, None], (H_R, c, DK_R))
    cd = np.broadcast_to(np.exp(lg * c)[:, None, None], (H_R, 1, DV_R))
    return tuple(jnp.asarray(a, F32) for a in (dm, qd, kd, cd))


def _lower_bound(lb_ref):
    a = lb_ref[...]
    a0, a1 = a[0:1, :], a[1:2, :]
    m = jnp.maximum(a0, a1)
    e0 = jnp.exp(a0 - m)
    e1 = jnp.exp(a1 - m)
    return e0 / (e0 + e1)


def _forget_and_key(g, lb):
    half_scale = 0.5 * (1.0 - lb)
    th_scaled = half_scale * jnp.tanh(0.5 * g)
    return (lb + half_scale) + th_scaled, half_scale - th_scaled


def _cumsum_rows(tri, x):
    hi = x.astype(BF16)
    rest = x - hi.astype(F32)
    mid = rest.astype(BF16)
    lo = (rest - mid.astype(F32)).astype(BF16)
    y = jnp.dot(tri, jnp.concatenate([hi, mid, lo], axis=1), preferred_element_type=F32)
    n = x.shape[1]
    return y[:, :n] + y[:, n:2 * n] + y[:, 2 * n:]


def _hgrn_pipeline(h_ref, wq_ref, wf_ref, wi_ref, wg_ref, lb_ref, gain_ref, o_ref, s_ref,
                   w_sc, z_sc, q_sc, b_sc, c_sc, *, chunk, sub, group, lag, heads):
    assert sub == 8 and chunk % (2 * sub) == 0
    g_rows = group * chunk
    n_chunks = h_ref.shape[0] // chunk
    assert n_chunks % group == 0 and lag >= 1
    lanes = lambda hh: slice(hh * DK_G, (hh + 1) * DK_G)
    cols = lambda hh, role: lanes(4 * hh + role)
    for hh in range(heads):
        for role, w_ref in enumerate((wq_ref, wf_ref, wi_ref, wg_ref)):
            w_sc[:, cols(hh, role)] = w_ref[:, lanes(hh)].astype(BF16)

    lb = _lower_bound(lb_ref)
    gain = gain_ref[...]
    r64 = lax.broadcasted_iota(jnp.int32, (chunk, chunk), 0)
    c64 = lax.broadcasted_iota(jnp.int32, (chunk, chunk), 1)
    tri = jnp.where(r64 >= c64, 1.0, 0.0).astype(BF16)
    same_sub = (r64 // sub) == (c64 // sub)
    place = [(c64[:sub] % sub == s) & (r64[:sub] >= s) for s in range(sub)]
    zrows = lambda n: jnp.zeros((n, DK_G), F32)

    piece_cols = z_sc.shape[1] // group
    assert piece_cols % DK_G == 0

    def project(c):
        rows = slice(c // group * g_rows, (c // group + 1) * g_rows)
        pcols = slice(c % group * piece_cols, (c % group + 1) * piece_cols)
        z_sc[rows, pcols] = jnp.dot(h_ref[rows, :], w_sc[:, pcols], preferred_element_type=F32)

    def gates(hh, r0):
        rows = slice(r0, r0 + chunk)
        f, k = _forget_and_key(z_sc[rows, cols(hh, 1)], lb[:, lanes(hh)])
        b_sc[rows, lanes(hh)] = _cumsum_rows(tri, jnp.log2(f))
        c_sc[rows, lanes(hh)] = b_sc[rows, lanes(hh)] - jnp.log2(k)
        q_sc[rows, lanes(hh)] = _silu(z_sc[rows, cols(hh, 0)])

    def diagonal(hh, r0):
        blocks = []
        for o in range(r0, r0 + chunk, sub):
            q, b = q_sc[o:o + sub, lanes(hh)], b_sc[o:o + sub, lanes(hh)]
            acc = jnp.zeros((sub, chunk), F32)
            for s in range(sub):
                p = jnp.exp2(b - c_sc[o + s:o + s + 1, lanes(hh)]) * q
                acc = jnp.where(place[s], jnp.sum(p, axis=-1, keepdims=True), acc)
            blocks.append(acc)
        return jnp.where(same_sub, jnp.concatenate(blocks, axis=0), 0.0)

    def below(hh, r0, w):
        qs, ks = [], []
        for mid in range(r0 + w, r0 + chunk, 2 * w):
            ref_b = b_sc[mid - 1:mid, lanes(hh)]
            qs += [zrows(w), q_sc[mid:mid + w, lanes(hh)] * jnp.exp2(b_sc[mid:mid + w, lanes(hh)] - ref_b)]
            ks += [jnp.exp2(ref_b - c_sc[mid - w:mid, lanes(hh)]), zrows(w)]
        scores = _dot_nt(jnp.concatenate(qs, axis=0), jnp.concatenate(ks, axis=0))
        return jnp.where(c64 // w == r64 // w - 1, scores, 0.0)

    def scores(hh, r0):
        a = diagonal(hh, r0)
        w = chunk // 2
        while w >= sub:
            a = a + below(hh, r0, w)
            w //= 2
        return a

    def outputs(hh, r0, a, st):
        rows = slice(r0, r0 + chunk)
        v = z_sc[rows, cols(hh, 2)].astype(BF16)
        b_last = b_sc[rows.stop - 1:rows.stop, lanes(hh)]
        o = _dot(a, v) + _dot_nt(q_sc[rows, lanes(hh)] * jnp.exp2(b_sc[rows, lanes(hh)]), st)
        o_ref[rows, lanes(hh)] = _head_norm_gate(o, gain[:, lanes(hh)], z_sc[rows, cols(hh, 3)])
        return st * jnp.exp2(b_last) + _dot_tn(v, jnp.exp2(b_last - c_sc[rows, lanes(hh)]))

    st = [jnp.zeros((DV_G, DK_G), F32) for _ in range(heads)]
    pending = {}

    def do_scores(hh, c):
        pending[hh, c] = scores(hh, c * chunk)

    def do_outputs(hh, c):
        st[hh] = outputs(hh, c * chunk, pending.pop((hh, c)), st[hh])

    work = []
    for step in range(-group, n_chunks + 2 * lag):
        if step + group < n_chunks:
            work.append(functools.partial(project, step + group))
        for hh in range(heads):
            if 0 <= step < n_chunks:
                work.append(functools.partial(gates, hh, step * chunk))
            if 0 <= step - lag < n_chunks:
                work.append(functools.partial(do_scores, hh, step - lag))
            if 0 <= step - 2 * lag < n_chunks:
                work.append(functools.partial(do_outputs, hh, step - 2 * lag))

    def finish():
        for hh in range(heads):
            s_ref[hh] = st[hh].T

    return work, finish


def _run_pipeline(build, **config):
    def kernel(*refs):
        work, finish = build(*refs, **config)
        for item in work:
            item()
        finish()
    return kernel


N_RET_IN = 11


def _ret_kernel(x_ref, g_ref, *refs, norm_rows, n_cast, **config):
    ret_in, cast_in = refs[:N_RET_IN], refs[N_RET_IN:N_RET_IN + n_cast]
    outs = refs[N_RET_IN + n_cast:]
    (o_ref, s_ref, hn_ref), cast_out, scratch = outs[:3], outs[3:3 + n_cast], outs[3 + n_cast:]

    @pl.when(pl.program_id(1) == 0)
    def _():
        def body(i, carry):
            rows = pl.ds(pl.multiple_of(i * norm_rows, norm_rows), norm_rows)
            hn_ref[rows, :] = _rms(x_ref[rows, :], g_ref[...]).astype(BF16)
            return carry

        lax.fori_loop(0, x_ref.shape[0] // norm_rows, body, 0)

    for src, dst in zip(cast_in, cast_out):
        dst[...] = src[...].astype(BF16)
    work, finish = _ret_pipeline(hn_ref, *ret_in, o_ref, s_ref, *scratch, **config)
    for item in work:
        item()
    finish()


def _retention(x3, g_mix, w_in, cos, sin, gain, casts):
    b, t, d = x3.shape
    n_head = 2 * DK_R + 2 * DV_R
    steps = b * H_R
    w_block = lambda width, lo: pl.BlockSpec((None, d, width), lambda bi, h: (0, 0, lo // width + h))
    head_tab = lambda rows, cols: pl.BlockSpec((None, rows, cols), lambda bi, h: (h, 0, 0))
    cast_in, cast_out, cast_shape = [], [], []
    for a, axis, col_block in casts:
        _, rows, cols = a.shape
        if axis == 1:
            n = rows // steps
            cols = cols if col_block is None else d
            cb = 0 if col_block is None else col_block
            cast_in.append(pl.BlockSpec((None, n, cols), lambda bi, h, cb=cb: (0, bi * H_R + h, cb)))
            cast_out.append(pl.BlockSpec((n, cols), lambda bi, h: (bi * H_R + h, 0)))
        else:
            n = cols // steps
            cast_in.append(pl.BlockSpec((None, rows, n), lambda bi, h: (0, 0, bi * H_R + h)))
            cast_out.append(pl.BlockSpec((rows, n), lambda bi, h: (0, bi * H_R + h)))
        cast_shape.append(jax.ShapeDtypeStruct((rows, cols), BF16))
    return pl.pallas_call(
        functools.partial(_ret_kernel, norm_rows=NORM_ROWS, n_cast=len(casts),
                          chunk=RET_CHUNK, group=RET_GROUP, lag=RET_LAG),
        grid=(b, H_R),
        in_specs=[
            pl.BlockSpec((None, t, d), lambda bi, h: (bi, 0, 0)),
            pl.BlockSpec((1, d), lambda bi, h: (0, 0)),
            w_block(DK_R, 0), w_block(DK_R, QK_R), w_block(DV_R, 2 * QK_R), w_block(DV_R, 2 * QK_R + V_R),
            pl.BlockSpec((t, DK_R), lambda bi, h: (0, 0)),
            pl.BlockSpec((t, DK_R), lambda bi, h: (0, 0)),
            head_tab(RET_CHUNK, RET_CHUNK), head_tab(RET_CHUNK, DK_R), head_tab(RET_CHUNK, DK_R),
            head_tab(1, DV_R),
            pl.BlockSpec((1, DV_R), lambda bi, h: (0, h)),
        ] + cast_in,
        out_specs=[
            pl.BlockSpec((None, t, DV_R), lambda bi, h: (bi, 0, h)),
            pl.BlockSpec((None, None, DK_R, DV_R), lambda bi, h: (bi, h, 0, 0)),
            pl.BlockSpec((None, t, d), lambda bi, h: (bi, 0, 0)),
        ] + cast_out,
        out_shape=[
            jax.ShapeDtypeStruct((b, t, V_R), F32),
            jax.ShapeDtypeStruct((b, H_R, DK_R, DV_R), F32),
            jax.ShapeDtypeStruct((b, t, d), BF16),
        ] + cast_shape,
        scratch_shapes=[pltpu.VMEM((d, n_head), BF16), pltpu.VMEM((t, n_head), F32)],
        compiler_params=_params(("parallel", "arbitrary")),
        name="retention",
    )(x3, g_mix, w_in, w_in, w_in, w_in, cos, sin, *_retention_tables(RET_CHUNK), gain,
      *(a for a, _, _ in casts))


def _hgrn(h3, w_in, hg_lb, gain):
    b, t, d = h3.shape
    heads = HG_HEADS
    wide = heads * DK_G
    first = (2 * QK_R + 2 * V_R) // wide
    w_block = lambda role: pl.BlockSpec((None, d, wide), lambda bi, h: (0, 0, first + role * (H_G // heads) + h))
    return pl.pallas_call(
        _run_pipeline(_hgrn_pipeline, chunk=HG_CHUNK, sub=HG_SUB, group=HG_GROUP, lag=HG_LAG, heads=heads),
        grid=(b, H_G // heads),
        in_specs=[
            pl.BlockSpec((None, t, d), lambda bi, h: (bi, 0, 0)),
            w_block(0), w_block(1), w_block(2), w_block(3),
            pl.BlockSpec((2, wide), lambda bi, h: (0, h)),
            pl.BlockSpec((1, wide), lambda bi, h: (0, h)),
        ],
        out_specs=[
            pl.BlockSpec((None, t, wide), lambda bi, h: (bi, 0, h)),
            pl.BlockSpec((None, heads, DK_G, DV_G), lambda bi, h: (bi, h, 0, 0)),
        ],
        out_shape=[
            jax.ShapeDtypeStruct((b, t, V_G), F32),
            jax.ShapeDtypeStruct((b, H_G, DK_G, DV_G), F32),
        ],
        scratch_shapes=[
            pltpu.VMEM((d, 4 * wide), BF16),
            pltpu.VMEM((t, 4 * wide), F32),
            pltpu.VMEM((t, wide), F32),
            pltpu.VMEM((t, wide), F32),
            pltpu.VMEM((t, wide), F32),
        ],
        compiler_params=_params(("parallel", "arbitrary")),
        name="hgrn",
    )(h3, w_in, w_in, w_in, w_in, hg_lb, gain)


def _step_kernel(z_ref, cos_ref, sin_ref, lb_ref, rgain_ref, ggain_ref, sr0_ref, sg0_ref,
                 or_ref, og_ref, sr_ref, sg_ref, *, bb):
    first = lambda a: jnp.where(lax.broadcasted_iota(jnp.int32, a.shape, 0) == 0, a, 0.0)
    cos = cos_ref[...]
    sin = sin_ref[...]
    lb_all = _lower_bound(lb_ref)
    row8 = lax.broadcasted_iota(jnp.int32, (8, DK_G), 0)
    ones3 = jnp.where(row8 < 3, 1.0, 0.0)
    g0 = 2 * QK_R + 2 * V_R
    base = (pl.program_id(0) % (8 // bb)) * bb
    work = []
    for i in range(bb):
        def z_row(lo, n, i=i):
            mine = lax.broadcasted_iota(jnp.int32, (8, n), 0) == base + i
            return jnp.sum(jnp.where(mine, z_ref[:, lo:lo + n], 0.0), axis=0, keepdims=True)

        z8 = lambda lo, n, z_row=z_row: jnp.broadcast_to(z_row(lo, n), (8, n))
        for h in range(H_R):
            q = _rotate(z8(h * DK_R, DK_R), cos, sin)
            k = _rotate(z8(QK_R + h * DK_R, DK_R), cos, sin) * (DK_R ** -0.5)
            v = z8(2 * QK_R + h * DV_R, DV_R)
            lanes = slice(h * DV_R, (h + 1) * DV_R)
            gate = z_row(2 * QK_R + V_R + h * DV_R, DV_R)
            work.append((sr0_ref, sr_ref, or_ref, rgain_ref, i, h, lanes, q, gate,
                         1.0 - 2.0 ** (-5.0 - h), _dot_tn(first(k), v)))
        for h in range(H_G):
            lanes = slice(h * DK_G, (h + 1) * DK_G)
            q = _silu(z8(g0 + h * DK_G, DK_G))
            f, k = _forget_and_key(z8(g0 + F_G + h * DK_G, DK_G), lb_all[:, lanes])
            v = z8(g0 + 2 * F_G + h * DV_G, DV_G)
            gate = z_row(g0 + 2 * F_G + V_G + h * DV_G, DV_G)
            f_hi = f.astype(BF16).astype(F32)
            f_mid = (f - f_hi).astype(BF16).astype(F32)
            f_lo = f - f_hi - f_mid
            lhs = jnp.where(row8 == 0, f_hi, jnp.where(row8 == 1, f_mid, jnp.where(row8 == 2, f_lo,
                            jnp.where(row8 == 3, k, 0.0))))
            rhs = jnp.concatenate([ones3, jnp.where(row8 == 3, v, 0.0)], axis=1)
            both = _dot_tn(lhs, rhs)
            work.append((sg0_ref, sg_ref, og_ref, ggain_ref, i, h, lanes, q, gate,
                         both[:, :DV_G], both[:, DV_G:]))
    outs = []
    for s0_ref, s_ref, _, _, i, h, _, q, _, decay, kv in work:
        s_new = decay * s0_ref[i, h] + kv
        s_ref[i, h] = s_new
        outs.append(_dot(q, s_new)[0:1, :])
    for (_, _, o_ref, gain_ref, i, _, lanes, _, gate, _, _), o in zip(work, outs):
        o_ref[i, :, lanes] = _head_norm_gate(o, gain_ref[:, lanes], gate)


def _step_specs(z, cos, sin, hg_lb, rgain, ggain, sr0, sg0, bb):
    n, d_in = z.shape
    assert 8 % bb == 0
    row = lambda w: pl.BlockSpec((bb, 1, w), lambda i: (i, 0, 0))
    const = lambda a: pl.BlockSpec(a.shape, lambda i: (0,) * a.ndim)
    sr_spec = pl.BlockSpec((bb, H_R, DK_R, DV_R), lambda i: (i, 0, 0, 0))
    sg_spec = pl.BlockSpec((bb, H_G, DK_G, DV_G), lambda i: (i, 0, 0, 0))
    operands = (z, cos, sin, hg_lb, rgain, ggain, sr0, sg0)
    in_specs = [pl.BlockSpec((8, d_in), lambda i: (i * bb // 8, 0)),
                const(cos), const(sin), const(hg_lb), const(rgain), const(ggain), sr_spec, sg_spec]
    out_specs = [row(V_R), row(V_G), sr_spec, sg_spec]
    out_shape = [
        jax.ShapeDtypeStruct((n, 1, V_R), F32),
        jax.ShapeDtypeStruct((n, 1, V_G), F32),
        jax.ShapeDtypeStruct(sr0.shape, F32),
        jax.ShapeDtypeStruct(sg0.shape, F32),
    ]
    return operands, in_specs, out_specs, out_shape


N_POST_IN = 17
N_STEP_IN = 8


def _post_kernel(*refs, bb):
    if bb:
        _post_body(*refs[:N_POST_IN], refs[N_POST_IN + N_STEP_IN])
        _step_kernel(*refs[N_POST_IN:N_POST_IN + N_STEP_IN], *refs[N_POST_IN + N_STEP_IN + 1:], bb=bb)
    else:
        _post_body(*refs)


def _post_body(x_ref, h_ref, gr_ref, gg_ref, p_ref, war_ref, wag_ref, wur_ref, wug_ref, wo_ref,
               nm_ref, w1_ref, w2_ref, np_ref, wg_ref, wp_ref, nf_ref, y_ref):
    h = h_ref[...]
    u_r = jnp.dot(gr_ref[...].astype(BF16), wur_ref[...], preferred_element_type=F32)
    u_g = jnp.dot(gg_ref[...].astype(BF16), wug_ref[...], preferred_element_type=F32)
    m = (_sigmoid(jnp.dot(h, war_ref[...], preferred_element_type=F32)) * u_r
         + _sigmoid(jnp.dot(h, wag_ref[...], preferred_element_type=F32)) * u_g)
    r = x_ref[...] + jnp.dot(m.astype(BF16), wo_ref[...], preferred_element_type=F32)
    hm = _rms(r, nm_ref[...]).astype(BF16)
    a = jnp.maximum(jnp.dot(hm, w1_ref[...], preferred_element_type=F32), 0.0)
    r = r + jnp.dot((a * a).astype(BF16), w2_ref[...], preferred_element_type=F32)
    hp = _rms(r, np_ref[...]).astype(BF16)
    gate = _sigmoid(jnp.dot(hp, wg_ref[...], preferred_element_type=F32))
    r = r + gate * jnp.dot(p_ref[...].astype(BF16), wp_ref[...], preferred_element_type=F32)
    y_ref[...] = _rms(r, nf_ref[...])


def _post(x, h, g_r, g_g, p, war, wag, wur, wug, wo, nm, w1, w2, npl, wg, wp, nf, tm, step=None):
    n, d = x.shape
    grid = n // tm
    rows = lambda w: pl.BlockSpec((tm, w), lambda i: (i, 0))
    const = lambda a: pl.BlockSpec(a.shape, lambda i: (0, 0), pipeline_mode=pl.Buffered(1))
    operands = (x, h, g_r, g_g, p, war, wag, wur, wug, wo, nm, w1, w2, npl, wg, wp, nf)
    in_specs = [rows(d), rows(d), rows(d), rows(d), rows(p.shape[1]),
                const(war), const(wag), const(wur), const(wug), const(wo), const(nm), const(w1), const(w2),
                const(npl), const(wg), const(wp), const(nf)]
    assert len(operands) == N_POST_IN
    out_specs = [rows(d)]
    out_shape = [jax.ShapeDtypeStruct((n, d), F32)]
    bb = 0
    if step is not None:
        bb = step[0].shape[0] // grid
        assert bb * grid == step[0].shape[0] and len(step) == N_STEP_IN
        s_operands, s_in, s_out, s_shape = _step_specs(*step, bb=bb)
        operands += s_operands
        in_specs += s_in
        out_specs += s_out
        out_shape += s_shape
    out = pl.pallas_call(
        functools.partial(_post_kernel, bb=bb),
        grid=(grid,),
        in_specs=in_specs,
        out_specs=out_specs,
        out_shape=out_shape,
        compiler_params=_params(("parallel",)),
        name="post",
    )(*operands)
    return out if bb else out[0]


def _rope_tables(pos):
    half = DK_R // 2
    inv = ROPE_BASE ** (-jnp.arange(half, dtype=F32) / half)
    ang = pos[:, None] * inv[None, :]
    cos, sin = jnp.cos(ang), jnp.sin(ang)
    return jnp.concatenate([cos, cos], axis=-1), jnp.concatenate([-sin, sin], axis=-1)


def kernel(x_prompt, x_sample, state_ret, state_hgrn, p_prompt, p_sample, norm_mix_g, w_in,
           ret_norm_g, hg_norm_g, hg_lb, w_up_ret, w_up_hg, w_o, norm_mlp_g, w_ff1, w_ff2,
           norm_ple_g, w_ple_gate, w_ple_proj, norm_final_g):
    b, t, d = x_prompt.shape
    nb, td, _ = x_sample.shape
    assert w_in.shape[0] == 1 and td == 1 and hg_lb.shape[0] == 2
    row = lambda g: g.reshape(1, -1)
    merge_block = (2 * QK_R + 2 * V_R + 2 * F_G + 2 * V_G) // d
    g_mix = row(norm_mix_g[0])
    rgain = row(ret_norm_g[0])
    ggain = row(hg_norm_g[0])

    xs = x_sample.reshape(nb, d)
    hs = _norm(xs, g_mix, tm=nb)
    zs = _proj(hs, w_in, tm=nb, tn=1024)
    cos_s, sin_s = _rope_tables(PAST_LEN + jnp.arange(td, dtype=F32))
    step = (zs, cos_s, sin_s, hg_lb, rgain, ggain, state_ret[0], state_hgrn[0])

    xp = x_prompt.reshape(b * t, d)
    cos_p, sin_p = _rope_tables(jnp.arange(t, dtype=F32))
    casts = [(w_in, 1, merge_block), (w_in, 1, merge_block + 1), (w_up_ret, 1, None), (w_up_hg, 1, None),
             (w_o, 1, None), (w_ff1, 2, None), (w_ff2, 1, None), (w_ple_gate, 1, None)]
    gr_p, ret_p, hp3, war, wag, wur, wug, wo, w1, w2, wg = _retention(
        x_prompt, g_mix, w_in, cos_p, sin_p, rgain, casts)
    post_w = (war, wag, wur, wug, wo, row(norm_mlp_g[0]), w1, w2, row(norm_ple_g[0]), wg,
              w_ple_proj[0].astype(BF16), row(norm_final_g))
    gg_p, hg_p = _hgrn(hp3, w_in, hg_lb, ggain)
    y_p, gr_s, gg_s, ret_s, hg_s = _post(
        xp, hp3.reshape(b * t, d), gr_p.reshape(b * t, -1), gg_p.reshape(b * t, -1),
        p_prompt[0].reshape(b * t, -1), *post_w, tm=256, step=step)

    y_s = _post(xs, hs, gr_s.reshape(nb, -1), gg_s.reshape(nb, -1),
                p_sample[0].reshape(nb, -1), *post_w, tm=nb)

    return (y_p.reshape(b, t, d), y_s.reshape(nb, td, d), ret_p[None], hg_p[None],
            ret_s[None], hg_s[None])
```

```python
import functools

import jax
import jax.numpy as jnp
import numpy as np
from jax import lax
from jax.experimental import pallas as pl
from jax.experimental.pallas import tpu as pltpu

F32 = jnp.float32
BF16 = jnp.bfloat16

PAST_LEN = 16384
ROPE_BASE = 10000.0
EPS = 1e-6

H_R, DK_R, DV_R = 4, 128, 256
H_G, DK_G, DV_G = 8, 128, 128
QK_R = H_R * DK_R
V_R = H_R * DV_R
F_G = H_G * DK_G
V_G = H_G * DV_G

V7X_VMEM_BYTES = 64 * 1024 * 1024
VMEM_LIMIT = V7X_VMEM_BYTES - 8 * 1024 * 1024

NORM_ROWS = 256
RET_CHUNK = 128
RET_GROUP = 2
RET_LAG = 1
HG_CHUNK = 64
HG_SUB = 8
HG_GROUP = 4
HG_LAG = 2
HG_HEADS = 2

NT_DIMS = (((1,), (1,)), ((), ()))
TN_DIMS = (((0,), (0,)), ((), ()))


def _dot(a, b):
    return jnp.dot(a.astype(BF16), b.astype(BF16), preferred_element_type=F32)


def _dot_nt(a, b):
    return lax.dot_general(a.astype(BF16), b.astype(BF16), NT_DIMS, preferred_element_type=F32)


def _dot_tn(a, b):
    return lax.dot_general(a.astype(BF16), b.astype(BF16), TN_DIMS, preferred_element_type=F32)


def _rms(x, g):
    return x * lax.rsqrt(jnp.mean(x * x, axis=-1, keepdims=True) + EPS) * g


def _sigmoid(x):
    return 0.5 * jnp.tanh(0.5 * x) + 0.5


def _silu(x):
    half = 0.5 * x
    return half + half * jnp.tanh(half)


def _params(sem):
    return pltpu.CompilerParams(dimension_semantics=sem, vmem_limit_bytes=VMEM_LIMIT)


def _norm_kernel(x_ref, g_ref, h_ref):
    h_ref[...] = _rms(x_ref[...], g_ref[...]).astype(BF16)


def _norm(x, g, tm):
    n, d = x.shape
    return pl.pallas_call(
        _norm_kernel,
        grid=(n // tm,),
        in_specs=[pl.BlockSpec((tm, d), lambda i: (i, 0)), pl.BlockSpec((1, d), lambda i: (0, 0))],
        out_specs=pl.BlockSpec((tm, d), lambda i: (i, 0)),
        out_shape=jax.ShapeDtypeStruct((n, d), BF16),
        compiler_params=_params(("parallel",)),
        name="norm",
    )(x, g)


def _rotate(x, cos, sin_signed):
    return x * cos + pltpu.roll(x, DK_R // 2, 1) * sin_signed


def _head_norm_gate(o, gain, gate):
    return o * lax.rsqrt(jnp.mean(o * o, axis=-1, keepdims=True) + EPS) * gain * _silu(gate)


def _ret_pipeline(h_ref, wq_ref, wk_ref, wv_ref, wg_ref, cos_ref, sin_ref, dm_ref, qd_ref, kd_ref,
                  cd_ref, gain_ref, o_ref, s_ref, w_sc, z_sc, *, chunk, group, lag):
    n_chunks = h_ref.shape[0] // chunk
    assert n_chunks % group == 0 and lag >= 1
    g_rows = group * chunk
    q_cols = slice(0, DK_R)
    k_cols = slice(DK_R, 2 * DK_R)
    v_cols = slice(2 * DK_R, 2 * DK_R + DV_R)
    g_cols = slice(2 * DK_R + DV_R, 2 * DK_R + 2 * DV_R)
    for cols, w_ref in ((q_cols, wq_ref), (k_cols, wk_ref), (v_cols, wv_ref), (g_cols, wg_ref)):
        w_sc[:, cols] = w_ref[...].astype(BF16)
    dm = dm_ref[...]
    qd = qd_ref[...]
    kd = kd_ref[...]
    cd = cd_ref[...]
    gain = gain_ref[...]

    assert group == 2
    pieces = (slice(0, g_cols.start), g_cols)

    def project(c):
        rows = slice(c // group * g_rows, (c // group + 1) * g_rows)
        pcols = pieces[c % group]
        z_sc[rows, pcols] = jnp.dot(h_ref[rows, :], w_sc[:, pcols], preferred_element_type=F32)

    def scores(r0):
        rows = slice(r0, r0 + chunk)
        cos = cos_ref[rows, :]
        sin = sin_ref[rows, :]
        q = _rotate(z_sc[rows, q_cols], cos, sin)
        k = _rotate(z_sc[rows, k_cols], cos, sin) * (DK_R ** -0.5)
        a = (_dot_nt(q, k) * dm).astype(BF16)
        return a, (q * qd).astype(BF16), (k * kd).astype(BF16)

    def outputs(r0, parts, s):
        a, q_in, k_st = parts
        rows = slice(r0, r0 + chunk)
        v = z_sc[rows, v_cols].astype(BF16)
        o = jnp.dot(jnp.concatenate([a, q_in], axis=1), jnp.concatenate([v, s.astype(BF16)], axis=0),
                    preferred_element_type=F32)
        o_ref[rows, :] = _head_norm_gate(o, gain, z_sc[rows, g_cols])
        return cd * s + lax.dot_general(k_st, v, TN_DIMS, preferred_element_type=F32)

    carry = {"s": jnp.zeros((DK_R, DV_R), F32)}
    pending = {}

    def do_scores(c):
        pending[c] = scores(c * chunk)

    def do_outputs(c):
        carry["s"] = outputs(c * chunk, pending.pop(c), carry["s"])

    work = []
    for step in range(-group, n_chunks + lag):
        if step + group < n_chunks:
            work.append(functools.partial(project, step + group))
        if 0 <= step < n_chunks:
            work.append(functools.partial(do_scores, step))
        if 0 <= step - lag < n_chunks:
            work.append(functools.partial(do_outputs, step - lag))

    def finish():
        s_ref[...] = carry["s"]

    return work, finish


def _retention_tables(c):
    lg = np.log(1.0 - 2.0 ** (-5.0 - np.arange(H_R, dtype=np.float64)))
    idx = np.arange(c, dtype=np.float64)
    rel = idx[:, None] - idx[None, :]
    dm = np.where(rel >= 0, np.exp(lg[:, None, None] * np.maximum(rel, 0.0)), 0.0)
    qd = np.broadcast_to(np.exp(lg[:, None] * (idx + 1.0))[:, :, None], (H_R, c, DK_R))
    kd = np.broadcast_to(np.exp(lg[:, None] * (c - 1.0 - idx))[:, :, None], (H_R, c, DK_R))
    cd = np.broadcast_to(np.exp(lg * c)[:, None, None], (H_R, 1, DV_R))
    return tuple(jnp.asarray(a, F32) for a in (dm, qd, kd, cd))


def _lower_bound(lb_ref):
    a = lb_ref[...]
    a0, a1 = a[0:1, :], a[1:2, :]
    m = jnp.maximum(a0, a1)
    e0 = jnp.exp(a0 - m)
    e1 = jnp.exp(a1 - m)
    return e0 / (e0 + e1)


def _forget_and_key(g, lb):
    half_scale = 0.5 * (1.0 - lb)
    th_scaled = half_scale * jnp.tanh(0.5 * g)
    return (lb + half_scale) + th_scaled, half_scale - th_scaled


def _cumsum_rows(tri, x):
    hi = x.astype(BF16)
    rest = x - hi.astype(F32)
    mid = rest.astype(BF16)
    lo = (rest - mid.astype(F32)).astype(BF16)
    y = jnp.dot(tri, jnp.concatenate([hi, mid, lo], axis=1), preferred_element_type=F32)
    n = x.shape[1]
    return y[:, :n] + y[:, n:2 * n] + y[:, 2 * n:]


def _hgrn_pipeline(h_ref, wq_ref, wf_ref, wi_ref, wg_ref, lb_ref, gain_ref, o_ref, s_ref,
                   w_sc, z_sc, q_sc, b_sc, c_sc, *, chunk, sub, group, lag, heads):
    assert sub == 8 and chunk % (2 * sub) == 0
    g_rows = group * chunk
    n_chunks = h_ref.shape[0] // chunk
    assert n_chunks % group == 0 and lag >= 1
    lanes = lambda hh: slice(hh * DK_G, (hh + 1) * DK_G)
    cols = lambda hh, role: lanes(4 * hh + role)
    for hh in range(heads):
        for role, w_ref in enumerate((wq_ref, wf_ref, wi_ref, wg_ref)):
            w_sc[:, cols(hh, role)] = w_ref[:, lanes(hh)].astype(BF16)

    lb = _lower_bound(lb_ref)
    gain = gain_ref[...]
    r64 = lax.broadcasted_iota(jnp.int32, (chunk, chunk), 0)
    c64 = lax.broadcasted_iota(jnp.int32, (chunk, chunk), 1)
    tri = jnp.where(r64 >= c64, 1.0, 0.0).astype(BF16)
    same_sub = (r64 // sub) == (c64 // sub)
    place = [(c64[:sub] % sub == s) & (r64[:sub] >= s) for s in range(sub)]
    zrows = lambda n: jnp.zeros((n, DK_G), F32)

    piece_cols = z_sc.shape[1] // group
    assert piece_cols % DK_G == 0

    def project(c):
        rows = slice(c // group * g_rows, (c // group + 1) * g_rows)
        pcols = slice(c % group * piece_cols, (c % group + 1) * piece_cols)
        z_sc[rows, pcols] = jnp.dot(h_ref[rows, :], w_sc[:, pcols], preferred_element_type=F32)

    def gates(hh, r0):
        rows = slice(r0, r0 + chunk)
        f, k = _forget_and_key(z_sc[rows, cols(hh, 1)], lb[:, lanes(hh)])
        b_sc[rows, lanes(hh)] = _cumsum_rows(tri, jnp.log2(f))
        c_sc[rows, lanes(hh)] = b_sc[rows, lanes(hh)] - jnp.log2(k)
        q_sc[rows, lanes(hh)] = _silu(z_sc[rows, cols(hh, 0)])

    def diagonal(hh, r0):
        blocks = []
        for o in range(r0, r0 + chunk, sub):
            q, b = q_sc[o:o + sub, lanes(hh)], b_sc[o:o + sub, lanes(hh)]
            acc = jnp.zeros((sub, chunk), F32)
            for s in range(sub):
                p = jnp.exp2(b - c_sc[o + s:o + s + 1, lanes(hh)]) * q
                acc = jnp.where(place[s], jnp.sum(p, axis=-1, keepdims=True), acc)
            blocks.append(acc)
        return jnp.where(same_sub, jnp.concatenate(blocks, axis=0), 0.0)

    def below(hh, r0, w):
        qs, ks = [], []
        for mid in range(r0 + w, r0 + chunk, 2 * w):
            ref_b = b_sc[mid - 1:mid, lanes(hh)]
            qs += [zrows(w), q_sc[mid:mid + w, lanes(hh)] * jnp.exp2(b_sc[mid:mid + w, lanes(hh)] - ref_b)]
            ks += [jnp.exp2(ref_b - c_sc[mid - w:mid, lanes(hh)]), zrows(w)]
        scores = _dot_nt(jnp.concatenate(qs, axis=0), jnp.concatenate(ks, axis=0))
        return jnp.where(c64 // w == r64 // w - 1, scores, 0.0)

    def scores(hh, r0):
        a = diagonal(hh, r0)
        w = chunk // 2
        while w >= sub:
            a = a + below(hh, r0, w)
            w //= 2
        return a

    def outputs(hh, r0, a, st):
        rows = slice(r0, r0 + chunk)
        v = z_sc[rows, cols(hh, 2)].astype(BF16)
        b_last = b_sc[rows.stop - 1:rows.stop, lanes(hh)]
        o = _dot(a, v) + _dot_nt(q_sc[rows, lanes(hh)] * jnp.exp2(b_sc[rows, lanes(hh)]), st)
        o_ref[rows, lanes(hh)] = _head_norm_gate(o, gain[:, lanes(hh)], z_sc[rows, cols(hh, 3)])
        return st * jnp.exp2(b_last) + _dot_tn(v, jnp.exp2(b_last - c_sc[rows, lanes(hh)]))

    st = [jnp.zeros((DV_G, DK_G), F32) for _ in range(heads)]
    pending = {}

    def do_scores(hh, c):
        pending[hh, c] = scores(hh, c * chunk)

    def do_outputs(hh, c):
        st[hh] = outputs(hh, c * chunk, pending.pop((hh, c)), st[hh])

    work = []
    for step in range(-group, n_chunks + 2 * lag):
        if step + group < n_chunks:
            work.append(functools.partial(project, step + group))
        for hh in range(heads):
            if 0 <= step < n_chunks:
                work.append(functools.partial(gates, hh, step * chunk))
            if 0 <= step - lag < n_chunks:
                work.append(functools.partial(do_scores, hh, step - lag))
            if 0 <= step - 2 * lag < n_chunks:
                work.append(functools.partial(do_outputs, hh, step - 2 * lag))

    def finish():
        for hh in range(heads):
            s_ref[hh] = st[hh].T

    return work, finish


def _run_pipeline(build, **config):
    def kernel(*refs):
        work, finish = build(*refs, **config)
        for item in work:
            item()
        finish()
    return kernel


N_RET_IN = 11


def _ret_kernel(x_ref, g_ref, *refs, norm_rows, n_cast, **config):
    ret_in, cast_in = refs[:N_RET_IN], refs[N_RET_IN:N_RET_IN + n_cast]
    hs_ref, wz_ref = refs[N_RET_IN + n_cast:N_RET_IN + n_cast + 2]
    outs = refs[N_RET_IN + n_cast + 2:]
    (o_ref, s_ref, hn_ref), cast_out, zs_ref, scratch = outs[:3], outs[3:3 + n_cast], outs[3 + n_cast], outs[4 + n_cast:]

    @pl.when(pl.program_id(1) == 0)
    def _():
        def body(i, carry):
            rows = pl.ds(pl.multiple_of(i * norm_rows, norm_rows), norm_rows)
            hn_ref[rows, :] = _rms(x_ref[rows, :], g_ref[...]).astype(BF16)
            return carry

        lax.fori_loop(0, x_ref.shape[0] // norm_rows, body, 0)

    for src, dst in zip(cast_in, cast_out):
        dst[...] = src[...].astype(BF16)
    zs_ref[...] = jnp.dot(hs_ref[...], wz_ref[...].astype(BF16), preferred_element_type=F32)
    work, finish = _ret_pipeline(hn_ref, *ret_in, o_ref, s_ref, *scratch, **config)
    for item in work:
        item()
    finish()


def _retention(x3, g_mix, w_in, cos, sin, gain, casts, hs):
    b, t, d = x3.shape
    n_head = 2 * DK_R + 2 * DV_R
    steps = b * H_R
    d_in = w_in.shape[2]
    z_cols = DK_R * -(-d_in // (DK_R * steps))
    assert d_in % z_cols == 0
    z_block = lambda bi, h: jnp.minimum(bi * H_R + h, d_in // z_cols - 1)
    w_block = lambda width, lo: pl.BlockSpec((None, d, width), lambda bi, h: (0, 0, lo // width + h))
    head_tab = lambda rows, cols: pl.BlockSpec((None, rows, cols), lambda bi, h: (h, 0, 0))
    cast_in, cast_out, cast_shape = [], [], []
    for a, axis, col_block in casts:
        _, rows, cols = a.shape
        if axis == 1:
            n = rows // steps
            cols = cols if col_block is None else d
            cb = 0 if col_block is None else col_block
            cast_in.append(pl.BlockSpec((None, n, cols), lambda bi, h, cb=cb: (0, bi * H_R + h, cb)))
            cast_out.append(pl.BlockSpec((n, cols), lambda bi, h: (bi * H_R + h, 0)))
        else:
            n = cols // steps
            cast_in.append(pl.BlockSpec((None, rows, n), lambda bi, h: (0, 0, bi * H_R + h)))
            cast_out.append(pl.BlockSpec((rows, n), lambda bi, h: (0, bi * H_R + h)))
        cast_shape.append(jax.ShapeDtypeStruct((rows, cols), BF16))
    return pl.pallas_call(
        functools.partial(_ret_kernel, norm_rows=NORM_ROWS, n_cast=len(casts),
                          chunk=RET_CHUNK, group=RET_GROUP, lag=RET_LAG),
        grid=(b, H_R),
        in_specs=[
            pl.BlockSpec((None, t, d), lambda bi, h: (bi, 0, 0)),
            pl.BlockSpec((1, d), lambda bi, h: (0, 0)),
            w_block(DK_R, 0), w_block(DK_R, QK_R), w_block(DV_R, 2 * QK_R), w_block(DV_R, 2 * QK_R + V_R),
            pl.BlockSpec((t, DK_R), lambda bi, h: (0, 0)),
            pl.BlockSpec((t, DK_R), lambda bi, h: (0, 0)),
            head_tab(RET_CHUNK, RET_CHUNK), head_tab(RET_CHUNK, DK_R), head_tab(RET_CHUNK, DK_R),
            head_tab(1, DV_R),
            pl.BlockSpec((1, DV_R), lambda bi, h: (0, h)),
        ] + cast_in + [
            pl.BlockSpec(hs.shape, lambda bi, h: (0, 0)),
            pl.BlockSpec((None, d, z_cols), lambda bi, h: (0, 0, z_block(bi, h))),
        ],
        out_specs=[
            pl.BlockSpec((None, t, DV_R), lambda bi, h: (bi, 0, h)),
            pl.BlockSpec((None, None, DK_R, DV_R), lambda bi, h: (bi, h, 0, 0)),
            pl.BlockSpec((None, t, d), lambda bi, h: (bi, 0, 0)),
        ] + cast_out + [pl.BlockSpec((hs.shape[0], z_cols), lambda bi, h: (0, z_block(bi, h)))],
        out_shape=[
            jax.ShapeDtypeStruct((b, t, V_R), F32),
            jax.ShapeDtypeStruct((b, H_R, DK_R, DV_R), F32),
            jax.ShapeDtypeStruct((b, t, d), BF16),
        ] + cast_shape + [jax.ShapeDtypeStruct((hs.shape[0], d_in), F32)],
        scratch_shapes=[pltpu.VMEM((d, n_head), BF16), pltpu.VMEM((t, n_head), F32)],
        compiler_params=_params(("arbitrary", "arbitrary")),
        name="retention",
    )(x3, g_mix, w_in, w_in, w_in, w_in, cos, sin, *_retention_tables(RET_CHUNK), gain,
      *(a for a, _, _ in casts), hs, w_in)


def _hgrn(h3, w_in, hg_lb, gain):
    b, t, d = h3.shape
    heads = HG_HEADS
    wide = heads * DK_G
    first = (2 * QK_R + 2 * V_R) // wide
    w_block = lambda role: pl.BlockSpec((None, d, wide), lambda bi, h: (0, 0, first + role * (H_G // heads) + h))
    return pl.pallas_call(
        _run_pipeline(_hgrn_pipeline, chunk=HG_CHUNK, sub=HG_SUB, group=HG_GROUP, lag=HG_LAG, heads=heads),
        grid=(b, H_G // heads),
        in_specs=[
            pl.BlockSpec((None, t, d), lambda bi, h: (bi, 0, 0)),
            w_block(0), w_block(1), w_block(2), w_block(3),
            pl.BlockSpec((2, wide), lambda bi, h: (0, h)),
            pl.BlockSpec((1, wide), lambda bi, h: (0, h)),
        ],
        out_specs=[
            pl.BlockSpec((None, t, wide), lambda bi, h: (bi, 0, h)),
            pl.BlockSpec((None, heads, DK_G, DV_G), lambda bi, h: (bi, h, 0, 0)),
        ],
        out_shape=[
            jax.ShapeDtypeStruct((b, t, V_G), F32),
            jax.ShapeDtypeStruct((b, H_G, DK_G, DV_G), F32),
        ],
        scratch_shapes=[
            pltpu.VMEM((d, 4 * wide), BF16),
            pltpu.VMEM((t, 4 * wide), F32),
            pltpu.VMEM((t, wide), F32),
            pltpu.VMEM((t, wide), F32),
            pltpu.VMEM((t, wide), F32),
        ],
        compiler_params=_params(("parallel", "arbitrary")),
        name="hgrn",
    )(h3, w_in, w_in, w_in, w_in, hg_lb, gain)


def _step_kernel(z_ref, cos_ref, sin_ref, lb_ref, rgain_ref, ggain_ref, sr0_ref, sg0_ref,
                 or_ref, og_ref, sr_ref, sg_ref, *, bb):
    first = lambda a: jnp.where(lax.broadcasted_iota(jnp.int32, a.shape, 0) == 0, a, 0.0)
    cos = cos_ref[...]
    sin = sin_ref[...]
    lb_all = _lower_bound(lb_ref)
    row8 = lax.broadcasted_iota(jnp.int32, (8, DK_G), 0)
    ones3 = jnp.where(row8 < 3, 1.0, 0.0)
    g0 = 2 * QK_R + 2 * V_R
    base = (pl.program_id(0) % (8 // bb)) * bb
    work = []
    for i in range(bb):
        def z_row(lo, n, i=i):
            mine = lax.broadcasted_iota(jnp.int32, (8, n), 0) == base + i
            return jnp.sum(jnp.where(mine, z_ref[:, lo:lo + n], 0.0), axis=0, keepdims=True)

        z8 = lambda lo, n, z_row=z_row: jnp.broadcast_to(z_row(lo, n), (8, n))
        for h in range(H_R):
            q = _rotate(z8(h * DK_R, DK_R), cos, sin)
            k = _rotate(z8(QK_R + h * DK_R, DK_R), cos, sin) * (DK_R ** -0.5)
            v = z8(2 * QK_R + h * DV_R, DV_R)
            lanes = slice(h * DV_R, (h + 1) * DV_R)
            gate = z_row(2 * QK_R + V_R + h * DV_R, DV_R)
            work.append((sr0_ref, sr_ref, or_ref, rgain_ref, i, h, lanes, q, gate,
                         1.0 - 2.0 ** (-5.0 - h), _dot_tn(first(k), v)))
        for h in range(H_G):
            lanes = slice(h * DK_G, (h + 1) * DK_G)
            q = _silu(z8(g0 + h * DK_G, DK_G))
            f, k = _forget_and_key(z8(g0 + F_G + h * DK_G, DK_G), lb_all[:, lanes])
            v = z8(g0 + 2 * F_G + h * DV_G, DV_G)
            gate = z_row(g0 + 2 * F_G + V_G + h * DV_G, DV_G)
            f_hi = f.astype(BF16).astype(F32)
            f_mid = (f - f_hi).astype(BF16).astype(F32)
            f_lo = f - f_hi - f_mid
            lhs = jnp.where(row8 == 0, f_hi, jnp.where(row8 == 1, f_mid, jnp.where(row8 == 2, f_lo,
                            jnp.where(row8 == 3, k, 0.0))))
            rhs = jnp.concatenate([ones3, jnp.where(row8 == 3, v, 0.0)], axis=1)
            both = _dot_tn(lhs, rhs)
            work.append((sg0_ref, sg_ref, og_ref, ggain_ref, i, h, lanes, q, gate,
                         both[:, :DV_G], both[:, DV_G:]))
    outs = []
    for s0_ref, s_ref, _, _, i, h, _, q, _, decay, kv in work:
        s_new = decay * s0_ref[i, h] + kv
        s_ref[i, h] = s_new
        outs.append(_dot(q, s_new)[0:1, :])
    for (_, _, o_ref, gain_ref, i, _, lanes, _, gate, _, _), o in zip(work, outs):
        o_ref[i, :, lanes] = _head_norm_gate(o, gain_ref[:, lanes], gate)


def _step_specs(z, cos, sin, hg_lb, rgain, ggain, sr0, sg0, bb):
    n, d_in = z.shape
    assert 8 % bb == 0
    row = lambda w: pl.BlockSpec((bb, 1, w), lambda i: (i, 0, 0))
    const = lambda a: pl.BlockSpec(a.shape, lambda i: (0,) * a.ndim)
    sr_spec = pl.BlockSpec((bb, H_R, DK_R, DV_R), lambda i: (i, 0, 0, 0))
    sg_spec = pl.BlockSpec((bb, H_G, DK_G, DV_G), lambda i: (i, 0, 0, 0))
    operands = (z, cos, sin, hg_lb, rgain, ggain, sr0, sg0)
    in_specs = [pl.BlockSpec((8, d_in), lambda i: (i * bb // 8, 0)),
                const(cos), const(sin), const(hg_lb), const(rgain), const(ggain), sr_spec, sg_spec]
    out_specs = [row(V_R), row(V_G), sr_spec, sg_spec]
    out_shape = [
        jax.ShapeDtypeStruct((n, 1, V_R), F32),
        jax.ShapeDtypeStruct((n, 1, V_G), F32),
        jax.ShapeDtypeStruct(sr0.shape, F32),
        jax.ShapeDtypeStruct(sg0.shape, F32),
    ]
    return operands, in_specs, out_specs, out_shape


N_POST_IN = 17
N_STEP_IN = 8


def _post_kernel(*refs, bb):
    if bb:
        _post_body(*refs[:N_POST_IN], refs[N_POST_IN + N_STEP_IN])
        _step_kernel(*refs[N_POST_IN:N_POST_IN + N_STEP_IN], *refs[N_POST_IN + N_STEP_IN + 1:], bb=bb)
    else:
        _post_body(*refs)


def _post_body(x_ref, h_ref, gr_ref, gg_ref, p_ref, war_ref, wag_ref, wur_ref, wug_ref, wo_ref,
               nm_ref, w1_ref, w2_ref, np_ref, wg_ref, wp_ref, nf_ref, y_ref):
    h = h_ref[...]
    u_r = jnp.dot(gr_ref[...].astype(BF16), wur_ref[...], preferred_element_type=F32)
    u_g = jnp.dot(gg_ref[...].astype(BF16), wug_ref[...], preferred_element_type=F32)
    m = (_sigmoid(jnp.dot(h, war_ref[...], preferred_element_type=F32)) * u_r
         + _sigmoid(jnp.dot(h, wag_ref[...], preferred_element_type=F32)) * u_g)
    r = x_ref[...] + jnp.dot(m.astype(BF16), wo_ref[...], preferred_element_type=F32)
    hm = _rms(r, nm_ref[...]).astype(BF16)
    a = jnp.maximum(jnp.dot(hm, w1_ref[...], preferred_element_type=F32), 0.0)
    r = r + jnp.dot((a * a).astype(BF16), w2_ref[...], preferred_element_type=F32)
    hp = _rms(r, np_ref[...]).astype(BF16)
    gate = _sigmoid(jnp.dot(hp, wg_ref[...], preferred_element_type=F32))
    r = r + gate * jnp.dot(p_ref[...].astype(BF16), wp_ref[...], preferred_element_type=F32)
    y_ref[...] = _rms(r, nf_ref[...])


def _post(x, h, g_r, g_g, p, war, wag, wur, wug, wo, nm, w1, w2, npl, wg, wp, nf, tm, step=None):
    n, d = x.shape
    grid = n // tm
    rows = lambda w: pl.BlockSpec((tm, w), lambda i: (i, 0))
    const = lambda a: pl.BlockSpec(a.shape, lambda i: (0, 0), pipeline_mode=pl.Buffered(1))
    operands = (x, h, g_r, g_g, p, war, wag, wur, wug, wo, nm, w1, w2, npl, wg, wp, nf)
    in_specs = [rows(d), rows(d), rows(d), rows(d), rows(p.shape[1]),
                const(war), const(wag), const(wur), const(wug), const(wo), const(nm), const(w1), const(w2),
                const(npl), const(wg), const(wp), const(nf)]
    assert len(operands) == N_POST_IN
    out_specs = [rows(d)]
    out_shape = [jax.ShapeDtypeStruct((n, d), F32)]
    bb = 0
    if step is not None:
        bb = step[0].shape[0] // grid
        assert bb * grid == step[0].shape[0] and len(step) == N_STEP_IN
        s_operands, s_in, s_out, s_shape = _step_specs(*step, bb=bb)
        operands += s_operands
        in_specs += s_in
        out_specs += s_out
        out_shape += s_shape
    out = pl.pallas_call(
        functools.partial(_post_kernel, bb=bb),
        grid=(grid,),
        in_specs=in_specs,
        out_specs=out_specs,
        out_shape=out_shape,
        compiler_params=_params(("parallel",)),
        name="post",
    )(*operands)
    return out if bb else out[0]


def _rope_tables(pos):
    half = DK_R // 2
    inv = ROPE_BASE ** (-jnp.arange(half, dtype=F32) / half)
    ang = pos[:, None] * inv[None, :]
    cos, sin = jnp.cos(ang), jnp.sin(ang)
    return jnp.concatenate([cos, cos], axis=-1), jnp.concatenate([-sin, sin], axis=-1)


def kernel(x_prompt, x_sample, state_ret, state_hgrn, p_prompt, p_sample, norm_mix_g, w_in,
           ret_norm_g, hg_norm_g, hg_lb, w_up_ret, w_up_hg, w_o, norm_mlp_g, w_ff1, w_ff2,
           norm_ple_g, w_ple_gate, w_ple_proj, norm_final_g):
    b, t, d = x_prompt.shape
    nb, td, _ = x_sample.shape
    assert w_in.shape[0] == 1 and td == 1 and hg_lb.shape[0] == 2
    row = lambda g: g.reshape(1, -1)
    merge_block = (2 * QK_R + 2 * V_R + 2 * F_G + 2 * V_G) // d
    g_mix = row(norm_mix_g[0])
    rgain = row(ret_norm_g[0])
    ggain = row(hg_norm_g[0])

    xs = x_sample.reshape(nb, d)
    hs = _norm(xs, g_mix, tm=nb)
    cos_s, sin_s = _rope_tables(PAST_LEN + jnp.arange(td, dtype=F32))

    xp = x_prompt.reshape(b * t, d)
    cos_p, sin_p = _rope_tables(jnp.arange(t, dtype=F32))
    casts = [(w_in, 1, merge_block), (w_in, 1, merge_block + 1), (w_up_ret, 1, None), (w_up_hg, 1, None),
             (w_o, 1, None), (w_ff1, 2, None), (w_ff2, 1, None), (w_ple_gate, 1, None)]
    gr_p, ret_p, hp3, war, wag, wur, wug, wo, w1, w2, wg, zs = _retention(
        x_prompt, g_mix, w_in, cos_p, sin_p, rgain, casts, hs)
    step = (zs, cos_s, sin_s, hg_lb, rgain, ggain, state_ret[0], state_hgrn[0])
    post_w = (war, wag, wur, wug, wo, row(norm_mlp_g[0]), w1, w2, row(norm_ple_g[0]), wg,
              w_ple_proj[0].astype(BF16), row(norm_final_g))
    gg_p, hg_p = _hgrn(hp3, w_in, hg_lb, ggain)
    y_p, gr_s, gg_s, ret_s, hg_s = _post(
        xp, hp3.reshape(b * t, d), gr_p.reshape(b * t, -1), gg_p.reshape(b * t, -1),
        p_prompt[0].reshape(b * t, -1), *post_w, tm=256, step=step)

    y_s = _post(xs, hs, gr_s.reshape(nb, -1), gg_s.reshape(nb, -1),
                p_sample[0].reshape(nb, -1), *post_w, tm=nb)

    return (y_p.reshape(b, t, d), y_s.reshape(nb, td, d), ret_p[None], hg_p[None],
            ret_s[None], hg_s[None])
```

```python
import functools

import jax
import jax.numpy as jnp
import numpy as np
from jax import lax
from jax.experimental import pallas as pl
from jax.experimental.pallas import tpu as pltpu

F32 = jnp.float32
BF16 = jnp.bfloat16

PAST_LEN = 16384
ROPE_BASE = 10000.0
EPS = 1e-6

H_R, DK_R, DV_R = 4, 128, 256
H_G, DK_G, DV_G = 8, 128, 128
QK_R = H_R * DK_R
V_R = H_R * DV_R
F_G = H_G * DK_G
V_G = H_G * DV_G

V7X_VMEM_BYTES = 64 * 1024 * 1024
VMEM_LIMIT = V7X_VMEM_BYTES - 8 * 1024 * 1024

NORM_ROWS = 256
RET_CHUNK = 128
RET_GROUP = 2
RET_LAG = 1
HG_CHUNK = 64
HG_SUB = 8
HG_GROUP = 4
HG_LAG = 2
HG_HEADS = 2

NT_DIMS = (((1,), (1,)), ((), ()))
TN_DIMS = (((0,), (0,)), ((), ()))


def _dot(a, b):
    return jnp.dot(a.astype(BF16), b.astype(BF16), preferred_element_type=F32)


def _dot_nt(a, b):
    return lax.dot_general(a.astype(BF16), b.astype(BF16), NT_DIMS, preferred_element_type=F32)


def _dot_tn(a, b):
    return lax.dot_general(a.astype(BF16), b.astype(BF16), TN_DIMS, preferred_element_type=F32)


def _rms(x, g):
    return x * lax.rsqrt(jnp.mean(x * x, axis=-1, keepdims=True) + EPS) * g


def _sigmoid(x):
    return 0.5 * jnp.tanh(0.5 * x) + 0.5


def _silu(x):
    half = 0.5 * x
    return half + half * jnp.tanh(half)


def _params(sem):
    return pltpu.CompilerParams(dimension_semantics=sem, vmem_limit_bytes=VMEM_LIMIT)


def _norm_kernel(x_ref, g_ref, h_ref):
    h_ref[...] = _rms(x_ref[...], g_ref[...]).astype(BF16)


def _norm(x, g, tm):
    n, d = x.shape
    return pl.pallas_call(
        _norm_kernel,
        grid=(n // tm,),
        in_specs=[pl.BlockSpec((tm, d), lambda i: (i, 0)), pl.BlockSpec((1, d), lambda i: (0, 0))],
        out_specs=pl.BlockSpec((tm, d), lambda i: (i, 0)),
        out_shape=jax.ShapeDtypeStruct((n, d), BF16),
        compiler_params=_params(("parallel",)),
        name="norm",
    )(x, g)


def _rotate(x, cos, sin_signed):
    return x * cos + pltpu.roll(x, DK_R // 2, 1) * sin_signed


def _head_norm_gate(o, gain, gate):
    return o * lax.rsqrt(jnp.mean(o * o, axis=-1, keepdims=True) + EPS) * gain * _silu(gate)


def _ret_pipeline(h_ref, wq_ref, wk_ref, wv_ref, wg_ref, cos_ref, sin_ref, dm_ref, qd_ref, kd_ref,
                  cd_ref, gain_ref, o_ref, s_ref, w_sc, z_sc, *, chunk, group, lag):
    n_chunks = h_ref.shape[0] // chunk
    assert n_chunks % group == 0 and lag >= 1
    g_rows = group * chunk
    q_cols = slice(0, DK_R)
    k_cols = slice(DK_R, 2 * DK_R)
    v_cols = slice(2 * DK_R, 2 * DK_R + DV_R)
    g_cols = slice(2 * DK_R + DV_R, 2 * DK_R + 2 * DV_R)
    for cols, w_ref in ((q_cols, wq_ref), (k_cols, wk_ref), (v_cols, wv_ref), (g_cols, wg_ref)):
        w_sc[:, cols] = w_ref[...].astype(BF16)
    dm = dm_ref[...]
    qd = qd_ref[...]
    kd = kd_ref[...]
    cd = cd_ref[...]
    gain = gain_ref[...]

    assert group == 2
    pieces = (slice(0, g_cols.start), g_cols)

    def project(c):
        rows = slice(c // group * g_rows, (c // group + 1) * g_rows)
        pcols = pieces[c % group]
        z_sc[rows, pcols] = jnp.dot(h_ref[rows, :], w_sc[:, pcols], preferred_element_type=F32)

    def scores(r0):
        rows = slice(r0, r0 + chunk)
        cos = cos_ref[rows, :]
        sin = sin_ref[rows, :]
        q = _rotate(z_sc[rows, q_cols], cos, sin)
        k = _rotate(z_sc[rows, k_cols], cos, sin) * (DK_R ** -0.5)
        a = (_dot_nt(q, k) * dm).astype(BF16)
        return a, (q * qd).astype(BF16), (k * kd).astype(BF16)

    def outputs(r0, parts, s):
        a, q_in, k_st = parts
        rows = slice(r0, r0 + chunk)
        v = z_sc[rows, v_cols].astype(BF16)
        o = jnp.dot(jnp.concatenate([a, q_in], axis=1), jnp.concatenate([v, s.astype(BF16)], axis=0),
                    preferred_element_type=F32)
        o_ref[rows, :] = _head_norm_gate(o, gain, z_sc[rows, g_cols])
        return cd * s + lax.dot_general(k_st, v, TN_DIMS, preferred_element_type=F32)

    carry = {"s": jnp.zeros((DK_R, DV_R), F32)}
    pending = {}

    def do_scores(c):
        pending[c] = scores(c * chunk)

    def do_outputs(c):
        carry["s"] = outputs(c * chunk, pending.pop(c), carry["s"])

    work = []
    for step in range(-group, n_chunks + lag):
        if step + group < n_chunks:
            work.append(functools.partial(project, step + group))
        if 0 <= step < n_chunks:
            work.append(functools.partial(do_scores, step))
        if 0 <= step - lag < n_chunks:
            work.append(functools.partial(do_outputs, step - lag))

    def finish():
        s_ref[...] = carry["s"]

    return work, finish


def _retention_tables(c):
    lg = np.log(1.0 - 2.0 ** (-5.0 - np.arange(H_R, dtype=np.float64)))
    idx = np.arange(c, dtype=np.float64)
    rel = idx[:, None] - idx[None, :]
    dm = np.where(rel >= 0, np.exp(lg[:, None, None] * np.maximum(rel, 0.0)), 0.0)
    qd = np.broadcast_to(np.exp(lg[:, None] * (idx + 1.0))[:, :, None], (H_R, c, DK_R))
    kd = np.broadcast_to(np.exp(lg[:, None] * (c - 1.0 - idx))[:, :, None], (H_R, c, DK_R))
    cd = np.broadcast_to(np.exp(lg * c)[:, None, None], (H_R, 1, DV_R))
    return tuple(jnp.asarray(a, F32) for a in (dm, qd, kd, cd))


def _lower_bound(lb_ref):
    a = lb_ref[...]
    a0, a1 = a[0:1, :], a[1:2, :]
    m = jnp.maximum(a0, a1)
    e0 = jnp.exp(a0 - m)
    e1 = jnp.exp(a1 - m)
    return e0 / (e0 + e1)


def _forget_and_key(g, lb):
    half_scale = 0.5 * (1.0 - lb)
    th_scaled = half_scale * jnp.tanh(0.5 * g)
    return (lb + half_scale) + th_scaled, half_scale - th_scaled


def _cumsum_rows(tri, x):
    hi = x.astype(BF16)
    rest = x - hi.astype(F32)
    mid = rest.astype(BF16)
    lo = (rest - mid.astype(F32)).astype(BF16)
    y = jnp.dot(tri, jnp.concatenate([hi, mid, lo], axis=1), preferred_element_type=F32)
    n = x.shape[1]
    return y[:, :n] + y[:, n:2 * n] + y[:, 2 * n:]


def _hgrn_pipeline(h_ref, wq_ref, wf_ref, wi_ref, wg_ref, lb_ref, gain_ref, o_ref, s_ref,
                   w_sc, z_sc, q_sc, b_sc, c_sc, *, chunk, sub, group, lag, heads):
    assert sub == 8 and chunk % (2 * sub) == 0
    g_rows = group * chunk
    n_chunks = h_ref.shape[0] // chunk
    assert n_chunks % group == 0 and lag >= 1
    lanes = lambda hh: slice(hh * DK_G, (hh + 1) * DK_G)
    cols = lambda hh, role: lanes(4 * hh + role)
    for hh in range(heads):
        for role, w_ref in enumerate((wq_ref, wf_ref, wi_ref, wg_ref)):
            w_sc[:, cols(hh, role)] = w_ref[:, lanes(hh)].astype(BF16)

    lb = _lower_bound(lb_ref)
    gain = gain_ref[...]
    r64 = lax.broadcasted_iota(jnp.int32, (chunk, chunk), 0)
    c64 = lax.broadcasted_iota(jnp.int32, (chunk, chunk), 1)
    tri = jnp.where(r64 >= c64, 1.0, 0.0).astype(BF16)
    same_sub = (r64 // sub) == (c64 // sub)
    place = [(c64[:sub] % sub == s) & (r64[:sub] >= s) for s in range(sub)]
    zrows = lambda n: jnp.zeros((n, DK_G), F32)

    piece_cols = z_sc.shape[1] // group
    assert piece_cols % DK_G == 0

    def project(c):
        rows = slice(c // group * g_rows, (c // group + 1) * g_rows)
        pcols = slice(c % group * piece_cols, (c % group + 1) * piece_cols)
        z_sc[rows, pcols] = jnp.dot(h_ref[rows, :], w_sc[:, pcols], preferred_element_type=F32)

    def gates(hh, r0):
        rows = slice(r0, r0 + chunk)
        f, k = _forget_and_key(z_sc[rows, cols(hh, 1)], lb[:, lanes(hh)])
        b_sc[rows, lanes(hh)] = _cumsum_rows(tri, jnp.log2(f))
        c_sc[rows, lanes(hh)] = b_sc[rows, lanes(hh)] - jnp.log2(k)
        q_sc[rows, lanes(hh)] = _silu(z_sc[rows, cols(hh, 0)])

    def diagonal(hh, r0):
        blocks = []
        for o in range(r0, r0 + chunk, sub):
            q, b = q_sc[o:o + sub, lanes(hh)], b_sc[o:o + sub, lanes(hh)]
            acc = jnp.zeros((sub, chunk), F32)
            for s in range(sub):
                p = jnp.exp2(b - c_sc[o + s:o + s + 1, lanes(hh)]) * q
                acc = jnp.where(place[s], jnp.sum(p, axis=-1, keepdims=True), acc)
            blocks.append(acc)
        return jnp.where(same_sub, jnp.concatenate(blocks, axis=0), 0.0)

    def below(hh, r0, w):
        qs, ks = [], []
        for mid in range(r0 + w, r0 + chunk, 2 * w):
            ref_b = b_sc[mid - 1:mid, lanes(hh)]
            qs += [zrows(w), q_sc[mid:mid + w, lanes(hh)] * jnp.exp2(b_sc[mid:mid + w, lanes(hh)] - ref_b)]
            ks += [jnp.exp2(ref_b - c_sc[mid - w:mid, lanes(hh)]), zrows(w)]
        scores = _dot_nt(jnp.concatenate(qs, axis=0), jnp.concatenate(ks, axis=0))
        return jnp.where(c64 // w == r64 // w - 1, scores, 0.0)

    def scores(hh, r0):
        a = diagonal(hh, r0)
        w = chunk // 2
        while w >= sub:
            a = a + below(hh, r0, w)
            w //= 2
        return a

    def outputs(hh, r0, a, st):
        rows = slice(r0, r0 + chunk)
        v = z_sc[rows, cols(hh, 2)].astype(BF16)
        b_last = b_sc[rows.stop - 1:rows.stop, lanes(hh)]
        o = _dot(a, v) + _dot_nt(q_sc[rows, lanes(hh)] * jnp.exp2(b_sc[rows, lanes(hh)]), st)
        o_ref[rows, lanes(hh)] = _head_norm_gate(o, gain[:, lanes(hh)], z_sc[rows, cols(hh, 3)])
        return st * jnp.exp2(b_last) + _dot_tn(v, jnp.exp2(b_last - c_sc[rows, lanes(hh)]))

    st = [jnp.zeros((DV_G, DK_G), F32) for _ in range(heads)]
    pending = {}

    def do_scores(hh, c):
        pending[hh, c] = scores(hh, c * chunk)

    def do_outputs(hh, c):
        st[hh] = outputs(hh, c * chunk, pending.pop((hh, c)), st[hh])

    work = []
    for step in range(-group, n_chunks + 2 * lag):
        if step + group < n_chunks:
            work.append(functools.partial(project, step + group))
        for hh in range(heads):
            if 0 <= step < n_chunks:
                work.append(functools.partial(gates, hh, step * chunk))
            if 0 <= step - lag < n_chunks:
                work.append(functools.partial(do_scores, hh, step - lag))
            if 0 <= step - 2 * lag < n_chunks:
                work.append(functools.partial(do_outputs, hh, step - 2 * lag))

    def finish():
        for hh in range(heads):
            s_ref[hh] = st[hh].T

    return work, finish


def _run_pipeline(build, **config):
    def kernel(*refs):
        work, finish = build(*refs, **config)
        for item in work:
            item()
        finish()
    return kernel


N_RET_IN = 11


def _ret_kernel(x_ref, g_ref, *refs, norm_rows, n_cast, **config):
    ret_in, cast_in = refs[:N_RET_IN], refs[N_RET_IN:N_RET_IN + n_cast]
    hs_ref, wz_ref = refs[N_RET_IN + n_cast:N_RET_IN + n_cast + 2]
    outs = refs[N_RET_IN + n_cast + 2:]
    (o_ref, s_ref, hn_ref), cast_out, zs_ref, scratch = outs[:3], outs[3:3 + n_cast], outs[3 + n_cast], outs[4 + n_cast:]

    @pl.when(pl.program_id(1) == 0)
    def _():
        def body(i, carry):
            rows = pl.ds(pl.multiple_of(i * norm_rows, norm_rows), norm_rows)
            hn_ref[rows, :] = _rms(x_ref[rows, :], g_ref[...]).astype(BF16)
            return carry

        lax.fori_loop(0, x_ref.shape[0] // norm_rows, body, 0)

    for src, dst in zip(cast_in, cast_out):
        dst[...] = src[...].astype(BF16)
    zs_ref[...] = jnp.dot(hs_ref[...], wz_ref[...].astype(BF16), preferred_element_type=F32)
    work, finish = _ret_pipeline(hn_ref, *ret_in, o_ref, s_ref, *scratch, **config)
    for item in work:
        item()
    finish()


def _retention(x3, g_mix, w_in, cos, sin, gain, casts, hs):
    b, t, d = x3.shape
    n_head = 2 * DK_R + 2 * DV_R
    steps = b * H_R
    d_in = w_in.shape[2]
    z_cols = DK_R * -(-d_in // (DK_R * steps))
    assert d_in % z_cols == 0
    z_block = lambda bi, h: jnp.minimum(bi * H_R + h, d_in // z_cols - 1)
    w_block = lambda width, lo: pl.BlockSpec((None, d, width), lambda bi, h: (0, 0, lo // width + h))
    head_tab = lambda rows, cols: pl.BlockSpec((None, rows, cols), lambda bi, h: (h, 0, 0))
    cast_in, cast_out, cast_shape = [], [], []
    for a, axis, col_block in casts:
        _, rows, cols = a.shape
        if axis == 1:
            n = rows // steps
            cols = cols if col_block is None else d
            cb = 0 if col_block is None else col_block
            cast_in.append(pl.BlockSpec((None, n, cols), lambda bi, h, cb=cb: (0, bi * H_R + h, cb)))
            cast_out.append(pl.BlockSpec((n, cols), lambda bi, h: (bi * H_R + h, 0)))
        else:
            n = cols // steps
            cast_in.append(pl.BlockSpec((None, rows, n), lambda bi, h: (0, 0, bi * H_R + h)))
            cast_out.append(pl.BlockSpec((rows, n), lambda bi, h: (0, bi * H_R + h)))
        cast_shape.append(jax.ShapeDtypeStruct((rows, cols), BF16))
    return pl.pallas_call(
        functools.partial(_ret_kernel, norm_rows=NORM_ROWS, n_cast=len(casts),
                          chunk=RET_CHUNK, group=RET_GROUP, lag=RET_LAG),
        grid=(b, H_R),
        in_specs=[
            pl.BlockSpec((None, t, d), lambda bi, h: (bi, 0, 0)),
            pl.BlockSpec((1, d), lambda bi, h: (0, 0)),
            w_block(DK_R, 0), w_block(DK_R, QK_R), w_block(DV_R, 2 * QK_R), w_block(DV_R, 2 * QK_R + V_R),
            pl.BlockSpec((t, DK_R), lambda bi, h: (0, 0)),
            pl.BlockSpec((t, DK_R), lambda bi, h: (0, 0)),
            head_tab(RET_CHUNK, RET_CHUNK), head_tab(RET_CHUNK, DK_R), head_tab(RET_CHUNK, DK_R),
            head_tab(1, DV_R),
            pl.BlockSpec((1, DV_R), lambda bi, h: (0, h)),
        ] + cast_in + [
            pl.BlockSpec(hs.shape, lambda bi, h: (0, 0)),
            pl.BlockSpec((None, d, z_cols), lambda bi, h: (0, 0, z_block(bi, h))),
        ],
        out_specs=[
            pl.BlockSpec((None, t, DV_R), lambda bi, h: (bi, 0, h)),
            pl.BlockSpec((None, None, DK_R, DV_R), lambda bi, h: (bi, h, 0, 0)),
            pl.BlockSpec((None, t, d), lambda bi, h: (bi, 0, 0)),
        ] + cast_out + [pl.BlockSpec((hs.shape[0], z_cols), lambda bi, h: (0, z_block(bi, h)))],
        out_shape=[
            jax.ShapeDtypeStruct((b, t, V_R), F32),
            jax.ShapeDtypeStruct((b, H_R, DK_R, DV_R), F32),
            jax.ShapeDtypeStruct((b, t, d), BF16),
        ] + cast_shape + [jax.ShapeDtypeStruct((hs.shape[0], d_in), F32)],
        scratch_shapes=[pltpu.VMEM((d, n_head), BF16), pltpu.VMEM((t, n_head), F32)],
        compiler_params=_params(("arbitrary", "arbitrary")),
        name="retention",
    )(x3, g_mix, w_in, w_in, w_in, w_in, cos, sin, *_retention_tables(RET_CHUNK), gain,
      *(a for a, _, _ in casts), hs, w_in)


def _hgrn(h3, w_in, hg_lb, gain):
    b, t, d = h3.shape
    heads = HG_HEADS
    wide = heads * DK_G
    first = (2 * QK_R + 2 * V_R) // wide
    w_block = lambda role: pl.BlockSpec((None, d, wide), lambda bi, h: (0, 0, first + role * (H_G // heads) + h))
    return pl.pallas_call(
        _run_pipeline(_hgrn_pipeline, chunk=HG_CHUNK, sub=HG_SUB, group=HG_GROUP, lag=HG_LAG, heads=heads),
        grid=(b, H_G // heads),
        in_specs=[
            pl.BlockSpec((None, t, d), lambda bi, h: (bi, 0, 0)),
            w_block(0), w_block(1), w_block(2), w_block(3),
            pl.BlockSpec((2, wide), lambda bi, h: (0, h)),
            pl.BlockSpec((1, wide), lambda bi, h: (0, h)),
        ],
        out_specs=[
            pl.BlockSpec((None, t, wide), lambda bi, h: (bi, 0, h)),
            pl.BlockSpec((None, heads, DK_G, DV_G), lambda bi, h: (bi, h, 0, 0)),
        ],
        out_shape=[
            jax.ShapeDtypeStruct((b, t, V_G), F32),
            jax.ShapeDtypeStruct((b, H_G, DK_G, DV_G), F32),
        ],
        scratch_shapes=[
            pltpu.VMEM((d, 4 * wide), BF16),
            pltpu.VMEM((t, 4 * wide), F32),
            pltpu.VMEM((t, wide), F32),
            pltpu.VMEM((t, wide), F32),
            pltpu.VMEM((t, wide), F32),
        ],
        compiler_params=_params(("parallel", "arbitrary")),
        name="hgrn",
    )(h3, w_in, w_in, w_in, w_in, hg_lb, gain)


def _step_kernel(z_ref, cos_ref, sin_ref, lb_ref, rgain_ref, ggain_ref, sr0_ref, sg0_ref,
                 or_ref, og_ref, sr_ref, sg_ref, *, bb):
    first = lambda a: jnp.where(lax.broadcasted_iota(jnp.int32, a.shape, 0) == 0, a, 0.0)
    cos = cos_ref[...]
    sin = sin_ref[...]
    lb_all = _lower_bound(lb_ref)
    row8 = lax.broadcasted_iota(jnp.int32, (8, DK_G), 0)
    ones3 = jnp.where(row8 < 3, 1.0, 0.0)
    g0 = 2 * QK_R + 2 * V_R
    base = (pl.program_id(0) % (8 // bb)) * bb
    work = []
    for i in range(bb):
        def z_row(lo, n, i=i):
            mine = lax.broadcasted_iota(jnp.int32, (8, n), 0) == base + i
            return jnp.sum(jnp.where(mine, z_ref[:, lo:lo + n], 0.0), axis=0, keepdims=True)

        z8 = lambda lo, n, z_row=z_row: jnp.broadcast_to(z_row(lo, n), (8, n))
        for h in range(H_R):
            q = _rotate(z8(h * DK_R, DK_R), cos, sin)
            k = _rotate(z8(QK_R + h * DK_R, DK_R), cos, sin) * (DK_R ** -0.5)
            v = z8(2 * QK_R + h * DV_R, DV_R)
            lanes = slice(h * DV_R, (h + 1) * DV_R)
            gate = z_row(2 * QK_R + V_R + h * DV_R, DV_R)
            work.append((sr0_ref, sr_ref, or_ref, rgain_ref, i, h, lanes, q, gate,
                         1.0 - 2.0 ** (-5.0 - h), _dot_tn(first(k), v)))
        for h in range(H_G):
            lanes = slice(h * DK_G, (h + 1) * DK_G)
            q = _silu(z8(g0 + h * DK_G, DK_G))
            f, k = _forget_and_key(z8(g0 + F_G + h * DK_G, DK_G), lb_all[:, lanes])
            v = z8(g0 + 2 * F_G + h * DV_G, DV_G)
            gate = z_row(g0 + 2 * F_G + V_G + h * DV_G, DV_G)
            f_hi = f.astype(BF16).astype(F32)
            f_mid = (f - f_hi).astype(BF16).astype(F32)
            f_lo = f - f_hi - f_mid
            lhs = jnp.where(row8 == 0, f_hi, jnp.where(row8 == 1, f_mid, jnp.where(row8 == 2, f_lo,
                            jnp.where(row8 == 3, k, 0.0))))
            rhs = jnp.concatenate([ones3, jnp.where(row8 == 3, v, 0.0)], axis=1)
            both = _dot_tn(lhs, rhs)
            work.append((sg0_ref, sg_ref, og_ref, ggain_ref, i, h, lanes, q, gate,
                         both[:, :DV_G], both[:, DV_G:]))
    outs = []
    for s0_ref, s_ref, _, _, i, h, _, q, _, decay, kv in work:
        s_new = decay * s0_ref[i, h] + kv
        s_ref[i, h] = s_new
        outs.append(_dot(q, s_new)[0:1, :])
    for (_, _, o_ref, gain_ref, i, _, lanes, _, gate, _, _), o in zip(work, outs):
        o_ref[i, :, lanes] = _head_norm_gate(o, gain_ref[:, lanes], gate)


def _step_specs(z, cos, sin, hg_lb, rgain, ggain, sr0, sg0, bb):
    n, d_in = z.shape
    assert 8 % bb == 0
    row = lambda w: pl.BlockSpec((bb, 1, w), lambda i: (i, 0, 0))
    const = lambda a: pl.BlockSpec(a.shape, lambda i: (0,) * a.ndim)
    sr_spec = pl.BlockSpec((bb, H_R, DK_R, DV_R), lambda i: (i, 0, 0, 0))
    sg_spec = pl.BlockSpec((bb, H_G, DK_G, DV_G), lambda i: (i, 0, 0, 0))
    operands = (z, cos, sin, hg_lb, rgain, ggain, sr0, sg0)
    in_specs = [pl.BlockSpec((8, d_in), lambda i: (i * bb // 8, 0)),
                const(cos), const(sin), const(hg_lb), const(rgain), const(ggain), sr_spec, sg_spec]
    out_specs = [row(V_R), row(V_G), sr_spec, sg_spec]
    out_shape = [
        jax.ShapeDtypeStruct((n, 1, V_R), F32),
        jax.ShapeDtypeStruct((n, 1, V_G), F32),
        jax.ShapeDtypeStruct(sr0.shape, F32),
        jax.ShapeDtypeStruct(sg0.shape, F32),
    ]
    return operands, in_specs, out_specs, out_shape


N_POST_IN = 17
N_STEP_IN = 8


def _post_kernel(*refs, bb):
    if bb:
        _post_body(*refs[:N_POST_IN], refs[N_POST_IN + N_STEP_IN])
        _step_kernel(*refs[N_POST_IN:N_POST_IN + N_STEP_IN], *refs[N_POST_IN + N_STEP_IN + 1:], bb=bb)
    else:
        _post_body(*refs)


def _post_body(x_ref, h_ref, gr_ref, gg_ref, p_ref, war_ref, wag_ref, wur_ref, wug_ref, wo_ref,
               nm_ref, w1_ref, w2_ref, np_ref, wg_ref, wp_ref, nf_ref, y_ref):
    h = h_ref[...]
    u_r = jnp.dot(gr_ref[...].astype(BF16), wur_ref[...], preferred_element_type=F32)
    u_g = jnp.dot(gg_ref[...].astype(BF16), wug_ref[...], preferred_element_type=F32)
    m = (_sigmoid(jnp.dot(h, war_ref[...], preferred_element_type=F32)) * u_r
         + _sigmoid(jnp.dot(h, wag_ref[...], preferred_element_type=F32)) * u_g)
    r = x_ref[...] + jnp.dot(m.astype(BF16), wo_ref[...], preferred_element_type=F32)
    hm = _rms(r, nm_ref[...]).astype(BF16)
    a = jnp.maximum(jnp.dot(hm, w1_ref[...], preferred_element_type=F32), 0.0)
    r = r + jnp.dot((a * a).astype(BF16), w2_ref[...], preferred_element_type=F32)
    hp = _rms(r, np_ref[...]).astype(BF16)
    gate = _sigmoid(jnp.dot(hp, wg_ref[...], preferred_element_type=F32))
    r = r + gate * jnp.dot(p_ref[...].astype(BF16), wp_ref[...], preferred_element_type=F32)
    y_ref[...] = _rms(r, nf_ref[...])


def _post(x, h, g_r, g_g, p, war, wag, wur, wug, wo, nm, w1, w2, npl, wg, wp, nf, tm, step=None):
    n, d = x.shape
    grid = n // tm
    rows = lambda w: pl.BlockSpec((tm, w), lambda i: (i, 0))
    const = lambda a: pl.BlockSpec(a.shape, lambda i: (0, 0), pipeline_mode=pl.Buffered(1))
    operands = (x, h, g_r, g_g, p, war, wag, wur, wug, wo, nm, w1, w2, npl, wg, wp, nf)
    in_specs = [rows(d), rows(d), rows(d), rows(d), rows(p.shape[1]),
                const(war), const(wag), const(wur), const(wug), const(wo), const(nm), const(w1), const(w2),
                const(npl), const(wg), const(wp), const(nf)]
    assert len(operands) == N_POST_IN
    out_specs = [rows(d)]
    out_shape = [jax.ShapeDtypeStruct((n, d), F32)]
    bb = 0
    if step is not None:
        bb = step[0].shape[0] // grid
        assert bb * grid == step[0].shape[0] and len(step) == N_STEP_IN
        s_operands, s_in, s_out, s_shape = _step_specs(*step, bb=bb)
        operands += s_operands
        in_specs += s_in
        out_specs += s_out
        out_shape += s_shape
    out = pl.pallas_call(
        functools.partial(_post_kernel, bb=bb),
        grid=(grid,),
        in_specs=in_specs,
        out_specs=out_specs,
        out_shape=out_shape,
        compiler_params=_params(("parallel",)),
        name="post",
    )(*operands)
    return out if bb else out[0]


def _rope_tables(pos):
    half = DK_R // 2
    inv = ROPE_BASE ** (-np.arange(half, dtype=np.float64) / half)
    ang = np.asarray(pos, np.float64)[:, None] * inv[None, :]
    cos, sin = np.cos(ang), np.sin(ang)
    return (jnp.asarray(np.concatenate([cos, cos], axis=-1), F32),
            jnp.asarray(np.concatenate([-sin, sin], axis=-1), F32))


def kernel(x_prompt, x_sample, state_ret, state_hgrn, p_prompt, p_sample, norm_mix_g, w_in,
           ret_norm_g, hg_norm_g, hg_lb, w_up_ret, w_up_hg, w_o, norm_mlp_g, w_ff1, w_ff2,
           norm_ple_g, w_ple_gate, w_ple_proj, norm_final_g):
    b, t, d = x_prompt.shape
    nb, td, _ = x_sample.shape
    assert w_in.shape[0] == 1 and td == 1 and hg_lb.shape[0] == 2
    row = lambda g: g.reshape(1, -1)
    merge_block = (2 * QK_R + 2 * V_R + 2 * F_G + 2 * V_G) // d
    g_mix = row(norm_mix_g[0])
    rgain = row(ret_norm_g[0])
    ggain = row(hg_norm_g[0])

    xs = x_sample.reshape(nb, d)
    hs = _norm(xs, g_mix, tm=nb)
    cos_s, sin_s = _rope_tables(PAST_LEN + np.arange(td))

    xp = x_prompt.reshape(b * t, d)
    cos_p, sin_p = _rope_tables(np.arange(t))
    casts = [(w_in, 1, merge_block), (w_in, 1, merge_block + 1), (w_up_ret, 1, None), (w_up_hg, 1, None),
             (w_o, 1, None), (w_ff1, 2, None), (w_ff2, 1, None), (w_ple_gate, 1, None)]
    gr_p, ret_p, hp3, war, wag, wur, wug, wo, w1, w2, wg, zs = _retention(
        x_prompt, g_mix, w_in, cos_p, sin_p, rgain, casts, hs)
    step = (zs, cos_s, sin_s, hg_lb, rgain, ggain, state_ret[0], state_hgrn[0])
    post_w = (war, wag, wur, wug, wo, row(norm_mlp_g[0]), w1, w2, row(norm_ple_g[0]), wg,
              w_ple_proj[0].astype(BF16), row(norm_final_g))
    gg_p, hg_p = _hgrn(hp3, w_in, hg_lb, ggain)
    y_p, gr_s, gg_s, ret_s, hg_s = _post(
        xp, hp3.reshape(b * t, d), gr_p.reshape(b * t, -1), gg_p.reshape(b * t, -1),
        p_prompt[0].reshape(b * t, -1), *post_w, tm=256, step=step)

    y_s = _post(xs, hs, gr_s.reshape(nb, -1), gg_s.reshape(nb, -1),
                p_sample[0].reshape(nb, -1), *post_w, tm=nb)

    return (y_p.reshape(b, t, d), y_s.reshape(nb, td, d), ret_p[None], hg_p[None],
            ret_s[None], hg_s[None])
```

```python
import functools

import jax
import jax.numpy as jnp
import numpy as np
from jax import lax
from jax.experimental import pallas as pl
from jax.experimental.pallas import tpu as pltpu

F32 = jnp.float32
BF16 = jnp.bfloat16

PAST_LEN = 16384
ROPE_BASE = 10000.0
EPS = 1e-6

H_R, DK_R, DV_R = 4, 128, 256
H_G, DK_G, DV_G = 8, 128, 128
QK_R = H_R * DK_R
V_R = H_R * DV_R
F_G = H_G * DK_G
V_G = H_G * DV_G

V7X_VMEM_BYTES = 64 * 1024 * 1024
VMEM_LIMIT = V7X_VMEM_BYTES - 8 * 1024 * 1024

NORM_ROWS = 256
RET_CHUNK = 128
RET_GROUP = 2
RET_LAG = 1
HG_CHUNK = 64
HG_SUB = 8
HG_GROUP = 4
HG_LAG = 2
HG_HEADS = 2

NT_DIMS = (((1,), (1,)), ((), ()))
TN_DIMS = (((0,), (0,)), ((), ()))


def _dot(a, b):
    return jnp.dot(a.astype(BF16), b.astype(BF16), preferred_element_type=F32)


def _dot_nt(a, b):
    return lax.dot_general(a.astype(BF16), b.astype(BF16), NT_DIMS, preferred_element_type=F32)


def _dot_tn(a, b):
    return lax.dot_general(a.astype(BF16), b.astype(BF16), TN_DIMS, preferred_element_type=F32)


def _rms(x, g):
    return x * lax.rsqrt(jnp.mean(x * x, axis=-1, keepdims=True) + EPS) * g


def _sigmoid(x):
    return 0.5 * jnp.tanh(0.5 * x) + 0.5


def _silu(x):
    half = 0.5 * x
    return half + half * jnp.tanh(half)


def _params(sem):
    return pltpu.CompilerParams(dimension_semantics=sem, vmem_limit_bytes=VMEM_LIMIT)


def _norm_kernel(x_ref, g_ref, h_ref):
    h_ref[...] = _rms(x_ref[...], g_ref[...]).astype(BF16)


def _norm(x, g, tm):
    n, d = x.shape
    return pl.pallas_call(
        _norm_kernel,
        grid=(n // tm,),
        in_specs=[pl.BlockSpec((tm, d), lambda i: (i, 0)), pl.BlockSpec((1, d), lambda i: (0, 0))],
        out_specs=pl.BlockSpec((tm, d), lambda i: (i, 0)),
        out_shape=jax.ShapeDtypeStruct((n, d), BF16),
        compiler_params=_params(("parallel",)),
        name="norm",
    )(x, g)


def _rotate(x, cos, sin_signed):
    return x * cos + pltpu.roll(x, DK_R // 2, 1) * sin_signed


def _head_norm_gate(o, gain, gate):
    return o * lax.rsqrt(jnp.mean(o * o, axis=-1, keepdims=True) + EPS) * gain * _silu(gate)


def _ret_pipeline(h_ref, wq_ref, wk_ref, wv_ref, wg_ref, cos_ref, sin_ref, dm_ref, qd_ref, kd_ref,
                  cd_ref, gain_ref, o_ref, s_ref, w_sc, z_sc, *, chunk, group, lag):
    n_chunks = h_ref.shape[0] // chunk
    assert n_chunks % group == 0 and lag >= 1
    g_rows = group * chunk
    q_cols = slice(0, DK_R)
    k_cols = slice(DK_R, 2 * DK_R)
    v_cols = slice(2 * DK_R, 2 * DK_R + DV_R)
    g_cols = slice(2 * DK_R + DV_R, 2 * DK_R + 2 * DV_R)
    for cols, w_ref in ((q_cols, wq_ref), (k_cols, wk_ref), (v_cols, wv_ref), (g_cols, wg_ref)):
        w_sc[:, cols] = w_ref[...].astype(BF16)
    dm = dm_ref[...]
    qd = qd_ref[...]
    kd = kd_ref[...]
    cd = cd_ref[...]
    gain = gain_ref[...]

    assert group == 2
    pieces = (slice(0, g_cols.start), g_cols)

    def project(c):
        rows = slice(c // group * g_rows, (c // group + 1) * g_rows)
        pcols = pieces[c % group]
        z_sc[rows, pcols] = jnp.dot(h_ref[rows, :], w_sc[:, pcols], preferred_element_type=F32)

    def scores(r0):
        rows = slice(r0, r0 + chunk)
        cos = cos_ref[rows, :]
        sin = sin_ref[rows, :]
        q = _rotate(z_sc[rows, q_cols], cos, sin)
        k = _rotate(z_sc[rows, k_cols], cos, sin) * (DK_R ** -0.5)
        a = (_dot_nt(q, k) * dm).astype(BF16)
        return a, (q * qd).astype(BF16), (k * kd).astype(BF16)

    def outputs(r0, parts, s):
        a, q_in, k_st = parts
        rows = slice(r0, r0 + chunk)
        v = z_sc[rows, v_cols].astype(BF16)
        o = jnp.dot(jnp.concatenate([a, q_in], axis=1), jnp.concatenate([v, s.astype(BF16)], axis=0),
                    preferred_element_type=F32)
        o_ref[rows, :] = _head_norm_gate(o, gain, z_sc[rows, g_cols])
        return cd * s + lax.dot_general(k_st, v, TN_DIMS, preferred_element_type=F32)

    carry = {"s": jnp.zeros((DK_R, DV_R), F32)}
    pending = {}

    def do_scores(c):
        pending[c] = scores(c * chunk)

    def do_outputs(c):
        carry["s"] = outputs(c * chunk, pending.pop(c), carry["s"])

    work = []
    for step in range(-group, n_chunks + lag):
        if step + group < n_chunks:
            work.append(functools.partial(project, step + group))
        if 0 <= step < n_chunks:
            work.append(functools.partial(do_scores, step))
        if 0 <= step - lag < n_chunks:
            work.append(functools.partial(do_outputs, step - lag))

    def finish():
        s_ref[...] = carry["s"]

    return work, finish


def _retention_tables(c):
    lg = np.log(1.0 - 2.0 ** (-5.0 - np.arange(H_R, dtype=np.float64)))
    idx = np.arange(c, dtype=np.float64)
    rel = idx[:, None] - idx[None, :]
    dm = np.where(rel >= 0, np.exp(lg[:, None, None] * np.maximum(rel, 0.0)), 0.0)
    qd = np.broadcast_to(np.exp(lg[:, None] * (idx + 1.0))[:, :, None], (H_R, c, DK_R))
    kd = np.broadcast_to(np.exp(lg[:, None] * (c - 1.0 - idx))[:, :, None], (H_R, c, DK_R))
    cd = np.broadcast_to(np.exp(lg * c)[:, None, None], (H_R, 1, DV_R))
    return tuple(jnp.asarray(a, F32) for a in (dm, qd, kd, cd))


def _lower_bound(lb_ref):
    a = lb_ref[...]
    a0, a1 = a[0:1, :], a[1:2, :]
    m = jnp.maximum(a0, a1)
    e0 = jnp.exp(a0 - m)
    e1 = jnp.exp(a1 - m)
    return e0 / (e0 + e1)


def _forget_and_key(g, lb):
    sg = jax.nn.sigmoid(g)
    return lb + (1.0 - lb) * sg, (1.0 - lb) * (1.0 - sg)


def _cumsum_rows(tri, x):
    hi = x.astype(BF16)
    rest = x - hi.astype(F32)
    mid = rest.astype(BF16)
    lo = (rest - mid.astype(F32)).astype(BF16)
    y = jnp.dot(tri, jnp.concatenate([hi, mid, lo], axis=1), preferred_element_type=F32)
    n = x.shape[1]
    return y[:, :n] + y[:, n:2 * n] + y[:, 2 * n:]


def _hgrn_pipeline(h_ref, wq_ref, wf_ref, wi_ref, wg_ref, lb_ref, gain_ref, o_ref, s_ref,
                   w_sc, z_sc, q_sc, b_sc, c_sc, *, chunk, sub, group, lag, heads):
    assert sub == 8 and chunk % (2 * sub) == 0
    g_rows = group * chunk
    n_chunks = h_ref.shape[0] // chunk
    assert n_chunks % group == 0 and lag >= 1
    lanes = lambda hh: slice(hh * DK_G, (hh + 1) * DK_G)
    cols = lambda hh, role: lanes(4 * hh + role)
    for hh in range(heads):
        for role, w_ref in enumerate((wq_ref, wf_ref, wi_ref, wg_ref)):
            w_sc[:, cols(hh, role)] = w_ref[:, lanes(hh)].astype(BF16)

    lb = _lower_bound(lb_ref)
    gain = gain_ref[...]
    r64 = lax.broadcasted_iota(jnp.int32, (chunk, chunk), 0)
    c64 = lax.broadcasted_iota(jnp.int32, (chunk, chunk), 1)
    tri = jnp.where(r64 >= c64, 1.0, 0.0).astype(BF16)
    same_sub = (r64 // sub) == (c64 // sub)
    place = [(c64[:sub] % sub == s) & (r64[:sub] >= s) for s in range(sub)]
    zrows = lambda n: jnp.zeros((n, DK_G), F32)

    piece_cols = z_sc.shape[1] // group
    assert piece_cols % DK_G == 0

    def project(c):
        rows = slice(c // group * g_rows, (c // group + 1) * g_rows)
        pcols = slice(c % group * piece_cols, (c % group + 1) * piece_cols)
        z_sc[rows, pcols] = jnp.dot(h_ref[rows, :], w_sc[:, pcols], preferred_element_type=F32)

    def gates(hh, r0):
        rows = slice(r0, r0 + chunk)
        f, k = _forget_and_key(z_sc[rows, cols(hh, 1)], lb[:, lanes(hh)])
        b_sc[rows, lanes(hh)] = _cumsum_rows(tri, jnp.log2(f))
        c_sc[rows, lanes(hh)] = b_sc[rows, lanes(hh)] - jnp.log2(k)
        q_sc[rows, lanes(hh)] = _silu(z_sc[rows, cols(hh, 0)])

    def diagonal(hh, r0):
        blocks = []
        for o in range(r0, r0 + chunk, sub):
            q, b = q_sc[o:o + sub, lanes(hh)], b_sc[o:o + sub, lanes(hh)]
            acc = jnp.zeros((sub, chunk), F32)
            for s in range(sub):
                p = jnp.exp2(b - c_sc[o + s:o + s + 1, lanes(hh)]) * q
                acc = jnp.where(place[s], jnp.sum(p, axis=-1, keepdims=True), acc)
            blocks.append(acc)
        return jnp.where(same_sub, jnp.concatenate(blocks, axis=0), 0.0)

    def below(hh, r0, w):
        qs, ks = [], []
        for mid in range(r0 + w, r0 + chunk, 2 * w):
            ref_b = b_sc[mid - 1:mid, lanes(hh)]
            qs += [zrows(w), q_sc[mid:mid + w, lanes(hh)] * jnp.exp2(b_sc[mid:mid + w, lanes(hh)] - ref_b)]
            ks += [jnp.exp2(ref_b - c_sc[mid - w:mid, lanes(hh)]), zrows(w)]
        scores = _dot_nt(jnp.concatenate(qs, axis=0), jnp.concatenate(ks, axis=0))
        return jnp.where(c64 // w == r64 // w - 1, scores, 0.0)

    def scores(hh, r0):
        a = diagonal(hh, r0)
        w = chunk // 2
        while w >= sub:
            a = a + below(hh, r0, w)
            w //= 2
        return a

    def outputs(hh, r0, a, st):
        rows = slice(r0, r0 + chunk)
        v = z_sc[rows, cols(hh, 2)].astype(BF16)
        b_last = b_sc[rows.stop - 1:rows.stop, lanes(hh)]
        o = _dot(a, v) + _dot_nt(q_sc[rows, lanes(hh)] * jnp.exp2(b_sc[rows, lanes(hh)]), st)
        o_ref[rows, lanes(hh)] = _head_norm_gate(o, gain[:, lanes(hh)], z_sc[rows, cols(hh, 3)])
        return st * jnp.exp2(b_last) + _dot_tn(v, jnp.exp2(b_last - c_sc[rows, lanes(hh)]))

    st = [jnp.zeros((DV_G, DK_G), F32) for _ in range(heads)]
    pending = {}

    def do_scores(hh, c):
        pending[hh, c] = scores(hh, c * chunk)

    def do_outputs(hh, c):
        st[hh] = outputs(hh, c * chunk, pending.pop((hh, c)), st[hh])

    work = []
    for step in range(-group, n_chunks + 2 * lag):
        if step + group < n_chunks:
            work.append(functools.partial(project, step + group))
        for hh in range(heads):
            if 0 <= step < n_chunks:
                work.append(functools.partial(gates, hh, step * chunk))
            if 0 <= step - lag < n_chunks:
                work.append(functools.partial(do_scores, hh, step - lag))
            if 0 <= step - 2 * lag < n_chunks:
                work.append(functools.partial(do_outputs, hh, step - 2 * lag))

    def finish():
        for hh in range(heads):
            s_ref[hh] = st[hh].T

    return work, finish


def _run_pipeline(build, **config):
    def kernel(*refs):
        work, finish = build(*refs, **config)
        for item in work:
            item()
        finish()
    return kernel


N_RET_IN = 11


def _ret_kernel(x_ref, g_ref, *refs, norm_rows, n_cast, **config):
    ret_in, cast_in = refs[:N_RET_IN], refs[N_RET_IN:N_RET_IN + n_cast]
    hs_ref, wz_ref = refs[N_RET_IN + n_cast:N_RET_IN + n_cast + 2]
    outs = refs[N_RET_IN + n_cast + 2:]
    (o_ref, s_ref, hn_ref), cast_out, zs_ref, scratch = outs[:3], outs[3:3 + n_cast], outs[3 + n_cast], outs[4 + n_cast:]

    @pl.when(pl.program_id(1) == 0)
    def _():
        def body(i, carry):
            rows = pl.ds(pl.multiple_of(i * norm_rows, norm_rows), norm_rows)
            hn_ref[rows, :] = _rms(x_ref[rows, :], g_ref[...]).astype(BF16)
            return carry

        lax.fori_loop(0, x_ref.shape[0] // norm_rows, body, 0)

    for src, dst in zip(cast_in, cast_out):
        dst[...] = src[...].astype(BF16)
    zs_ref[...] = jnp.dot(hs_ref[...], wz_ref[...].astype(BF16), preferred_element_type=F32)
    work, finish = _ret_pipeline(hn_ref, *ret_in, o_ref, s_ref, *scratch, **config)
    for item in work:
        item()
    finish()


def _retention(x3, g_mix, w_in, cos, sin, gain, casts, hs):
    b, t, d = x3.shape
    n_head = 2 * DK_R + 2 * DV_R
    steps = b * H_R
    d_in = w_in.shape[2]
    z_cols = DK_R * -(-d_in // (DK_R * steps))
    assert d_in % z_cols == 0
    z_block = lambda bi, h: jnp.minimum(bi * H_R + h, d_in // z_cols - 1)
    w_block = lambda width, lo: pl.BlockSpec((None, d, width), lambda bi, h: (0, 0, lo // width + h))
    head_tab = lambda rows, cols: pl.BlockSpec((None, rows, cols), lambda bi, h: (h, 0, 0))
    cast_in, cast_out, cast_shape = [], [], []
    for a, axis, col_block in casts:
        _, rows, cols = a.shape
        if axis == 1:
            n = rows // steps
            cols = cols if col_block is None else d
            cb = 0 if col_block is None else col_block
            cast_in.append(pl.BlockSpec((None, n, cols), lambda bi, h, cb=cb: (0, bi * H_R + h, cb)))
            cast_out.append(pl.BlockSpec((n, cols), lambda bi, h: (bi * H_R + h, 0)))
        else:
            n = cols // steps
            cast_in.append(pl.BlockSpec((None, rows, n), lambda bi, h: (0, 0, bi * H_R + h)))
            cast_out.append(pl.BlockSpec((rows, n), lambda bi, h: (0, bi * H_R + h)))
        cast_shape.append(jax.ShapeDtypeStruct((rows, cols), BF16))
    return pl.pallas_call(
        functools.partial(_ret_kernel, norm_rows=NORM_ROWS, n_cast=len(casts),
                          chunk=RET_CHUNK, group=RET_GROUP, lag=RET_LAG),
        grid=(b, H_R),
        in_specs=[
            pl.BlockSpec((None, t, d), lambda bi, h: (bi, 0, 0)),
            pl.BlockSpec((1, d), lambda bi, h: (0, 0)),
            w_block(DK_R, 0), w_block(DK_R, QK_R), w_block(DV_R, 2 * QK_R), w_block(DV_R, 2 * QK_R + V_R),
            pl.BlockSpec((t, DK_R), lambda bi, h: (0, 0)),
            pl.BlockSpec((t, DK_R), lambda bi, h: (0, 0)),
            head_tab(RET_CHUNK, RET_CHUNK), head_tab(RET_CHUNK, DK_R), head_tab(RET_CHUNK, DK_R),
            head_tab(1, DV_R),
            pl.BlockSpec((1, DV_R), lambda bi, h: (0, h)),
        ] + cast_in + [
            pl.BlockSpec(hs.shape, lambda bi, h: (0, 0)),
            pl.BlockSpec((None, d, z_cols), lambda bi, h: (0, 0, z_block(bi, h))),
        ],
        out_specs=[
            pl.BlockSpec((None, t, DV_R), lambda bi, h: (bi, 0, h)),
            pl.BlockSpec((None, None, DK_R, DV_R), lambda bi, h: (bi, h, 0, 0)),
            pl.BlockSpec((None, t, d), lambda bi, h: (bi, 0, 0)),
        ] + cast_out + [pl.BlockSpec((hs.shape[0], z_cols), lambda bi, h: (0, z_block(bi, h)))],
        out_shape=[
            jax.ShapeDtypeStruct((b, t, V_R), F32),
            jax.ShapeDtypeStruct((b, H_R, DK_R, DV_R), F32),
            jax.ShapeDtypeStruct((b, t, d), BF16),
        ] + cast_shape + [jax.ShapeDtypeStruct((hs.shape[0], d_in), F32)],
        scratch_shapes=[pltpu.VMEM((d, n_head), BF16), pltpu.VMEM((t, n_head), F32)],
        compiler_params=_params(("arbitrary", "arbitrary")),
        name="retention",
    )(x3, g_mix, w_in, w_in, w_in, w_in, cos, sin, *_retention_tables(RET_CHUNK), gain,
      *(a for a, _, _ in casts), hs, w_in)


def _hgrn(h3, w_in, hg_lb, gain):
    b, t, d = h3.shape
    heads = HG_HEADS
    wide = heads * DK_G
    first = (2 * QK_R + 2 * V_R) // wide
    w_block = lambda role: pl.BlockSpec((None, d, wide), lambda bi, h: (0, 0, first + role * (H_G // heads) + h))
    return pl.pallas_call(
        _run_pipeline(_hgrn_pipeline, chunk=HG_CHUNK, sub=HG_SUB, group=HG_GROUP, lag=HG_LAG, heads=heads),
        grid=(b, H_G // heads),
        in_specs=[
            pl.BlockSpec((None, t, d), lambda bi, h: (bi, 0, 0)),
            w_block(0), w_block(1), w_block(2), w_block(3),
            pl.BlockSpec((2, wide), lambda bi, h: (0, h)),
            pl.BlockSpec((1, wide), lambda bi, h: (0, h)),
        ],
        out_specs=[
            pl.BlockSpec((None, t, wide), lambda bi, h: (bi, 0, h)),
            pl.BlockSpec((None, heads, DK_G, DV_G), lambda bi, h: (bi, h, 0, 0)),
        ],
        out_shape=[
            jax.ShapeDtypeStruct((b, t, V_G), F32),
            jax.ShapeDtypeStruct((b, H_G, DK_G, DV_G), F32),
        ],
        scratch_shapes=[
            pltpu.VMEM((d, 4 * wide), BF16),
            pltpu.VMEM((t, 4 * wide), F32),
            pltpu.VMEM((t, wide), F32),
            pltpu.VMEM((t, wide), F32),
            pltpu.VMEM((t, wide), F32),
        ],
        compiler_params=_params(("parallel", "arbitrary")),
        name="hgrn",
    )(h3, w_in, w_in, w_in, w_in, hg_lb, gain)


def _step_kernel(z_ref, cos_ref, sin_ref, lb_ref, rgain_ref, ggain_ref, sr0_ref, sg0_ref,
                 or_ref, og_ref, sr_ref, sg_ref, *, bb):
    first = lambda a: jnp.where(lax.broadcasted_iota(jnp.int32, a.shape, 0) == 0, a, 0.0)
    cos = cos_ref[...]
    sin = sin_ref[...]
    lb_all = _lower_bound(lb_ref)
    row8 = lax.broadcasted_iota(jnp.int32, (8, DK_G), 0)
    ones3 = jnp.where(row8 < 3, 1.0, 0.0)
    g0 = 2 * QK_R + 2 * V_R
    base = (pl.program_id(0) % (8 // bb)) * bb
    work = []
    for i in range(bb):
        def z_row(lo, n, i=i):
            mine = lax.broadcasted_iota(jnp.int32, (8, n), 0) == base + i
            return jnp.sum(jnp.where(mine, z_ref[:, lo:lo + n], 0.0), axis=0, keepdims=True)

        z8 = lambda lo, n, z_row=z_row: jnp.broadcast_to(z_row(lo, n), (8, n))
        for h in range(H_R):
            q = _rotate(z8(h * DK_R, DK_R), cos, sin)
            k = _rotate(z8(QK_R + h * DK_R, DK_R), cos, sin) * (DK_R ** -0.5)
            v = z8(2 * QK_R + h * DV_R, DV_R)
            lanes = slice(h * DV_R, (h + 1) * DV_R)
            gate = z_row(2 * QK_R + V_R + h * DV_R, DV_R)
            work.append((sr0_ref, sr_ref, or_ref, rgain_ref, i, h, lanes, q, gate,
                         1.0 - 2.0 ** (-5.0 - h), _dot_tn(first(k), v)))
        for h in range(H_G):
            lanes = slice(h * DK_G, (h + 1) * DK_G)
            q = _silu(z8(g0 + h * DK_G, DK_G))
            f, k = _forget_and_key(z8(g0 + F_G + h * DK_G, DK_G), lb_all[:, lanes])
            v = z8(g0 + 2 * F_G + h * DV_G, DV_G)
            gate = z_row(g0 + 2 * F_G + V_G + h * DV_G, DV_G)
            f_hi = f.astype(BF16).astype(F32)
            f_mid = (f - f_hi).astype(BF16).astype(F32)
            f_lo = f - f_hi - f_mid
            lhs = jnp.where(row8 == 0, f_hi, jnp.where(row8 == 1, f_mid, jnp.where(row8 == 2, f_lo,
                            jnp.where(row8 == 3, k, 0.0))))
            rhs = jnp.concatenate([ones3, jnp.where(row8 == 3, v, 0.0)], axis=1)
            both = _dot_tn(lhs, rhs)
            work.append((sg0_ref, sg_ref, og_ref, ggain_ref, i, h, lanes, q, gate,
                         both[:, :DV_G], both[:, DV_G:]))
    outs = []
    for s0_ref, s_ref, _, _, i, h, _, q, _, decay, kv in work:
        s_new = decay * s0_ref[i, h] + kv
        s_ref[i, h] = s_new
        outs.append(_dot(q, s_new)[0:1, :])
    for (_, _, o_ref, gain_ref, i, _, lanes, _, gate, _, _), o in zip(work, outs):
        o_ref[i, :, lanes] = _head_norm_gate(o, gain_ref[:, lanes], gate)


def _step_specs(z, cos, sin, hg_lb, rgain, ggain, sr0, sg0, bb):
    n, d_in = z.shape
    assert 8 % bb == 0
    row = lambda w: pl.BlockSpec((bb, 1, w), lambda i: (i, 0, 0))
    const = lambda a: pl.BlockSpec(a.shape, lambda i: (0,) * a.ndim)
    sr_spec = pl.BlockSpec((bb, H_R, DK_R, DV_R), lambda i: (i, 0, 0, 0))
    sg_spec = pl.BlockSpec((bb, H_G, DK_G, DV_G), lambda i: (i, 0, 0, 0))
    operands = (z, cos, sin, hg_lb, rgain, ggain, sr0, sg0)
    in_specs = [pl.BlockSpec((8, d_in), lambda i: (i * bb // 8, 0)),
                const(cos), const(sin), const(hg_lb), const(rgain), const(ggain), sr_spec, sg_spec]
    out_specs = [row(V_R), row(V_G), sr_spec, sg_spec]
    out_shape = [
        jax.ShapeDtypeStruct((n, 1, V_R), F32),
        jax.ShapeDtypeStruct((n, 1, V_G), F32),
        jax.ShapeDtypeStruct(sr0.shape, F32),
        jax.ShapeDtypeStruct(sg0.shape, F32),
    ]
    return operands, in_specs, out_specs, out_shape


N_POST_IN = 17
N_STEP_IN = 8


def _post_kernel(*refs, bb):
    if bb:
        _post_body(*refs[:N_POST_IN], refs[N_POST_IN + N_STEP_IN])
        _step_kernel(*refs[N_POST_IN:N_POST_IN + N_STEP_IN], *refs[N_POST_IN + N_STEP_IN + 1:], bb=bb)
    else:
        _post_body(*refs)


def _post_body(x_ref, h_ref, gr_ref, gg_ref, p_ref, war_ref, wag_ref, wur_ref, wug_ref, wo_ref,
               nm_ref, w1_ref, w2_ref, np_ref, wg_ref, wp_ref, nf_ref, y_ref):
    h = h_ref[...]
    u_r = jnp.dot(gr_ref[...].astype(BF16), wur_ref[...], preferred_element_type=F32)
    u_g = jnp.dot(gg_ref[...].astype(BF16), wug_ref[...], preferred_element_type=F32)
    m = (_sigmoid(jnp.dot(h, war_ref[...], preferred_element_type=F32)) * u_r
         + _sigmoid(jnp.dot(h, wag_ref[...], preferred_element_type=F32)) * u_g)
    r = x_ref[...] + jnp.dot(m.astype(BF16), wo_ref[...], preferred_element_type=F32)
    hm = _rms(r, nm_ref[...]).astype(BF16)
    a = jnp.maximum(jnp.dot(hm, w1_ref[...], preferred_element_type=F32), 0.0)
    r = r + jnp.dot((a * a).astype(BF16), w2_ref[...], preferred_element_type=F32)
    hp = _rms(r, np_ref[...]).astype(BF16)
    gate = _sigmoid(jnp.dot(hp, wg_ref[...], preferred_element_type=F32))
    r = r + gate * jnp.dot(p_ref[...].astype(BF16), wp_ref[...], preferred_element_type=F32)
    y_ref[...] = _rms(r, nf_ref[...])


def _post(x, h, g_r, g_g, p, war, wag, wur, wug, wo, nm, w1, w2, npl, wg, wp, nf, tm, step=None):
    n, d = x.shape
    grid = n // tm
    rows = lambda w: pl.BlockSpec((tm, w), lambda i: (i, 0))
    const = lambda a: pl.BlockSpec(a.shape, lambda i: (0, 0), pipeline_mode=pl.Buffered(1))
    operands = (x, h, g_r, g_g, p, war, wag, wur, wug, wo, nm, w1, w2, npl, wg, wp, nf)
    in_specs = [rows(d), rows(d), rows(d), rows(d), rows(p.shape[1]),
                const(war), const(wag), const(wur), const(wug), const(wo), const(nm), const(w1), const(w2),
                const(npl), const(wg), const(wp), const(nf)]
    assert len(operands) == N_POST_IN
    out_specs = [rows(d)]
    out_shape = [jax.ShapeDtypeStruct((n, d), F32)]
    bb = 0
    if step is not None:
        bb = step[0].shape[0] // grid
        assert bb * grid == step[0].shape[0] and len(step) == N_STEP_IN
        s_operands, s_in, s_out, s_shape = _step_specs(*step, bb=bb)
        operands += s_operands
        in_specs += s_in
        out_specs += s_out
        out_shape += s_shape
    out = pl.pallas_call(
        functools.partial(_post_kernel, bb=bb),
        grid=(grid,),
        in_specs=in_specs,
        out_specs=out_specs,
        out_shape=out_shape,
        compiler_params=_params(("parallel",)),
        name="post",
    )(*operands)
    return out if bb else out[0]


def _rope_tables(pos):
    half = DK_R // 2
    inv = ROPE_BASE ** (-np.arange(half, dtype=np.float64) / half)
    ang = np.asarray(pos, np.float64)[:, None] * inv[None, :]
    cos, sin = np.cos(ang), np.sin(ang)
    return (jnp.asarray(np.concatenate([cos, cos], axis=-1), F32),
            jnp.asarray(np.concatenate([-sin, sin], axis=-1), F32))


def kernel(x_prompt, x_sample, state_ret, state_hgrn, p_prompt, p_sample, norm_mix_g, w_in,
           ret_norm_g, hg_norm_g, hg_lb, w_up_ret, w_up_hg, w_o, norm_mlp_g, w_ff1, w_ff2,
           norm_ple_g, w_ple_gate, w_ple_proj, norm_final_g):
    b, t, d = x_prompt.shape
    nb, td, _ = x_sample.shape
    assert w_in.shape[0] == 1 and td == 1 and hg_lb.shape[0] == 2
    row = lambda g: g.reshape(1, -1)
    merge_block = (2 * QK_R + 2 * V_R + 2 * F_G + 2 * V_G) // d
    g_mix = row(norm_mix_g[0])
    rgain = row(ret_norm_g[0])
    ggain = row(hg_norm_g[0])

    xs = x_sample.reshape(nb, d)
    hs = _norm(xs, g_mix, tm=nb)
    cos_s, sin_s = _rope_tables(PAST_LEN + np.arange(td))

    xp = x_prompt.reshape(b * t, d)
    cos_p, sin_p = _rope_tables(np.arange(t))
    casts = [(w_in, 1, merge_block), (w_in, 1, merge_block + 1), (w_up_ret, 1, None), (w_up_hg, 1, None),
             (w_o, 1, None), (w_ff1, 2, None), (w_ff2, 1, None), (w_ple_gate, 1, None)]
    gr_p, ret_p, hp3, war, wag, wur, wug, wo, w1, w2, wg, zs = _retention(
        x_prompt, g_mix, w_in, cos_p, sin_p, rgain, casts, hs)
    step = (zs, cos_s, sin_s, hg_lb, rgain, ggain, state_ret[0], state_hgrn[0])
    post_w = (war, wag, wur, wug, wo, row(norm_mlp_g[0]), w1, w2, row(norm_ple_g[0]), wg,
              w_ple_proj[0].astype(BF16), row(norm_final_g))
    gg_p, hg_p = _hgrn(hp3, w_in, hg_lb, ggain)
    y_p, gr_s, gg_s, ret_s, hg_s = _post(
        xp, hp3.reshape(b * t, d), gr_p.reshape(b * t, -1), gg_p.reshape(b * t, -1),
        p_prompt[0].reshape(b * t, -1), *post_w, tm=256, step=step)

    y_s = _post(xs, hs, gr_s.reshape(nb, -1), gg_s.reshape(nb, -1),
                p_sample[0].reshape(nb, -1), *post_w, tm=nb)

    return (y_p.reshape(b, t, d), y_s.reshape(nb, td, d), ret_p[None], hg_p[None],
            ret_s[None], hg_s[None])
```

```python
import functools

import jax
import jax.numpy as jnp
import numpy as np
from jax import lax
from jax.experimental import pallas as pl
from jax.experimental.pallas import tpu as pltpu

F32 = jnp.float32
BF16 = jnp.bfloat16

PAST_LEN = 16384
ROPE_BASE = 10000.0
EPS = 1e-6

H_R, DK_R, DV_R = 4, 128, 256
H_G, DK_G, DV_G = 8, 128, 128
QK_R = H_R * DK_R
V_R = H_R * DV_R
F_G = H_G * DK_G
V_G = H_G * DV_G

V7X_VMEM_BYTES = 64 * 1024 * 1024
VMEM_LIMIT = V7X_VMEM_BYTES - 8 * 1024 * 1024

NORM_ROWS = 256
RET_CHUNK = 128
RET_GROUP = 2
RET_LAG = 1
HG_CHUNK = 64
HG_SUB = 8
HG_GROUP = 4
HG_LAG = 2
HG_HEADS = 2

NT_DIMS = (((1,), (1,)), ((), ()))
TN_DIMS = (((0,), (0,)), ((), ()))


def _dot(a, b):
    return jnp.dot(a.astype(BF16), b.astype(BF16), preferred_element_type=F32)


def _dot_nt(a, b):
    return lax.dot_general(a.astype(BF16), b.astype(BF16), NT_DIMS, preferred_element_type=F32)


def _dot_tn(a, b):
    return lax.dot_general(a.astype(BF16), b.astype(BF16), TN_DIMS, preferred_element_type=F32)


def _rms(x, g):
    return x * lax.rsqrt(jnp.mean(x * x, axis=-1, keepdims=True) + EPS) * g


def _sigmoid(x):
    return 0.5 * jnp.tanh(0.5 * x) + 0.5


def _silu(x):
    half = 0.5 * x
    return half + half * jnp.tanh(half)


def _params(sem):
    return pltpu.CompilerParams(dimension_semantics=sem, vmem_limit_bytes=VMEM_LIMIT)


def _norm_kernel(x_ref, g_ref, h_ref):
    h_ref[...] = _rms(x_ref[...], g_ref[...]).astype(BF16)


def _norm(x, g, tm):
    n, d = x.shape
    return pl.pallas_call(
        _norm_kernel,
        grid=(n // tm,),
        in_specs=[pl.BlockSpec((tm, d), lambda i: (i, 0)), pl.BlockSpec((1, d), lambda i: (0, 0))],
        out_specs=pl.BlockSpec((tm, d), lambda i: (i, 0)),
        out_shape=jax.ShapeDtypeStruct((n, d), BF16),
        compiler_params=_params(("parallel",)),
        name="norm",
    )(x, g)


def _rotate(x, cos, sin_signed):
    return x * cos + pltpu.roll(x, DK_R // 2, 1) * sin_signed


def _head_norm_gate(o, gain, gate):
    return o * lax.rsqrt(jnp.mean(o * o, axis=-1, keepdims=True) + EPS) * gain * _silu(gate)


def _ret_pipeline(h_ref, wq_ref, wk_ref, wv_ref, wg_ref, cos_ref, sin_ref, dm_ref, qd_ref, kd_ref,
                  cd_ref, gain_ref, o_ref, s_ref, w_sc, z_sc, *, chunk, group, lag):
    n_chunks = h_ref.shape[0] // chunk
    assert n_chunks % group == 0 and lag >= 1
    g_rows = group * chunk
    q_cols = slice(0, DK_R)
    k_cols = slice(DK_R, 2 * DK_R)
    v_cols = slice(2 * DK_R, 2 * DK_R + DV_R)
    g_cols = slice(2 * DK_R + DV_R, 2 * DK_R + 2 * DV_R)
    for cols, w_ref in ((q_cols, wq_ref), (k_cols, wk_ref), (v_cols, wv_ref), (g_cols, wg_ref)):
        w_sc[:, cols] = w_ref[...].astype(BF16)
    dm = dm_ref[...]
    qd = qd_ref[...]
    kd = kd_ref[...]
    cd = cd_ref[...]
    gain = gain_ref[...]

    assert group == 2
    pieces = (slice(0, g_cols.start), g_cols)

    def project(c):
        rows = slice(c // group * g_rows, (c // group + 1) * g_rows)
        pcols = pieces[c % group]
        z_sc[rows, pcols] = jnp.dot(h_ref[rows, :], w_sc[:, pcols], preferred_element_type=F32)

    def scores(r0):
        rows = slice(r0, r0 + chunk)
        cos = cos_ref[rows, :]
        sin = sin_ref[rows, :]
        q = _rotate(z_sc[rows, q_cols], cos, sin)
        k = _rotate(z_sc[rows, k_cols], cos, sin) * (DK_R ** -0.5)
        a = (_dot_nt(q, k) * dm).astype(BF16)
        return a, (q * qd).astype(BF16), (k * kd).astype(BF16)

    def outputs(r0, parts, s):
        a, q_in, k_st = parts
        rows = slice(r0, r0 + chunk)
        v = z_sc[rows, v_cols].astype(BF16)
        o = jnp.dot(jnp.concatenate([a, q_in], axis=1), jnp.concatenate([v, s.astype(BF16)], axis=0),
                    preferred_element_type=F32)
        o_ref[rows, :] = _head_norm_gate(o, gain, z_sc[rows, g_cols])
        return cd * s + lax.dot_general(k_st, v, TN_DIMS, preferred_element_type=F32)

    carry = {"s": jnp.zeros((DK_R, DV_R), F32)}
    pending = {}

    def do_scores(c):
        pending[c] = scores(c * chunk)

    def do_outputs(c):
        carry["s"] = outputs(c * chunk, pending.pop(c), carry["s"])

    work = []
    for step in range(-group, n_chunks + lag):
        if step + group < n_chunks:
            work.append(functools.partial(project, step + group))
        if 0 <= step < n_chunks:
            work.append(functools.partial(do_scores, step))
        if 0 <= step - lag < n_chunks:
            work.append(functools.partial(do_outputs, step - lag))

    def finish():
        s_ref[...] = carry["s"]

    return work, finish


def _retention_tables(c):
    lg = np.log(1.0 - 2.0 ** (-5.0 - np.arange(H_R, dtype=np.float64)))
    idx = np.arange(c, dtype=np.float64)
    rel = idx[:, None] - idx[None, :]
    dm = np.where(rel >= 0, np.exp(lg[:, None, None] * np.maximum(rel, 0.0)), 0.0)
    qd = np.broadcast_to(np.exp(lg[:, None] * (idx + 1.0))[:, :, None], (H_R, c, DK_R))
    kd = np.broadcast_to(np.exp(lg[:, None] * (c - 1.0 - idx))[:, :, None], (H_R, c, DK_R))
    cd = np.broadcast_to(np.exp(lg * c)[:, None, None], (H_R, 1, DV_R))
    return tuple(jnp.asarray(a, F32) for a in (dm, qd, kd, cd))


def _lower_bound(lb_ref):
    a = lb_ref[...]
    a0, a1 = a[0:1, :], a[1:2, :]
    m = jnp.maximum(a0, a1)
    e0 = jnp.exp(a0 - m)
    e1 = jnp.exp(a1 - m)
    return e0 / (e0 + e1)


def _forget_and_key(g, lb):
    half_scale = 0.5 * (1.0 - lb)
    th_scaled = half_scale * jnp.tanh(0.5 * g)
    return lb + jnp.maximum(half_scale + th_scaled, 0.0), jnp.maximum(half_scale - th_scaled, 0.0)


def _cumsum_rows(tri, x):
    hi = x.astype(BF16)
    rest = x - hi.astype(F32)
    mid = rest.astype(BF16)
    lo = (rest - mid.astype(F32)).astype(BF16)
    y = jnp.dot(tri, jnp.concatenate([hi, mid, lo], axis=1), preferred_element_type=F32)
    n = x.shape[1]
    return y[:, :n] + y[:, n:2 * n] + y[:, 2 * n:]


def _hgrn_pipeline(h_ref, wq_ref, wf_ref, wi_ref, wg_ref, lb_ref, gain_ref, o_ref, s_ref,
                   w_sc, z_sc, q_sc, b_sc, c_sc, *, chunk, sub, group, lag, heads):
    assert sub == 8 and chunk % (2 * sub) == 0
    g_rows = group * chunk
    n_chunks = h_ref.shape[0] // chunk
    assert n_chunks % group == 0 and lag >= 1
    lanes = lambda hh: slice(hh * DK_G, (hh + 1) * DK_G)
    cols = lambda hh, role: lanes(4 * hh + role)
    for hh in range(heads):
        for role, w_ref in enumerate((wq_ref, wf_ref, wi_ref, wg_ref)):
            w_sc[:, cols(hh, role)] = w_ref[:, lanes(hh)].astype(BF16)

    lb = _lower_bound(lb_ref)
    gain = gain_ref[...]
    r64 = lax.broadcasted_iota(jnp.int32, (chunk, chunk), 0)
    c64 = lax.broadcasted_iota(jnp.int32, (chunk, chunk), 1)
    tri = jnp.where(r64 >= c64, 1.0, 0.0).astype(BF16)
    same_sub = (r64 // sub) == (c64 // sub)
    place = [(c64[:sub] % sub == s) & (r64[:sub] >= s) for s in range(sub)]
    zrows = lambda n: jnp.zeros((n, DK_G), F32)

    piece_cols = z_sc.shape[1] // group
    assert piece_cols % DK_G == 0

    def project(c):
        rows = slice(c // group * g_rows, (c // group + 1) * g_rows)
        pcols = slice(c % group * piece_cols, (c % group + 1) * piece_cols)
        z_sc[rows, pcols] = jnp.dot(h_ref[rows, :], w_sc[:, pcols], preferred_element_type=F32)

    def gates(hh, r0):
        rows = slice(r0, r0 + chunk)
        f, k = _forget_and_key(z_sc[rows, cols(hh, 1)], lb[:, lanes(hh)])
        b_sc[rows, lanes(hh)] = _cumsum_rows(tri, jnp.log2(f))
        c_sc[rows, lanes(hh)] = b_sc[rows, lanes(hh)] - jnp.log2(k)
        q_sc[rows, lanes(hh)] = _silu(z_sc[rows, cols(hh, 0)])

    def diagonal(hh, r0):
        blocks = []
        for o in range(r0, r0 + chunk, sub):
            q, b = q_sc[o:o + sub, lanes(hh)], b_sc[o:o + sub, lanes(hh)]
            acc = jnp.zeros((sub, chunk), F32)
            for s in range(sub):
                p = jnp.exp2(b - c_sc[o + s:o + s + 1, lanes(hh)]) * q
                acc = jnp.where(place[s], jnp.sum(p, axis=-1, keepdims=True), acc)
            blocks.append(acc)
        return jnp.where(same_sub, jnp.concatenate(blocks, axis=0), 0.0)

    def below(hh, r0, w):
        qs, ks = [], []
        for mid in range(r0 + w, r0 + chunk, 2 * w):
            ref_b = b_sc[mid - 1:mid, lanes(hh)]
            qs += [zrows(w), q_sc[mid:mid + w, lanes(hh)] * jnp.exp2(b_sc[mid:mid + w, lanes(hh)] - ref_b)]
            ks += [jnp.exp2(ref_b - c_sc[mid - w:mid, lanes(hh)]), zrows(w)]
        scores = _dot_nt(jnp.concatenate(qs, axis=0), jnp.concatenate(ks, axis=0))
        return jnp.where(c64 // w == r64 // w - 1, scores, 0.0)

    def scores(hh, r0):
        a = diagonal(hh, r0)
        w = chunk // 2
        while w >= sub:
            a = a + below(hh, r0, w)
            w //= 2
        return a

    def outputs(hh, r0, a, st):
        rows = slice(r0, r0 + chunk)
        v = z_sc[rows, cols(hh, 2)].astype(BF16)
        b_last = b_sc[rows.stop - 1:rows.stop, lanes(hh)]
        o = _dot(a, v) + _dot_nt(q_sc[rows, lanes(hh)] * jnp.exp2(b_sc[rows, lanes(hh)]), st)
        o_ref[rows, lanes(hh)] = _head_norm_gate(o, gain[:, lanes(hh)], z_sc[rows, cols(hh, 3)])
        return st * jnp.exp2(b_last) + _dot_tn(v, jnp.exp2(b_last - c_sc[rows, lanes(hh)]))

    st = [jnp.zeros((DV_G, DK_G), F32) for _ in range(heads)]
    pending = {}

    def do_scores(hh, c):
        pending[hh, c] = scores(hh, c * chunk)

    def do_outputs(hh, c):
        st[hh] = outputs(hh, c * chunk, pending.pop((hh, c)), st[hh])

    work = []
    for step in range(-group, n_chunks + 2 * lag):
        if step + group < n_chunks:
            work.append(functools.partial(project, step + group))
        for hh in range(heads):
            if 0 <= step < n_chunks:
                work.append(functools.partial(gates, hh, step * chunk))
            if 0 <= step - lag < n_chunks:
                work.append(functools.partial(do_scores, hh, step - lag))
            if 0 <= step - 2 * lag < n_chunks:
                work.append(functools.partial(do_outputs, hh, step - 2 * lag))

    def finish():
        for hh in range(heads):
            s_ref[hh] = st[hh].T

    return work, finish


def _run_pipeline(build, **config):
    def kernel(*refs):
        work, finish = build(*refs, **config)
        for item in work:
            item()
        finish()
    return kernel


N_RET_IN = 11


def _ret_kernel(x_ref, g_ref, *refs, norm_rows, n_cast, **config):
    ret_in, cast_in = refs[:N_RET_IN], refs[N_RET_IN:N_RET_IN + n_cast]
    hs_ref, wz_ref = refs[N_RET_IN + n_cast:N_RET_IN + n_cast + 2]
    outs = refs[N_RET_IN + n_cast + 2:]
    (o_ref, s_ref, hn_ref), cast_out, zs_ref, scratch = outs[:3], outs[3:3 + n_cast], outs[3 + n_cast], outs[4 + n_cast:]

    @pl.when(pl.program_id(1) == 0)
    def _():
        def body(i, carry):
            rows = pl.ds(pl.multiple_of(i * norm_rows, norm_rows), norm_rows)
            hn_ref[rows, :] = _rms(x_ref[rows, :], g_ref[...]).astype(BF16)
            return carry

        lax.fori_loop(0, x_ref.shape[0] // norm_rows, body, 0)

    for src, dst in zip(cast_in, cast_out):
        dst[...] = src[...].astype(BF16)
    zs_ref[...] = jnp.dot(hs_ref[...], wz_ref[...].astype(BF16), preferred_element_type=F32)
    work, finish = _ret_pipeline(hn_ref, *ret_in, o_ref, s_ref, *scratch, **config)
    for item in work:
        item()
    finish()


def _retention(x3, g_mix, w_in, cos, sin, gain, casts, hs):
    b, t, d = x3.shape
    n_head = 2 * DK_R + 2 * DV_R
    steps = b * H_R
    d_in = w_in.shape[2]
    z_cols = DK_R * -(-d_in // (DK_R * steps))
    assert d_in % z_cols == 0
    z_block = lambda bi, h: jnp.minimum(bi * H_R + h, d_in // z_cols - 1)
    w_block = lambda width, lo: pl.BlockSpec((None, d, width), lambda bi, h: (0, 0, lo // width + h))
    head_tab = lambda rows, cols: pl.BlockSpec((None, rows, cols), lambda bi, h: (h, 0, 0))
    cast_in, cast_out, cast_shape = [], [], []
    for a, axis, col_block in casts:
        _, rows, cols = a.shape
        if axis == 1:
            n = rows // steps
            cols = cols if col_block is None else d
            cb = 0 if col_block is None else col_block
            cast_in.append(pl.BlockSpec((None, n, cols), lambda bi, h, cb=cb: (0, bi * H_R + h, cb)))
            cast_out.append(pl.BlockSpec((n, cols), lambda bi, h: (bi * H_R + h, 0)))
        else:
            n = cols // steps
            cast_in.append(pl.BlockSpec((None, rows, n), lambda bi, h: (0, 0, bi * H_R + h)))
            cast_out.append(pl.BlockSpec((rows, n), lambda bi, h: (0, bi * H_R + h)))
        cast_shape.append(jax.ShapeDtypeStruct((rows, cols), BF16))
    return pl.pallas_call(
        functools.partial(_ret_kernel, norm_rows=NORM_ROWS, n_cast=len(casts),
                          chunk=RET_CHUNK, group=RET_GROUP, lag=RET_LAG),
        grid=(b, H_R),
        in_specs=[
            pl.BlockSpec((None, t, d), lambda bi, h: (bi, 0, 0)),
            pl.BlockSpec((1, d), lambda bi, h: (0, 0)),
            w_block(DK_R, 0), w_block(DK_R, QK_R), w_block(DV_R, 2 * QK_R), w_block(DV_R, 2 * QK_R + V_R),
            pl.BlockSpec((t, DK_R), lambda bi, h: (0, 0)),
            pl.BlockSpec((t, DK_R), lambda bi, h: (0, 0)),
            head_tab(RET_CHUNK, RET_CHUNK), head_tab(RET_CHUNK, DK_R), head_tab(RET_CHUNK, DK_R),
            head_tab(1, DV_R),
            pl.BlockSpec((1, DV_R), lambda bi, h: (0, h)),
        ] + cast_in + [
            pl.BlockSpec(hs.shape, lambda bi, h: (0, 0)),
            pl.BlockSpec((None, d, z_cols), lambda bi, h: (0, 0, z_block(bi, h))),
        ],
        out_specs=[
            pl.BlockSpec((None, t, DV_R), lambda bi, h: (bi, 0, h)),
            pl.BlockSpec((None, None, DK_R, DV_R), lambda bi, h: (bi, h, 0, 0)),
            pl.BlockSpec((None, t, d), lambda bi, h: (bi, 0, 0)),
        ] + cast_out + [pl.BlockSpec((hs.shape[0], z_cols), lambda bi, h: (0, z_block(bi, h)))],
        out_shape=[
            jax.ShapeDtypeStruct((b, t, V_R), F32),
            jax.ShapeDtypeStruct((b, H_R, DK_R, DV_R), F32),
            jax.ShapeDtypeStruct((b, t, d), BF16),
        ] + cast_shape + [jax.ShapeDtypeStruct((hs.shape[0], d_in), F32)],
        scratch_shapes=[pltpu.VMEM((d, n_head), BF16), pltpu.VMEM((t, n_head), F32)],
        compiler_params=_params(("arbitrary", "arbitrary")),
        name="retention",
    )(x3, g_mix, w_in, w_in, w_in, w_in, cos, sin, *_retention_tables(RET_CHUNK), gain,
      *(a for a, _, _ in casts), hs, w_in)


def _hgrn(h3, w_in, hg_lb, gain):
    b, t, d = h3.shape
    heads = HG_HEADS
    wide = heads * DK_G
    first = (2 * QK_R + 2 * V_R) // wide
    w_block = lambda role: pl.BlockSpec((None, d, wide), lambda bi, h: (0, 0, first + role * (H_G // heads) + h))
    return pl.pallas_call(
        _run_pipeline(_hgrn_pipeline, chunk=HG_CHUNK, sub=HG_SUB, group=HG_GROUP, lag=HG_LAG, heads=heads),
        grid=(b, H_G // heads),
        in_specs=[
            pl.BlockSpec((None, t, d), lambda bi, h: (bi, 0, 0)),
            w_block(0), w_block(1), w_block(2), w_block(3),
            pl.BlockSpec((2, wide), lambda bi, h: (0, h)),
            pl.BlockSpec((1, wide), lambda bi, h: (0, h)),
        ],
        out_specs=[
            pl.BlockSpec((None, t, wide), lambda bi, h: (bi, 0, h)),
            pl.BlockSpec((None, heads, DK_G, DV_G), lambda bi, h: (bi, h, 0, 0)),
        ],
        out_shape=[
            jax.ShapeDtypeStruct((b, t, V_G), F32),
            jax.ShapeDtypeStruct((b, H_G, DK_G, DV_G), F32),
        ],
        scratch_shapes=[
            pltpu.VMEM((d, 4 * wide), BF16),
            pltpu.VMEM((t, 4 * wide), F32),
            pltpu.VMEM((t, wide), F32),
            pltpu.VMEM((t, wide), F32),
            pltpu.VMEM((t, wide), F32),
        ],
        compiler_params=_params(("parallel", "arbitrary")),
        name="hgrn",
    )(h3, w_in, w_in, w_in, w_in, hg_lb, gain)


def _step_kernel(z_ref, cos_ref, sin_ref, lb_ref, rgain_ref, ggain_ref, sr0_ref, sg0_ref,
                 or_ref, og_ref, sr_ref, sg_ref, *, bb):
    first = lambda a: jnp.where(lax.broadcasted_iota(jnp.int32, a.shape, 0) == 0, a, 0.0)
    cos = cos_ref[...]
    sin = sin_ref[...]
    lb_all = _lower_bound(lb_ref)
    row8 = lax.broadcasted_iota(jnp.int32, (8, DK_G), 0)
    ones3 = jnp.where(row8 < 3, 1.0, 0.0)
    g0 = 2 * QK_R + 2 * V_R
    base = (pl.program_id(0) % (8 // bb)) * bb
    work = []
    for i in range(bb):
        def z_row(lo, n, i=i):
            mine = lax.broadcasted_iota(jnp.int32, (8, n), 0) == base + i
            return jnp.sum(jnp.where(mine, z_ref[:, lo:lo + n], 0.0), axis=0, keepdims=True)

        z8 = lambda lo, n, z_row=z_row: jnp.broadcast_to(z_row(lo, n), (8, n))
        for h in range(H_R):
            q = _rotate(z8(h * DK_R, DK_R), cos, sin)
            k = _rotate(z8(QK_R + h * DK_R, DK_R), cos, sin) * (DK_R ** -0.5)
            v = z8(2 * QK_R + h * DV_R, DV_R)
            lanes = slice(h * DV_R, (h + 1) * DV_R)
            gate = z_row(2 * QK_R + V_R + h * DV_R, DV_R)
            work.append((sr0_ref, sr_ref, or_ref, rgain_ref, i, h, lanes, q, gate,
                         1.0 - 2.0 ** (-5.0 - h), _dot_tn(first(k), v)))
        for h in range(H_G):
            lanes = slice(h * DK_G, (h + 1) * DK_G)
            q = _silu(z8(g0 + h * DK_G, DK_G))
            f, k = _forget_and_key(z8(g0 + F_G + h * DK_G, DK_G), lb_all[:, lanes])
            v = z8(g0 + 2 * F_G + h * DV_G, DV_G)
            gate = z_row(g0 + 2 * F_G + V_G + h * DV_G, DV_G)
            f_hi = f.astype(BF16).astype(F32)
            f_mid = (f - f_hi).astype(BF16).astype(F32)
            f_lo = f - f_hi - f_mid
            lhs = jnp.where(row8 == 0, f_hi, jnp.where(row8 == 1, f_mid, jnp.where(row8 == 2, f_lo,
                            jnp.where(row8 == 3, k, 0.0))))
            rhs = jnp.concatenate([ones3, jnp.where(row8 == 3, v, 0.0)], axis=1)
            both = _dot_tn(lhs, rhs)
            work.append((sg0_ref, sg_ref, og_ref, ggain_ref, i, h, lanes, q, gate,
                         both[:, :DV_G], both[:, DV_G:]))
    outs = []
    for s0_ref, s_ref, _, _, i, h, _, q, _, decay, kv in work:
        s_new = decay * s0_ref[i, h] + kv
        s_ref[i, h] = s_new
        outs.append(_dot(q, s_new)[0:1, :])
    for (_, _, o_ref, gain_ref, i, _, lanes, _, gate, _, _), o in zip(work, outs):
        o_ref[i, :, lanes] = _head_norm_gate(o, gain_ref[:, lanes], gate)


def _step_specs(z, cos, sin, hg_lb, rgain, ggain, sr0, sg0, bb):
    n, d_in = z.shape
    assert 8 % bb == 0
    row = lambda w: pl.BlockSpec((bb, 1, w), lambda i: (i, 0, 0))
    const = lambda a: pl.BlockSpec(a.shape, lambda i: (0,) * a.ndim)
    sr_spec = pl.BlockSpec((bb, H_R, DK_R, DV_R), lambda i: (i, 0, 0, 0))
    sg_spec = pl.BlockSpec((bb, H_G, DK_G, DV_G), lambda i: (i, 0, 0, 0))
    operands = (z, cos, sin, hg_lb, rgain, ggain, sr0, sg0)
    in_specs = [pl.BlockSpec((8, d_in), lambda i: (i * bb // 8, 0)),
                const(cos), const(sin), const(hg_lb), const(rgain), const(ggain), sr_spec, sg_spec]
    out_specs = [row(V_R), row(V_G), sr_spec, sg_spec]
    out_shape = [
        jax.ShapeDtypeStruct((n, 1, V_R), F32),
        jax.ShapeDtypeStruct((n, 1, V_G), F32),
        jax.ShapeDtypeStruct(sr0.shape, F32),
        jax.ShapeDtypeStruct(sg0.shape, F32),
    ]
    return operands, in_specs, out_specs, out_shape


N_POST_IN = 17
N_STEP_IN = 8


def _post_kernel(*refs, bb):
    if bb:
        _post_body(*refs[:N_POST_IN], refs[N_POST_IN + N_STEP_IN])
        _step_kernel(*refs[N_POST_IN:N_POST_IN + N_STEP_IN], *refs[N_POST_IN + N_STEP_IN + 1:], bb=bb)
    else:
        _post_body(*refs)


def _post_body(x_ref, h_ref, gr_ref, gg_ref, p_ref, war_ref, wag_ref, wur_ref, wug_ref, wo_ref,
               nm_ref, w1_ref, w2_ref, np_ref, wg_ref, wp_ref, nf_ref, y_ref):
    h = h_ref[...]
    u_r = jnp.dot(gr_ref[...].astype(BF16), wur_ref[...], preferred_element_type=F32)
    u_g = jnp.dot(gg_ref[...].astype(BF16), wug_ref[...], preferred_element_type=F32)
    m = (_sigmoid(jnp.dot(h, war_ref[...], preferred_element_type=F32)) * u_r
         + _sigmoid(jnp.dot(h, wag_ref[...], preferred_element_type=F32)) * u_g)
    r = x_ref[...] + jnp.dot(m.astype(BF16), wo_ref[...], preferred_element_type=F32)
    hm = _rms(r, nm_ref[...]).astype(BF16)
    a = jnp.maximum(jnp.dot(hm, w1_ref[...], preferred_element_type=F32), 0.0)
    r = r + jnp.dot((a * a).astype(BF16), w2_ref[...], preferred_element_type=F32)
    hp = _rms(r, np_ref[...]).astype(BF16)
    gate = _sigmoid(jnp.dot(hp, wg_ref[...], preferred_element_type=F32))
    r = r + gate * jnp.dot(p_ref[...].astype(BF16), wp_ref[...], preferred_element_type=F32)
    y_ref[...] = _rms(r, nf_ref[...])


def _post(x, h, g_r, g_g, p, war, wag, wur, wug, wo, nm, w1, w2, npl, wg, wp, nf, tm, step=None):
    n, d = x.shape
    grid = n // tm
    rows = lambda w: pl.BlockSpec((tm, w), lambda i: (i, 0))
    const = lambda a: pl.BlockSpec(a.shape, lambda i: (0, 0), pipeline_mode=pl.Buffered(1))
    operands = (x, h, g_r, g_g, p, war, wag, wur, wug, wo, nm, w1, w2, npl, wg, wp, nf)
    in_specs = [rows(d), rows(d), rows(d), rows(d), rows(p.shape[1]),
                const(war), const(wag), const(wur), const(wug), const(wo), const(nm), const(w1), const(w2),
                const(npl), const(wg), const(wp), const(nf)]
    assert len(operands) == N_POST_IN
    out_specs = [rows(d)]
    out_shape = [jax.ShapeDtypeStruct((n, d), F32)]
    bb = 0
    if step is not None:
        bb = step[0].shape[0] // grid
        assert bb * grid == step[0].shape[0] and len(step) == N_STEP_IN
        s_operands, s_in, s_out, s_shape = _step_specs(*step, bb=bb)
        operands += s_operands
        in_specs += s_in
        out_specs += s_out
        out_shape += s_shape
    out = pl.pallas_call(
        functools.partial(_post_kernel, bb=bb),
        grid=(grid,),
        in_specs=in_specs,
        out_specs=out_specs,
        out_shape=out_shape,
        compiler_params=_params(("parallel",)),
        name="post",
    )(*operands)
    return out if bb else out[0]


def _rope_tables(pos):
    half = DK_R // 2
    inv = ROPE_BASE ** (-np.arange(half, dtype=np.float64) / half)
    ang = np.asarray(pos, np.float64)[:, None] * inv[None, :]
    cos, sin = np.cos(ang), np.sin(ang)
    return (jnp.asarray(np.concatenate([cos, cos], axis=-1), F32),
            jnp.asarray(np.concatenate([-sin, sin], axis=-1), F32))


def kernel(x_prompt, x_sample, state_ret, state_hgrn, p_prompt, p_sample, norm_mix_g, w_in,
           ret_norm_g, hg_norm_g, hg_lb, w_up_ret, w_up_hg, w_o, norm_mlp_g, w_ff1, w_ff2,
           norm_ple_g, w_ple_gate, w_ple_proj, norm_final_g):
    b, t, d = x_prompt.shape
    nb, td, _ = x_sample.shape
    assert w_in.shape[0] == 1 and td == 1 and hg_lb.shape[0] == 2
    row = lambda g: g.reshape(1, -1)
    merge_block = (2 * QK_R + 2 * V_R + 2 * F_G + 2 * V_G) // d
    g_mix = row(norm_mix_g[0])
    rgain = row(ret_norm_g[0])
    ggain = row(hg_norm_g[0])

    xs = x_sample.reshape(nb, d)
    hs = _norm(xs, g_mix, tm=nb)
    cos_s, sin_s = _rope_tables(PAST_LEN + np.arange(td))

    xp = x_prompt.reshape(b * t, d)
    cos_p, sin_p = _rope_tables(np.arange(t))
    casts = [(w_in, 1, merge_block), (w_in, 1, merge_block + 1), (w_up_ret, 1, None), (w_up_hg, 1, None),
             (w_o, 1, None), (w_ff1, 2, None), (w_ff2, 1, None), (w_ple_gate, 1, None)]
    gr_p, ret_p, hp3, war, wag, wur, wug, wo, w1, w2, wg, zs = _retention(
        x_prompt, g_mix, w_in, cos_p, sin_p, rgain, casts, hs)
    step = (zs, cos_s, sin_s, hg_lb, rgain, ggain, state_ret[0], state_hgrn[0])
    post_w = (war, wag, wur, wug, wo, row(norm_mlp_g[0]), w1, w2, row(norm_ple_g[0]), wg,
              w_ple_proj[0].astype(BF16), row(norm_final_g))
    gg_p, hg_p = _hgrn(hp3, w_in, hg_lb, ggain)
    y_p, gr_s, gg_s, ret_s, hg_s = _post(
        xp, hp3.reshape(b * t, d), gr_p.reshape(b * t, -1), gg_p.reshape(b * t, -1),
        p_prompt[0].reshape(b * t, -1), *post_w, tm=256, step=step)

    y_s = _post(xs, hs, gr_s.reshape(nb, -1), gg_s.reshape(nb, -1),
                p_sample[0].reshape(nb, -1), *post_w, tm=nb)

    return (y_p.reshape(b, t, d), y_s.reshape(nb, td, d), ret_p[None], hg_p[None],
            ret_s[None], hg_s[None])
```

```python
import functools

import jax
import jax.numpy as jnp
import numpy as np
from jax import lax
from jax.experimental import pallas as pl
from jax.experimental.pallas import tpu as pltpu

F32 = jnp.float32
BF16 = jnp.bfloat16

PAST_LEN = 16384
ROPE_BASE = 10000.0
EPS = 1e-6

H_R, DK_R, DV_R = 4, 128, 256
H_G, DK_G, DV_G = 8, 128, 128
QK_R = H_R * DK_R
V_R = H_R * DV_R
F_G = H_G * DK_G
V_G = H_G * DV_G

V7X_VMEM_BYTES = 64 * 1024 * 1024
VMEM_LIMIT = V7X_VMEM_BYTES - 8 * 1024 * 1024

RET_CHUNK = 128
RET_GROUP = 2
RET_LAG = 1
HG_CHUNK = 64
HG_SUB = 8
HG_GROUP = 4
HG_LAG = 2
HG_HEADS = 2

NT_DIMS = (((1,), (1,)), ((), ()))
TN_DIMS = (((0,), (0,)), ((), ()))


def _dot(a, b):
    return jnp.dot(a.astype(BF16), b.astype(BF16), preferred_element_type=F32)


def _dot_nt(a, b):
    return lax.dot_general(a.astype(BF16), b.astype(BF16), NT_DIMS, preferred_element_type=F32)


def _dot_tn(a, b):
    return lax.dot_general(a.astype(BF16), b.astype(BF16), TN_DIMS, preferred_element_type=F32)


def _rms(x, g):
    return x * lax.rsqrt(jnp.mean(x * x, axis=-1, keepdims=True) + EPS) * g


def _sigmoid(x):
    return 0.5 * jnp.tanh(0.5 * x) + 0.5


def _silu(x):
    half = 0.5 * x
    return half + half * jnp.tanh(half)


def _params(sem):
    return pltpu.CompilerParams(dimension_semantics=sem, vmem_limit_bytes=VMEM_LIMIT)


def _norm_kernel(x_ref, g_ref, h_ref):
    h_ref[...] = _rms(x_ref[...], g_ref[...]).astype(BF16)


def _norm(x, g, tm):
    n, d = x.shape
    return pl.pallas_call(
        _norm_kernel,
        grid=(n // tm,),
        in_specs=[pl.BlockSpec((tm, d), lambda i: (i, 0)), pl.BlockSpec((1, d), lambda i: (0, 0))],
        out_specs=pl.BlockSpec((tm, d), lambda i: (i, 0)),
        out_shape=jax.ShapeDtypeStruct((n, d), BF16),
        compiler_params=_params(("parallel",)),
        name="norm",
    )(x, g)


def _rotate(x, cos, sin_signed):
    return x * cos + pltpu.roll(x, DK_R // 2, 1) * sin_signed


def _head_norm_gate(o, gain, gate):
    return o * lax.rsqrt(jnp.mean(o * o, axis=-1, keepdims=True) + EPS) * gain * _silu(gate)


def _ret_pipeline(h_ref, wq_ref, wk_ref, wv_ref, wg_ref, cos_ref, sin_ref, dm_ref, qd_ref, kd_ref,
                  cd_ref, gain_ref, o_ref, s_ref, w_sc, z_sc, *, chunk, group, lag, norm_src=None):
    n_chunks = h_ref.shape[0] // chunk
    assert n_chunks % group == 0 and lag >= 1
    g_rows = group * chunk
    q_cols = slice(0, DK_R)
    k_cols = slice(DK_R, 2 * DK_R)
    v_cols = slice(2 * DK_R, 2 * DK_R + DV_R)
    g_cols = slice(2 * DK_R + DV_R, 2 * DK_R + 2 * DV_R)
    for cols, w_ref in ((q_cols, wq_ref), (k_cols, wk_ref), (v_cols, wv_ref), (g_cols, wg_ref)):
        w_sc[:, cols] = w_ref[...].astype(BF16)
    dm = dm_ref[...]
    qd = qd_ref[...]
    kd = kd_ref[...]
    cd = cd_ref[...]
    gain = gain_ref[...]

    assert group == 2
    pieces = (slice(0, g_cols.start), g_cols)

    def project(c):
        rows = slice(c // group * g_rows, (c // group + 1) * g_rows)
        pcols = pieces[c % group]
        if norm_src is not None and c % group == 0:
            x_ref, g_ref = norm_src
            h_ref[rows, :] = _rms(x_ref[rows, :], g_ref[...]).astype(BF16)
        z_sc[rows, pcols] = jnp.dot(h_ref[rows, :], w_sc[:, pcols], preferred_element_type=F32)

    def scores(r0):
        rows = slice(r0, r0 + chunk)
        cos = cos_ref[rows, :]
        sin = sin_ref[rows, :]
        q = _rotate(z_sc[rows, q_cols], cos, sin)
        k = _rotate(z_sc[rows, k_cols], cos, sin) * (DK_R ** -0.5)
        a = (_dot_nt(q, k) * dm).astype(BF16)
        return a, (q * qd).astype(BF16), (k * kd).astype(BF16)

    def outputs(r0, parts, s):
        a, q_in, k_st = parts
        rows = slice(r0, r0 + chunk)
        v = z_sc[rows, v_cols].astype(BF16)
        o = jnp.dot(jnp.concatenate([a, q_in], axis=1), jnp.concatenate([v, s.astype(BF16)], axis=0),
                    preferred_element_type=F32)
        o_ref[rows, :] = _head_norm_gate(o, gain, z_sc[rows, g_cols])
        return cd * s + lax.dot_general(k_st, v, TN_DIMS, preferred_element_type=F32)

    carry = {"s": jnp.zeros((DK_R, DV_R), F32)}
    pending = {}

    def do_scores(c):
        pending[c] = scores(c * chunk)

    def do_outputs(c):
        carry["s"] = outputs(c * chunk, pending.pop(c), carry["s"])

    work = []
    for step in range(-group, n_chunks + lag):
        if step + group < n_chunks:
            work.append(functools.partial(project, step + group))
        if 0 <= step < n_chunks:
            work.append(functools.partial(do_scores, step))
        if 0 <= step - lag < n_chunks:
            work.append(functools.partial(do_outputs, step - lag))

    def finish():
        s_ref[...] = carry["s"]

    return work, finish


def _retention_tables(c):
    lg = np.log(1.0 - 2.0 ** (-5.0 - np.arange(H_R, dtype=np.float64)))
    idx = np.arange(c, dtype=np.float64)
    rel = idx[:, None] - idx[None, :]
    dm = np.where(rel >= 0, np.exp(lg[:, None, None] * np.maximum(rel, 0.0)), 0.0)
    qd = np.broadcast_to(np.exp(lg[:, None] * (idx + 1.0))[:, :, None], (H_R, c, DK_R))
    kd = np.broadcast_to(np.exp(lg[:, None] * (c - 1.0 - idx))[:, :, None], (H_R, c, DK_R))
    cd = np.broadcast_to(np.exp(lg * c)[:, None, None], (H_R, 1, DV_R))
    return tuple(jnp.asarray(a, F32) for a in (dm, qd, kd, cd))


def _lower_bound(lb_ref):
    a = lb_ref[...]
    a0, a1 = a[0:1, :], a[1:2, :]
    m = jnp.maximum(a0, a1)
    e0 = jnp.exp(a0 - m)
    e1 = jnp.exp(a1 - m)
    return e0 / (e0 + e1)


def _forget_and_key(g, lb):
    half_scale = 0.5 * (1.0 - lb)
    th_scaled = half_scale * jnp.tanh(0.5 * g)
    return lb + jnp.maximum(half_scale + th_scaled, 0.0), jnp.maximum(half_scale - th_scaled, 0.0)


def _cumsum_rows(tri, x):
    hi = x.astype(BF16)
    rest = x - hi.astype(F32)
    mid = rest.astype(BF16)
    lo = (rest - mid.astype(F32)).astype(BF16)
    y = jnp.dot(tri, jnp.concatenate([hi, mid, lo], axis=1), preferred_element_type=F32)
    n = x.shape[1]
    return y[:, :n] + y[:, n:2 * n] + y[:, 2 * n:]


def _hgrn_pipeline(h_ref, wq_ref, wf_ref, wi_ref, wg_ref, lb_ref, gain_ref, o_ref, s_ref,
                   w_sc, z_sc, q_sc, b_sc, c_sc, *, chunk, sub, group, lag, heads):
    assert sub == 8 and chunk % (2 * sub) == 0
    g_rows = group * chunk
    n_chunks = h_ref.shape[0] // chunk
    assert n_chunks % group == 0 and lag >= 1
    lanes = lambda hh: slice(hh * DK_G, (hh + 1) * DK_G)
    cols = lambda hh, role: lanes(4 * hh + role)
    for hh in range(heads):
        for role, w_ref in enumerate((wq_ref, wf_ref, wi_ref, wg_ref)):
            w_sc[:, cols(hh, role)] = w_ref[:, lanes(hh)].astype(BF16)

    lb = _lower_bound(lb_ref)
    gain = gain_ref[...]
    r64 = lax.broadcasted_iota(jnp.int32, (chunk, chunk), 0)
    c64 = lax.broadcasted_iota(jnp.int32, (chunk, chunk), 1)
    tri = jnp.where(r64 >= c64, 1.0, 0.0).astype(BF16)
    same_sub = (r64 // sub) == (c64 // sub)
    place = [(c64[:sub] % sub == s) & (r64[:sub] >= s) for s in range(sub)]
    zrows = lambda n: jnp.zeros((n, DK_G), F32)

    piece_cols = z_sc.shape[1] // group
    assert piece_cols % DK_G == 0

    def project(c):
        rows = slice(c // group * g_rows, (c // group + 1) * g_rows)
        pcols = slice(c % group * piece_cols, (c % group + 1) * piece_cols)
        z_sc[rows, pcols] = jnp.dot(h_ref[rows, :], w_sc[:, pcols], preferred_element_type=F32)

    def gates(hh, r0):
        rows = slice(r0, r0 + chunk)
        f, k = _forget_and_key(z_sc[rows, cols(hh, 1)], lb[:, lanes(hh)])
        b_sc[rows, lanes(hh)] = _cumsum_rows(tri, jnp.log2(f))
        c_sc[rows, lanes(hh)] = b_sc[rows, lanes(hh)] - jnp.log2(k)
        q_sc[rows, lanes(hh)] = _silu(z_sc[rows, cols(hh, 0)])

    def diagonal(hh, r0):
        blocks = []
        for o in range(r0, r0 + chunk, sub):
            q, b = q_sc[o:o + sub, lanes(hh)], b_sc[o:o + sub, lanes(hh)]
            acc = jnp.zeros((sub, chunk), F32)
            for s in range(sub):
                p = jnp.exp2(b - c_sc[o + s:o + s + 1, lanes(hh)]) * q
                acc = jnp.where(place[s], jnp.sum(p, axis=-1, keepdims=True), acc)
            blocks.append(acc)
        return jnp.where(same_sub, jnp.concatenate(blocks, axis=0), 0.0)

    def below(hh, r0, w):
        qs, ks = [], []
        for mid in range(r0 + w, r0 + chunk, 2 * w):
            ref_b = b_sc[mid - 1:mid, lanes(hh)]
            qs += [zrows(w), q_sc[mid:mid + w, lanes(hh)] * jnp.exp2(b_sc[mid:mid + w, lanes(hh)] - ref_b)]
            ks += [jnp.exp2(ref_b - c_sc[mid - w:mid, lanes(hh)]), zrows(w)]
        scores = _dot_nt(jnp.concatenate(qs, axis=0), jnp.concatenate(ks, axis=0))
        return jnp.where(c64 // w == r64 // w - 1, scores, 0.0)

    def scores(hh, r0):
        a = diagonal(hh, r0)
        w = chunk // 2
        while w >= sub:
            a = a + below(hh, r0, w)
            w //= 2
        return a

    def outputs(hh, r0, a, st):
        rows = slice(r0, r0 + chunk)
        v = z_sc[rows, cols(hh, 2)].astype(BF16)
        b_last = b_sc[rows.stop - 1:rows.stop, lanes(hh)]
        o = _dot(a, v) + _dot_nt(q_sc[rows, lanes(hh)] * jnp.exp2(b_sc[rows, lanes(hh)]), st)
        o_ref[rows, lanes(hh)] = _head_norm_gate(o, gain[:, lanes(hh)], z_sc[rows, cols(hh, 3)])
        return st * jnp.exp2(b_last) + _dot_tn(v, jnp.exp2(b_last - c_sc[rows, lanes(hh)]))

    st = [jnp.zeros((DV_G, DK_G), F32) for _ in range(heads)]
    pending = {}

    def do_scores(hh, c):
        pending[hh, c] = scores(hh, c * chunk)

    def do_outputs(hh, c):
        st[hh] = outputs(hh, c * chunk, pending.pop((hh, c)), st[hh])

    work = []
    for step in range(-group, n_chunks + 2 * lag):
        if step + group < n_chunks:
            work.append(functools.partial(project, step + group))
        for hh in range(heads):
            if 0 <= step < n_chunks:
                work.append(functools.partial(gates, hh, step * chunk))
            if 0 <= step - lag < n_chunks:
                work.append(functools.partial(do_scores, hh, step - lag))
            if 0 <= step - 2 * lag < n_chunks:
                work.append(functools.partial(do_outputs, hh, step - 2 * lag))

    def finish():
        for hh in range(heads):
            s_ref[hh] = st[hh].T

    return work, finish


def _run_pipeline(build, **config):
    def kernel(*refs):
        work, finish = build(*refs, **config)
        for item in work:
            item()
        finish()
    return kernel


N_RET_IN = 11


def _ret_kernel(x_ref, g_ref, *refs, n_cast, **config):
    ret_in, cast_in = refs[:N_RET_IN], refs[N_RET_IN:N_RET_IN + n_cast]
    hs_ref, wz_ref = refs[N_RET_IN + n_cast:N_RET_IN + n_cast + 2]
    outs = refs[N_RET_IN + n_cast + 2:]
    (o_ref, s_ref, hn_ref), cast_out, zs_ref, scratch = outs[:3], outs[3:3 + n_cast], outs[3 + n_cast], outs[4 + n_cast:]

    config = dict(config, norm_src=(x_ref, g_ref))
    for src, dst in zip(cast_in, cast_out):
        dst[...] = src[...].astype(BF16)
    zs_ref[...] = jnp.dot(hs_ref[...], wz_ref[...].astype(BF16), preferred_element_type=F32)
    work, finish = _ret_pipeline(hn_ref, *ret_in, o_ref, s_ref, *scratch, **config)
    for item in work:
        item()
    finish()


def _retention(x3, g_mix, w_in, cos, sin, gain, casts, hs):
    b, t, d = x3.shape
    n_head = 2 * DK_R + 2 * DV_R
    steps = b * H_R
    d_in = w_in.shape[2]
    z_cols = DK_R * -(-d_in // (DK_R * steps))
    assert d_in % z_cols == 0
    z_block = lambda bi, h: jnp.minimum(bi * H_R + h, d_in // z_cols - 1)
    w_block = lambda width, lo: pl.BlockSpec((None, d, width), lambda bi, h: (0, 0, lo // width + h))
    head_tab = lambda rows, cols: pl.BlockSpec((None, rows, cols), lambda bi, h: (h, 0, 0))
    cast_in, cast_out, cast_shape = [], [], []
    for a, axis, col_block in casts:
        _, rows, cols = a.shape
        if axis == 1:
            n = rows // steps
            cols = cols if col_block is None else d
            cb = 0 if col_block is None else col_block
            cast_in.append(pl.BlockSpec((None, n, cols), lambda bi, h, cb=cb: (0, bi * H_R + h, cb)))
            cast_out.append(pl.BlockSpec((n, cols), lambda bi, h: (bi * H_R + h, 0)))
        else:
            n = cols // steps
            cast_in.append(pl.BlockSpec((None, rows, n), lambda bi, h: (0, 0, bi * H_R + h)))
            cast_out.append(pl.BlockSpec((rows, n), lambda bi, h: (0, bi * H_R + h)))
        cast_shape.append(jax.ShapeDtypeStruct((rows, cols), BF16))
    return pl.pallas_call(
        functools.partial(_ret_kernel, n_cast=len(casts),
                          chunk=RET_CHUNK, group=RET_GROUP, lag=RET_LAG),
        grid=(b, H_R),
        in_specs=[
            pl.BlockSpec((None, t, d), lambda bi, h: (bi, 0, 0)),
            pl.BlockSpec((1, d), lambda bi, h: (0, 0)),
            w_block(DK_R, 0), w_block(DK_R, QK_R), w_block(DV_R, 2 * QK_R), w_block(DV_R, 2 * QK_R + V_R),
            pl.BlockSpec((t, DK_R), lambda bi, h: (0, 0)),
            pl.BlockSpec((t, DK_R), lambda bi, h: (0, 0)),
            head_tab(RET_CHUNK, RET_CHUNK), head_tab(RET_CHUNK, DK_R), head_tab(RET_CHUNK, DK_R),
            head_tab(1, DV_R),
            pl.BlockSpec((1, DV_R), lambda bi, h: (0, h)),
        ] + cast_in + [
            pl.BlockSpec(hs.shape, lambda bi, h: (0, 0)),
            pl.BlockSpec((None, d, z_cols), lambda bi, h: (0, 0, z_block(bi, h))),
        ],
        out_specs=[
            pl.BlockSpec((None, t, DV_R), lambda bi, h: (bi, 0, h)),
            pl.BlockSpec((None, None, DK_R, DV_R), lambda bi, h: (bi, h, 0, 0)),
            pl.BlockSpec((None, t, d), lambda bi, h: (bi, 0, 0)),
        ] + cast_out + [pl.BlockSpec((hs.shape[0], z_cols), lambda bi, h: (0, z_block(bi, h)))],
        out_shape=[
            jax.ShapeDtypeStruct((b, t, V_R), F32),
            jax.ShapeDtypeStruct((b, H_R, DK_R, DV_R), F32),
            jax.ShapeDtypeStruct((b, t, d), BF16),
        ] + cast_shape + [jax.ShapeDtypeStruct((hs.shape[0], d_in), F32)],
        scratch_shapes=[pltpu.VMEM((d, n_head), BF16), pltpu.VMEM((t, n_head), F32)],
        compiler_params=_params(("arbitrary", "arbitrary")),
        name="retention",
    )(x3, g_mix, w_in, w_in, w_in, w_in, cos, sin, *_retention_tables(RET_CHUNK), gain,
      *(a for a, _, _ in casts), hs, w_in)


def _hgrn(h3, w_in, hg_lb, gain):
    b, t, d = h3.shape
    heads = HG_HEADS
    wide = heads * DK_G
    first = (2 * QK_R + 2 * V_R) // wide
    w_block = lambda role: pl.BlockSpec((None, d, wide), lambda bi, h: (0, 0, first + role * (H_G // heads) + h))
    return pl.pallas_call(
        _run_pipeline(_hgrn_pipeline, chunk=HG_CHUNK, sub=HG_SUB, group=HG_GROUP, lag=HG_LAG, heads=heads),
        grid=(b, H_G // heads),
        in_specs=[
            pl.BlockSpec((None, t, d), lambda bi, h: (bi, 0, 0)),
            w_block(0), w_block(1), w_block(2), w_block(3),
            pl.BlockSpec((2, wide), lambda bi, h: (0, h)),
            pl.BlockSpec((1, wide), lambda bi, h: (0, h)),
        ],
        out_specs=[
            pl.BlockSpec((None, t, wide), lambda bi, h: (bi, 0, h)),
            pl.BlockSpec((None, heads, DK_G, DV_G), lambda bi, h: (bi, h, 0, 0)),
        ],
        out_shape=[
            jax.ShapeDtypeStruct((b, t, V_G), F32),
            jax.ShapeDtypeStruct((b, H_G, DK_G, DV_G), F32),
        ],
        scratch_shapes=[
            pltpu.VMEM((d, 4 * wide), BF16),
            pltpu.VMEM((t, 4 * wide), F32),
            pltpu.VMEM((t, wide), F32),
            pltpu.VMEM((t, wide), F32),
            pltpu.VMEM((t, wide), F32),
        ],
        compiler_params=_params(("parallel", "arbitrary")),
        name="hgrn",
    )(h3, w_in, w_in, w_in, w_in, hg_lb, gain)


def _step_kernel(z_ref, cos_ref, sin_ref, lb_ref, rgain_ref, ggain_ref, sr0_ref, sg0_ref,
                 or_ref, og_ref, sr_ref, sg_ref, *, bb):
    first = lambda a: jnp.where(lax.broadcasted_iota(jnp.int32, a.shape, 0) == 0, a, 0.0)
    cos = cos_ref[...]
    sin = sin_ref[...]
    lb_all = _lower_bound(lb_ref)
    row8 = lax.broadcasted_iota(jnp.int32, (8, DK_G), 0)
    ones3 = jnp.where(row8 < 3, 1.0, 0.0)
    g0 = 2 * QK_R + 2 * V_R
    base = (pl.program_id(0) % (8 // bb)) * bb
    work = []
    for i in range(bb):
        def z_row(lo, n, i=i):
            mine = lax.broadcasted_iota(jnp.int32, (8, n), 0) == base + i
            return jnp.sum(jnp.where(mine, z_ref[:, lo:lo + n], 0.0), axis=0, keepdims=True)

        z8 = lambda lo, n, z_row=z_row: jnp.broadcast_to(z_row(lo, n), (8, n))
        for h in range(H_R):
            q = _rotate(z8(h * DK_R, DK_R), cos, sin)
            k = _rotate(z8(QK_R + h * DK_R, DK_R), cos, sin) * (DK_R ** -0.5)
            v = z8(2 * QK_R + h * DV_R, DV_R)
            lanes = slice(h * DV_R, (h + 1) * DV_R)
            gate = z_row(2 * QK_R + V_R + h * DV_R, DV_R)
            work.append((sr0_ref, sr_ref, or_ref, rgain_ref, i, h, lanes, q, gate,
                         1.0 - 2.0 ** (-5.0 - h), _dot_tn(first(k), v)))
        for h in range(H_G):
            lanes = slice(h * DK_G, (h + 1) * DK_G)
            q = _silu(z8(g0 + h * DK_G, DK_G))
            f, k = _forget_and_key(z8(g0 + F_G + h * DK_G, DK_G), lb_all[:, lanes])
            v = z8(g0 + 2 * F_G + h * DV_G, DV_G)
            gate = z_row(g0 + 2 * F_G + V_G + h * DV_G, DV_G)
            f_hi = f.astype(BF16).astype(F32)
            f_mid = (f - f_hi).astype(BF16).astype(F32)
            f_lo = f - f_hi - f_mid
            lhs = jnp.where(row8 == 0, f_hi, jnp.where(row8 == 1, f_mid, jnp.where(row8 == 2, f_lo,
                            jnp.where(row8 == 3, k, 0.0))))
            rhs = jnp.concatenate([ones3, jnp.where(row8 == 3, v, 0.0)], axis=1)
            both = _dot_tn(lhs, rhs)
            work.append((sg0_ref, sg_ref, og_ref, ggain_ref, i, h, lanes, q, gate,
                         both[:, :DV_G], both[:, DV_G:]))
    outs = []
    for s0_ref, s_ref, _, _, i, h, _, q, _, decay, kv in work:
        s_new = decay * s0_ref[i, h] + kv
        s_ref[i, h] = s_new
        outs.append(_dot(q, s_new)[0:1, :])
    for (_, _, o_ref, gain_ref, i, _, lanes, _, gate, _, _), o in zip(work, outs):
        o_ref[i, :, lanes] = _head_norm_gate(o, gain_ref[:, lanes], gate)


def _step_specs(z, cos, sin, hg_lb, rgain, ggain, sr0, sg0, bb):
    n, d_in = z.shape
    assert 8 % bb == 0
    row = lambda w: pl.BlockSpec((bb, 1, w), lambda i: (i, 0, 0))
    const = lambda a: pl.BlockSpec(a.shape, lambda i: (0,) * a.ndim)
    sr_spec = pl.BlockSpec((bb, H_R, DK_R, DV_R), lambda i: (i, 0, 0, 0))
    sg_spec = pl.BlockSpec((bb, H_G, DK_G, DV_G), lambda i: (i, 0, 0, 0))
    operands = (z, cos, sin, hg_lb, rgain, ggain, sr0, sg0)
    in_specs = [pl.BlockSpec((8, d_in), lambda i: (i * bb // 8, 0)),
                const(cos), const(sin), const(hg_lb), const(rgain), const(ggain), sr_spec, sg_spec]
    out_specs = [row(V_R), row(V_G), sr_spec, sg_spec]
    out_shape = [
        jax.ShapeDtypeStruct((n, 1, V_R), F32),
        jax.ShapeDtypeStruct((n, 1, V_G), F32),
        jax.ShapeDtypeStruct(sr0.shape, F32),
        jax.ShapeDtypeStruct(sg0.shape, F32),
    ]
    return operands, in_specs, out_specs, out_shape


N_POST_IN = 17
N_STEP_IN = 8


def _post_kernel(*refs, bb):
    if bb:
        _post_body(*refs[:N_POST_IN], refs[N_POST_IN + N_STEP_IN])
        _step_kernel(*refs[N_POST_IN:N_POST_IN + N_STEP_IN], *refs[N_POST_IN + N_STEP_IN + 1:], bb=bb)
    else:
        _post_body(*refs)


def _post_body(x_ref, h_ref, gr_ref, gg_ref, p_ref, war_ref, wag_ref, wur_ref, wug_ref, wo_ref,
               nm_ref, w1_ref, w2_ref, np_ref, wg_ref, wp_ref, nf_ref, y_ref):
    h = h_ref[...]
    u_r = jnp.dot(gr_ref[...].astype(BF16), wur_ref[...], preferred_element_type=F32)
    u_g = jnp.dot(gg_ref[...].astype(BF16), wug_ref[...], preferred_element_type=F32)
    m = (_sigmoid(jnp.dot(h, war_ref[...], preferred_element_type=F32)) * u_r
         + _sigmoid(jnp.dot(h, wag_ref[...], preferred_element_type=F32)) * u_g)
    r = x_ref[...] + jnp.dot(m.astype(BF16), wo_ref[...], preferred_element_type=F32)
    hm = _rms(r, nm_ref[...]).astype(BF16)
    a = jnp.maximum(jnp.dot(hm, w1_ref[...], preferred_element_type=F32), 0.0)
    r = r + jnp.dot((a * a).astype(BF16), w2_ref[...], preferred_element_type=F32)
    hp = _rms(r, np_ref[...]).astype(BF16)
    gate = _sigmoid(jnp.dot(hp, wg_ref[...], preferred_element_type=F32))
    r = r + gate * jnp.dot(p_ref[...].astype(BF16), wp_ref[...], preferred_element_type=F32)
    y_ref[...] = _rms(r, nf_ref[...])


def _post(x, h, g_r, g_g, p, war, wag, wur, wug, wo, nm, w1, w2, npl, wg, wp, nf, tm, step=None):
    n, d = x.shape
    grid = n // tm
    rows = lambda w: pl.BlockSpec((tm, w), lambda i: (i, 0))
    const = lambda a: pl.BlockSpec(a.shape, lambda i: (0, 0), pipeline_mode=pl.Buffered(1))
    operands = (x, h, g_r, g_g, p, war, wag, wur, wug, wo, nm, w1, w2, npl, wg, wp, nf)
    in_specs = [rows(d), rows(d), rows(d), rows(d), rows(p.shape[1]),
                const(war), const(wag), const(wur), const(wug), const(wo), const(nm), const(w1), const(w2),
                const(npl), const(wg), const(wp), const(nf)]
    assert len(operands) == N_POST_IN
    out_specs = [rows(d)]
    out_shape = [jax.ShapeDtypeStruct((n, d), F32)]
    bb = 0
    if step is not None:
        bb = step[0].shape[0] // grid
        assert bb * grid == step[0].shape[0] and len(step) == N_STEP_IN
        s_operands, s_in, s_out, s_shape = _step_specs(*step, bb=bb)
        operands += s_operands
        in_specs += s_in
        out_specs += s_out
        out_shape += s_shape
    out = pl.pallas_call(
        functools.partial(_post_kernel, bb=bb),
        grid=(grid,),
        in_specs=in_specs,
        out_specs=out_specs,
        out_shape=out_shape,
        compiler_params=_params(("parallel",)),
        name="post",
    )(*operands)
    return out if bb else out[0]


def _rope_tables(pos):
    half = DK_R // 2
    inv = ROPE_BASE ** (-np.arange(half, dtype=np.float64) / half)
    ang = np.asarray(pos, np.float64)[:, None] * inv[None, :]
    cos, sin = np.cos(ang), np.sin(ang)
    return (jnp.asarray(np.concatenate([cos, cos], axis=-1), F32),
            jnp.asarray(np.concatenate([-sin, sin], axis=-1), F32))


def kernel(x_prompt, x_sample, state_ret, state_hgrn, p_prompt, p_sample, norm_mix_g, w_in,
           ret_norm_g, hg_norm_g, hg_lb, w_up_ret, w_up_hg, w_o, norm_mlp_g, w_ff1, w_ff2,
           norm_ple_g, w_ple_gate, w_ple_proj, norm_final_g):
    b, t, d = x_prompt.shape
    nb, td, _ = x_sample.shape
    assert w_in.shape[0] == 1 and td == 1 and hg_lb.shape[0] == 2
    row = lambda g: g.reshape(1, -1)
    merge_block = (2 * QK_R + 2 * V_R + 2 * F_G + 2 * V_G) // d
    g_mix = row(norm_mix_g[0])
    rgain = row(ret_norm_g[0])
    ggain = row(hg_norm_g[0])

    xs = x_sample.reshape(nb, d)
    hs = _norm(xs, g_mix, tm=nb)
    cos_s, sin_s = _rope_tables(PAST_LEN + np.arange(td))

    xp = x_prompt.reshape(b * t, d)
    cos_p, sin_p = _rope_tables(np.arange(t))
    casts = [(w_in, 1, merge_block), (w_in, 1, merge_block + 1), (w_up_ret, 1, None), (w_up_hg, 1, None),
             (w_o, 1, None), (w_ff1, 2, None), (w_ff2, 1, None), (w_ple_gate, 1, None)]
    gr_p, ret_p, hp3, war, wag, wur, wug, wo, w1, w2, wg, zs = _retention(
        x_prompt, g_mix, w_in, cos_p, sin_p, rgain, casts, hs)
    step = (zs, cos_s, sin_s, hg_lb, rgain, ggain, state_ret[0], state_hgrn[0])
    post_w = (war, wag, wur, wug, wo, row(norm_mlp_g[0]), w1, w2, row(norm_ple_g[0]), wg,
              w_ple_proj[0].astype(BF16), row(norm_final_g))
    gg_p, hg_p = _hgrn(hp3, w_in, hg_lb, ggain)
    y_p, gr_s, gg_s, ret_s, hg_s = _post(
        xp, hp3.reshape(b * t, d), gr_p.reshape(b * t, -1), gg_p.reshape(b * t, -1),
        p_prompt[0].reshape(b * t, -1), *post_w, tm=256, step=step)

    y_s = _post(xs, hs, gr_s.reshape(nb, -1), gg_s.reshape(nb, -1),
                p_sample[0].reshape(nb, -1), *post_w, tm=nb)

    return (y_p.reshape(b, t, d), y_s.reshape(nb, td, d), ret_p[None], hg_p[None],
            ret_s[None], hg_s[None])
```
